```python
import math
import jax, jax.numpy as jnp
from jax import lax
import numpy as np

D_MODEL = 2048
BATCH = 4
SEQ = 4096
DEPTH = 2

MIX_WIDTH = D_MODEL
GROUP_WIDTH = MIX_WIDTH // 4
POOL_WINDOWS = (2, 4, 8, 16)
POOL_GROUPS = len(POOL_WINDOWS)
POOL_CH = GROUP_WIDTH // POOL_GROUPS
DIFF_V_DIM = 128
DIFF_QK_DIM = DIFF_V_DIM // 2
DIFF_HEADS = GROUP_WIDTH // DIFF_V_DIM
DIL_HEAD_DIM = 128
DIL_HEADS = GROUP_WIDTH // DIL_HEAD_DIM
DILATED_PAIRS = ((128, 1), (512, 4), (2048, 16))
CONV_CH = GROUP_WIDTH
CONV_WIDTH = 31
N_EXPERTS = 32
TOP_K = 4
EXPERT_HIDDEN = D_MODEL
SWIGLU_ALPHA = 1.702
SWIGLU_LIMIT = 7.0
ROPE_THETA = 500000.0
ROPE_FRACTION = 4
Q_BLOCK = 128
RMS_EPS = 1e-6
LN_EPS = 1e-5
NEG_INF = -1e30
N_MOD = 6
IN_WIDTHS = (GROUP_WIDTH,
             DIFF_HEADS * 2 * DIFF_QK_DIM,
             DIFF_HEADS * 2 * DIFF_QK_DIM,
             DIFF_HEADS * DIFF_V_DIM,
             DIL_HEADS * DIL_HEAD_DIM,
             DIL_HEADS * DIL_HEAD_DIM,
             DIL_HEADS * DIL_HEAD_DIM,
             2 * CONV_CH)
IN_WIDTH = sum(IN_WIDTHS)
IN_SPLITS = tuple(int(s) for s in np.cumsum(IN_WIDTHS)[:-1])

kernel_name = 'hybrid_parallel_mixer_moe_block'


def rms_norm(x, g):
    xf = x.astype(jnp.float32)
    y = xf * lax.rsqrt(jnp.mean(xf * xf, axis=-1, keepdims=True) + RMS_EPS)
    return (y * g.astype(jnp.float32)).astype(x.dtype)


def layer_norm(x, g, b):
    xf = x.astype(jnp.float32)
    mu = jnp.mean(xf, axis=-1, keepdims=True)
    var = jnp.mean(jnp.square(xf - mu), axis=-1, keepdims=True)
    y = (xf - mu) * lax.rsqrt(var + LN_EPS)
    return (y * g.astype(jnp.float32) + b.astype(jnp.float32)).astype(x.dtype)


def partial_rope(x, positions):
    hd = x.shape[-1]
    rot = hd // ROPE_FRACTION
    half = rot // 2
    inv_freq = ROPE_THETA ** (-jnp.arange(half, dtype=jnp.float32) / half)
    ang = positions.astype(jnp.float32)[..., None] * inv_freq
    ang = ang.reshape(ang.shape[:2] + (1,) * (x.ndim - 3) + (half,))
    cos, sin = jnp.cos(ang), jnp.sin(ang)
    xf = x.astype(jnp.float32)
    x1, x2 = xf[..., :half], xf[..., half:rot]
    rotated = jnp.concatenate([x1 * cos - x2 * sin, x2 * cos + x1 * sin], axis=-1).astype(x.dtype)
    return jnp.concatenate([rotated, x[..., rot:]], axis=-1)


def pool_mixer(u, pool_w, pool_scale):
    B, T, _ = u.shape
    ug = u.reshape(B, T, POOL_GROUPS, POOL_CH).astype(jnp.float32)
    count_base = jnp.arange(1, T + 1, dtype=jnp.float32)
    outs = []
    for g, w in enumerate(POOL_WINDOWS):
        xg = ug[:, :, g]
        cs = jnp.cumsum(xg, axis=1)
        lagged = jnp.pad(cs, ((0, 0), (w, 0), (0, 0)))[:, :T]
        mean = (cs - lagged) / jnp.minimum(count_base, float(w))[None, :, None]
        outs.append(mean - xg)
    pooled = jnp.stack(outs, axis=2).astype(u.dtype)
    y = jnp.einsum('btgc,gcd->btgd', pooled, pool_w).reshape(B, T, GROUP_WIDTH)
    return y * pool_scale


def diff_attention(q, k, v, lam):
    B, T, H, _, dk = q.shape
    dv = v.shape[-1]
    nb = T // Q_BLOCK
    scale = dk ** -0.5
    kh = k.transpose(0, 2, 3, 1, 4)
    vh = v.transpose(0, 2, 1, 3)
    qb = q.transpose(0, 2, 3, 1, 4).reshape(B, H, 2, nb, Q_BLOCK, dk).transpose(3, 0, 1, 2, 4, 5)
    key_pos = jnp.arange(T)

    def one_block(args):
        qblk, i = args
        s = jnp.einsum('bhmqd,bhmkd->bhmqk', qblk, kh).astype(jnp.float32) * scale
        q_pos = i * Q_BLOCK + jnp.arange(Q_BLOCK)
        s = jnp.where(key_pos[None, :] <= q_pos[:, None], s, NEG_INF)
        p = jax.nn.softmax(s, axis=-1)
        a = p[:, :, 0] - lam * p[:, :, 1]
        return jnp.einsum('bhqk,bhkd->bhqd', a.astype(vh.dtype), vh)

    o = lax.map(one_block, (qb, jnp.arange(nb)))
    return o.transpose(1, 0, 3, 2, 4).reshape(B, T, H, dv)


def banded_causal_attention(q, k, v, band):
    lead = q.shape[:-2]
    L, hd = q.shape[-2:]
    nb = L // band
    qb = q.reshape(lead + (nb, band, hd))

    def with_prev(x):
        xb = x.reshape(lead + (nb, band, hd))
        prev = jnp.concatenate([jnp.zeros_like(xb[..., :1, :, :]), xb[..., :-1, :, :]], axis=-3)
        return jnp.concatenate([prev, xb], axis=-2)

    kk, vv = with_prev(k), with_prev(v)
    s = jnp.einsum('...nqd,...nkd->...nqk', qb, kk).astype(jnp.float32) * (hd ** -0.5)
    a = jnp.arange(band)[:, None]
    b = jnp.arange(2 * band)[None, :]
    in_band = (b >= a) & (b <= a + band)
    valid = in_band[None] & ((jnp.arange(nb) > 0)[:, None, None] | (b >= band)[None])
    s = jnp.where(valid, s, NEG_INF)
    lse = jax.nn.logsumexp(s, axis=-1)
    p = jnp.exp(s - lse[..., None])
    o = jnp.einsum('...nqk,...nkd->...nqd', p.astype(v.dtype), vv)
    return o.reshape(lead + (L, hd)), lse.reshape(lead + (L,))


def dilated_attention(q, k, v):
    B, T, H, hd = q.shape
    outs, lses = [], []
    for window, dil in DILATED_PAIRS:
        band = window // dil
        t_pad = -(-T // window) * window
        L = t_pad // dil
        pad = ((0, 0), (0, t_pad - T), (0, 0), (0, 0))
        qs = jnp.pad(q, pad).reshape(B, L, dil, H, hd).transpose(0, 2, 3, 1, 4)
        ks = jnp.pad(k, pad).reshape(B, L, dil, H, hd).transpose(0, 2, 3, 1, 4)
        vs = jnp.pad(v, pad).reshape(B, L, dil, H, hd).transpose(0, 2, 3, 1, 4)
        o, lse = banded_causal_attention(qs, ks, vs, band)
        outs.append(o.transpose(0, 3, 1, 2, 4).reshape(B, t_pad, H, hd)[:, :T])
        lses.append(lse.transpose(0, 3, 1, 2).reshape(B, t_pad, H)[:, :T])
    wts = jax.nn.softmax(jnp.stack(lses), axis=0)
    out = jnp.sum(wts[..., None] * jnp.stack(outs).astype(jnp.float32), axis=0)
    return out.astype(q.dtype)


def conv_module(z, dw_w, dw_b, ln_g, ln_b, pw_w, pw_b):
    a, g = jnp.split(z, 2, axis=-1)
    u = a * jax.nn.sigmoid(g)
    u = lax.conv_general_dilated(u, dw_w[:, None, :], window_strides=(1,),
                                 padding=[(CONV_WIDTH - 1, 0)],
                                 dimension_numbers=('NWC', 'WIO', 'NWC'),
                                 feature_group_count=CONV_CH) + dw_b
    u = jax.nn.silu(layer_norm(u, ln_g, ln_b))
    return u @ pw_w + pw_b


def token_mixing(h, positions, lambda_init, w_in, pool_w, pool_scale, lq1, lk1, lq2, lk2, subln_g,
                 dw_w, dw_b, ln_g, ln_b, pw_w, pw_b, w_out):
    B, T, _ = h.shape
    z = h @ w_in
    z_pool, q_d, k_d, v_d, q_c, k_c, v_c, z_conv = jnp.split(z, IN_SPLITS, axis=-1)
    y_pool = pool_mixer(z_pool, pool_w, pool_scale)
    qd = partial_rope(q_d.reshape(B, T, DIFF_HEADS, 2, DIFF_QK_DIM), positions)
    kd = partial_rope(k_d.reshape(B, T, DIFF_HEADS, 2, DIFF_QK_DIM), positions)
    vd = v_d.reshape(B, T, DIFF_HEADS, DIFF_V_DIM)
    f32 = jnp.float32
    lam = (jnp.exp(jnp.sum(lq1.astype(f32) * lk1.astype(f32)))
           - jnp.exp(jnp.sum(lq2.astype(f32) * lk2.astype(f32))) + lambda_init)
    od = diff_attention(qd, kd, vd, lam)
    y_diff = (rms_norm(od, subln_g) * (1.0 - lambda_init)).reshape(B, T, GROUP_WIDTH)
    qc = partial_rope(q_c.reshape(B, T, DIL_HEADS, DIL_HEAD_DIM), positions)
    kc = partial_rope(k_c.reshape(B, T, DIL_HEADS, DIL_HEAD_DIM), positions)
    vc = v_c.reshape(B, T, DIL_HEADS, DIL_HEAD_DIM)
    y_dil = dilated_attention(qc, kc, vc).reshape(B, T, GROUP_WIDTH)
    y_conv = conv_module(z_conv, dw_w, dw_b, ln_g, ln_b, pw_w, pw_b)
    return jnp.concatenate([y_pool, y_diff, y_dil, y_conv], axis=-1) @ w_out


def clamped_swiglu(hh):
    g = jnp.minimum(hh[..., ::2], SWIGLU_LIMIT)
    lin = jnp.clip(hh[..., 1::2], -SWIGLU_LIMIT, SWIGLU_LIMIT)
    return g * jax.nn.sigmoid(SWIGLU_ALPHA * g) * (lin + 1.0)


def moe(h, router_w, router_b, w1, b1, w2, b2):
    B, T, D = h.shape
    xf = h.reshape(B * T, D)
    logits = (xf @ router_w + router_b).astype(jnp.float32)
    vals, idx = lax.top_k(logits, TOP_K)
    wts = jax.nn.softmax(vals, axis=-1)
    gate = jnp.sum(jax.nn.one_hot(idx, N_EXPERTS, dtype=jnp.float32) * wts[..., None], axis=1)
    y = jnp.zeros((B * T, D), jnp.float32)
    for e in range(N_EXPERTS):
        ye = clamped_swiglu(xf @ w1[e] + b1[e]) @ w2[e] + b2[e]
        y = y + gate[:, e:e + 1] * ye.astype(jnp.float32)
    return y.astype(h.dtype).reshape(B, T, D)


def setup_inputs(seed: int = 0) -> dict:
    key = jax.random.key(seed)
    ks = jax.random.split(key, 32)
    f32 = jnp.float32
    L = DEPTH

    def nrm(k, shape, scale):
        return jax.random.normal(k, shape, f32) * scale

    x = jax.random.normal(ks[0], (BATCH, SEQ, D_MODEL), f32)
    c = jax.random.normal(ks[1], (BATCH, D_MODEL), f32)
    offset = jax.random.randint(ks[2], (BATCH, 1), 0, SEQ, dtype=jnp.int32)
    positions = offset + jnp.arange(SEQ, dtype=jnp.int32)[None, :]
    return {
        'x': x,
        'c': c,
        'positions': positions,
        'mod_w': nrm(ks[3], (L, D_MODEL, N_MOD * D_MODEL), 0.5 * D_MODEL ** -0.5),
        'mod_b': nrm(ks[4], (L, N_MOD * D_MODEL), 0.01),
        'norm1_g': 1.0 + nrm(ks[5], (L, D_MODEL), 0.1),
        'norm2_g': 1.0 + nrm(ks[6], (L, D_MODEL), 0.1),
        'w_in': nrm(ks[7], (L, D_MODEL, IN_WIDTH), D_MODEL ** -0.5),
        'pool_w': nrm(ks[8], (L, POOL_GROUPS, POOL_CH, POOL_CH), POOL_CH ** -0.5),
        'pool_scale': 1.0 + nrm(ks[9], (L, GROUP_WIDTH), 0.1),
        'diff_lq1': nrm(ks[10], (L, DIFF_QK_DIM), 0.1),
        'diff_lk1': nrm(ks[11], (L, DIFF_QK_DIM), 0.1),
        'diff_lq2': nrm(ks[12], (L, DIFF_QK_DIM), 0.1),
        'diff_lk2': nrm(ks[13], (L, DIFF_QK_DIM), 0.1),
        'diff_subln_g': 1.0 + nrm(ks[14], (L, DIFF_V_DIM), 0.1),
        'conv_dw_w': nrm(ks[15], (L, CONV_WIDTH, CONV_CH), CONV_WIDTH ** -0.5),
        'conv_dw_b': nrm(ks[16], (L, CONV_CH), 0.01),
        'conv_ln_g': 1.0 + nrm(ks[17], (L, CONV_CH), 0.1),
        'conv_ln_b': nrm(ks[18], (L, CONV_CH), 0.01),
        'conv_pw_w': nrm(ks[19], (L, CONV_CH, CONV_CH), CONV_CH ** -0.5),
        'conv_pw_b': nrm(ks[20], (L, CONV_CH), 0.01),
        'w_out': nrm(ks[21], (L, MIX_WIDTH, D_MODEL), MIX_WIDTH ** -0.5),
        'router_w': nrm(ks[22], (L, D_MODEL, N_EXPERTS), D_MODEL ** -0.5),
        'router_b': nrm(ks[23], (L, N_EXPERTS), 0.01),
        'exp_w1': nrm(ks[24], (L, N_EXPERTS, D_MODEL, 2 * EXPERT_HIDDEN), D_MODEL ** -0.5),
        'exp_b1': nrm(ks[25], (L, N_EXPERTS, 2 * EXPERT_HIDDEN), 0.01),
        'exp_w2': nrm(ks[26], (L, N_EXPERTS, EXPERT_HIDDEN, D_MODEL), EXPERT_HIDDEN ** -0.5),
        'exp_b2': nrm(ks[27], (L, N_EXPERTS, D_MODEL), 0.01),
        'final_g': 1.0 + nrm(ks[28], (D_MODEL,), 0.1),
    }


def reference(x, c, positions, mod_w, mod_b, norm1_g, norm2_g, w_in, pool_w, pool_scale,
              diff_lq1, diff_lk1, diff_lq2, diff_lk2, diff_subln_g, conv_dw_w, conv_dw_b,
              conv_ln_g, conv_ln_b, conv_pw_w, conv_pw_b, w_out, router_w, router_b,
              exp_w1, exp_b1, exp_w2, exp_b2, final_g):
    c_act = jax.nn.silu(c)
    for l in range(DEPTH):
        lambda_init = 0.8 - 0.6 * math.exp(-0.3 * l)
        mod = c_act @ mod_w[l] + mod_b[l]
        sh1, sc1, g1, sh2, sc2, g2 = [m[:, None, :] for m in jnp.split(mod, N_MOD, axis=-1)]
        h = rms_norm(x, norm1_g[l]) * (1.0 + sc1) + sh1
        x = x + g1 * token_mixing(h, positions, lambda_init, w_in[l], pool_w[l], pool_scale[l],
                                  diff_lq1[l], diff_lk1[l], diff_lq2[l], diff_lk2[l], diff_subln_g[l],
                                  conv_dw_w[l], conv_dw_b[l], conv_ln_g[l], conv_ln_b[l],
                                  conv_pw_w[l], conv_pw_b[l], w_out[l])
        h = rms_norm(x, norm2_g[l]) * (1.0 + sc2) + sh2
        x = x + g2 * moe(h, router_w[l], router_b[l], exp_w1[l], exp_b1[l], exp_w2[l], exp_b2[l])
    return rms_norm(x, final_g)
```

```python
import functools
import math

import numpy as np
import jax
import jax.numpy as jnp
from jax import lax
from jax.experimental import pallas as pl
from jax.experimental.pallas import tpu as pltpu

F32 = jnp.float32
BF16 = jnp.bfloat16
I32 = jnp.int32
U32 = jnp.uint32
HIGHEST = lax.Precision.HIGHEST

LANES = 128
VMEM_LIMIT_BYTES = 56 * 1024 * 1024

POOL_WINDOWS = (2, 4, 8, 16)
DILATED_PAIRS = ((128, 1), (512, 4), (2048, 16))
DIL_BAND = 128
CONV_WIDTH = 31
HALO = 32
N_EXPERTS = 32
TOP_K = 4
SWIGLU_ALPHA = 1.702
SWIGLU_LIMIT = 7.0
ROPE_THETA = 500000.0
RMS_EPS = 1e-6
LN_EPS = 1e-5
NEG_INF = -1e30
GROUP = 512
HEAD = 128
DIFF_QK = 64


def _params(*semantics):
    return pltpu.CompilerParams(dimension_semantics=semantics, vmem_limit_bytes=VMEM_LIMIT_BYTES)


def _sigmoid(x):
    return 1.0 / (1.0 + jnp.exp(-x))


def _mod_kernel(c_ref, w_ref, b_ref, o_ref):
    c = c_ref[...]
    ca = c * _sigmoid(c)
    o_ref[0] = jnp.dot(ca, w_ref[0], precision=HIGHEST, preferred_element_type=F32) + b_ref[0]


def _modulation(c, mod_w, mod_b):
    depth, d, n = mod_w.shape
    b = c.shape[0]
    rows = 8
    c_pad = jnp.zeros((rows, d), F32).at[:b].set(c)
    tn = 512
    out = pl.pallas_call(
        _mod_kernel,
        grid=(depth, n // tn),
        in_specs=[
            pl.BlockSpec((rows, d), lambda l, j: (0, 0)),
            pl.BlockSpec((1, d, tn), lambda l, j: (l, 0, j)),
            pl.BlockSpec((1, 1, tn), lambda l, j: (l, 0, j)),
        ],
        out_specs=pl.BlockSpec((1, rows, tn), lambda l, j: (l, 0, j)),
        out_shape=jax.ShapeDtypeStruct((depth, rows, n), F32),
        compiler_params=_params("arbitrary", "arbitrary"),
        name="modulation",
    )(c_pad, mod_w, mod_b.reshape(depth, 1, n))
    return out[:, :b]


def _rope_table_kernel(pos_ref, invd_ref, invc_ref, cd_ref, sd_ref, cc_ref, sc_ref):
    p = pos_ref[0]
    lane = lax.broadcasted_iota(I32, (1, LANES), 1)
    for inv_ref, c_ref, s_ref, hd in ((invd_ref, cd_ref, sd_ref, DIFF_QK), (invc_ref, cc_ref, sc_ref, HEAD)):
        half = hd // 8
        lm = lane % hd
        ang = p * inv_ref[...]
        c_ref[0] = jnp.cos(ang)
        s_ref[0] = jnp.where(lm < half, -jnp.sin(ang), jnp.sin(ang))


def _lane_inv_freq(hd):
    half = hd // 8
    inv = ROPE_THETA ** (-jnp.arange(half, dtype=F32) / half)
    lm = np.arange(LANES) % hd
    rotated = jnp.asarray(lm < 2 * half)
    return jnp.where(rotated, inv[lm % half], 0.0).reshape(1, LANES).astype(F32)


def _rope_tables(positions):
    b, t = positions.shape
    tm = min(t, 1024)
    pos = positions.astype(F32).reshape(b, t, 1)
    spec_t = pl.BlockSpec((1, tm, LANES), lambda bi, i: (bi, i, 0))
    spec_inv = pl.BlockSpec((1, LANES), lambda bi, i: (0, 0))
    shp = jax.ShapeDtypeStruct((b, t, LANES), F32)
    return pl.pallas_call(
        _rope_table_kernel,
        grid=(b, t // tm),
        in_specs=[pl.BlockSpec((1, tm, 1), lambda bi, i: (bi, i, 0)), spec_inv, spec_inv],
        out_specs=[spec_t] * 4,
        out_shape=[shp] * 4,
        compiler_params=_params("arbitrary", "arbitrary"),
        name="rope_tables",
    )(pos, _lane_inv_freq(DIFF_QK), _lane_inv_freq(HEAD))


def _rope_apply(z, cos, sin, hd):
    half = hd // 8
    lane = lax.broadcasted_iota(I32, (1, LANES), 1)
    first = (lane % hd) < half
    outs = []
    for cb in range(z.shape[1] // LANES):
        zc = z[:, cb * LANES:(cb + 1) * LANES]
        partner = jnp.where(first, pltpu.roll(zc, LANES - half, 1), pltpu.roll(zc, half, 1))
        outs.append(zc * cos + partner * sin)
    return jnp.concatenate(outs, axis=1)


def _inproj_kernel(x_ref, g_ref, sc_ref, sh_ref, w_ref, cd_ref, sd_ref, cc_ref, sc2_ref, z_ref, h_scr):
    j = pl.program_id(2)

    @pl.when(j == 0)
    def _():
        x = x_ref[0]
        y = x * lax.rsqrt(jnp.mean(x * x, axis=-1, keepdims=True) + RMS_EPS) * g_ref[0]
        h_scr[...] = (y * (1.0 + sc_ref[0]) + sh_ref[0]).astype(BF16)

    z = jnp.dot(h_scr[...], w_ref[...], preferred_element_type=F32)

    @pl.when((j == 1) | (j == 2))
    def _():
        r = _rope_apply(z, cd_ref[0], sd_ref[0], DIFF_QK)
        r = r * jnp.where(j == 1, DIFF_QK ** -0.5, 1.0)
        z_ref[0] = r.astype(z_ref.dtype)

    @pl.when((j == 4) | (j == 5))
    def _():
        r = _rope_apply(z, cc_ref[0], sc2_ref[0], HEAD)
        r = r * jnp.where(j == 4, HEAD ** -0.5, 1.0)
        z_ref[0] = r.astype(z_ref.dtype)

    @pl.when((j != 1) & (j != 2) & (j != 4) & (j != 5))
    def _():
        z_ref[0] = z.astype(z_ref.dtype)


def _inproj(x, norm_g, scale, shift, w_in_bf16, tables, layer):
    b, t, d = x.shape
    n = w_in_bf16.shape[-1]
    tm = min(t, 1024)
    tn = GROUP
    cd, sd, cc, sc = tables
    spec_tab = pl.BlockSpec((1, tm, LANES), lambda bi, i, j: (bi, i, 0))
    spec_vec = pl.BlockSpec((1, 1, d), lambda bi, i, j: (bi, 0, 0))
    return pl.pallas_call(
        _inproj_kernel,
        grid=(b, t // tm, n // tn),
        in_specs=[
            pl.BlockSpec((1, tm, d), lambda bi, i, j: (bi, i, 0)),
            pl.BlockSpec((1, 1, d), lambda bi, i, j: (layer, 0, 0)),
            spec_vec, spec_vec,
            pl.BlockSpec((None, d, tn), lambda bi, i, j: (layer, 0, j)),
            spec_tab, spec_tab, spec_tab, spec_tab,
        ],
        out_specs=pl.BlockSpec((1, tm, tn), lambda bi, i, j: (bi, i, j)),
        out_shape=jax.ShapeDtypeStruct((b, t, n), BF16),
        scratch_shapes=[pltpu.VMEM((tm, d), BF16)],
        compiler_params=_params("arbitrary", "arbitrary", "arbitrary"),
        name="inproj",
    )(x, norm_g, scale, shift, w_in_bf16, cd, sd, cc, sc)


def _poolconv_kernel(zp_ref, zph_ref, za_ref, zah_ref, zg_ref, zgh_ref,
                     pw_ref, ps_ref, dww_ref, dwb_ref, lng_ref, lnb_ref, pww_ref, pwb_ref,
                     yp_ref, yc_ref, xp_scr, u_scr, *, tm):
    i = pl.program_id(1)
    keep = jnp.where(i == 0, 0.0, 1.0)

    xp_scr[0:HALO, :] = zph_ref[0].astype(F32) * keep
    xp_scr[HALO:, :] = zp_ref[0].astype(F32)
    t_glob = i * tm + lax.broadcasted_iota(I32, (tm, 1), 0)
    for g, w in enumerate(POOL_WINDOWS):
        cols = slice(g * LANES, (g + 1) * LANES)
        xg = xp_scr[HALO:HALO + tm, cols]
        acc = xg
        for k in range(1, w):
            acc = acc + xp_scr[HALO - k:HALO - k + tm, cols]
        cnt = jnp.minimum(t_glob + 1, w).astype(F32)
        pooled = acc / cnt - xg
        yg = jnp.dot(pooled.astype(BF16), pw_ref[0, g], preferred_element_type=F32)
        yp_ref[0, :, cols] = (yg * ps_ref[0, :, cols]).astype(yp_ref.dtype)

    ah = zah_ref[0].astype(F32)
    gh = zgh_ref[0].astype(F32)
    u_scr[0:HALO, :] = ah * _sigmoid(gh) * keep
    a = za_ref[0].astype(F32)
    gg = zg_ref[0].astype(F32)
    u_scr[HALO:, :] = a * _sigmoid(gg)
    acc = jnp.zeros((tm, GROUP), F32) + dwb_ref[0]
    base = HALO - (CONV_WIDTH - 1)
    for k in range(CONV_WIDTH):
        acc = acc + u_scr[base + k:base + k + tm, :] * dww_ref[0, k:k + 1, :]
    mu = jnp.mean(acc, axis=-1, keepdims=True)
    cen = acc - mu
    var = jnp.mean(cen * cen, axis=-1, keepdims=True)
    v = cen * lax.rsqrt(var + LN_EPS) * lng_ref[0] + lnb_ref[0]
    v = v * _sigmoid(v)
    y = jnp.dot(v.astype(BF16), pww_ref[0], preferred_element_type=F32) + pwb_ref[0]
    yc_ref[0] = y.astype(yc_ref.dtype)


def _poolconv(z, pool_w_bf16, pool_scale, dw_w, dw_b, ln_g, ln_b, pw_w_bf16, pw_b, layer):
    b, t, _ = z.shape
    tm = min(t, 512)
    r = tm // HALO
    a_blk = 7
    g_blk = 8

    def cur(col):
        return pl.BlockSpec((1, tm, GROUP), lambda bi, i: (bi, i, col))

    def halo(col):
        return pl.BlockSpec((1, HALO, GROUP), lambda bi, i: (bi, jnp.maximum(i * r - 1, 0), col))

    def vec(n):
        return pl.BlockSpec((1, 1, n), lambda bi, i: (layer, 0, 0))

    out_spec = pl.BlockSpec((1, tm, GROUP), lambda bi, i: (bi, i, 0))
    shp = jax.ShapeDtypeStruct((b, t, GROUP), BF16)
    return pl.pallas_call(
        functools.partial(_poolconv_kernel, tm=tm),
        grid=(b, t // tm),
        in_specs=[
            cur(0), halo(0), cur(a_blk), halo(a_blk), cur(g_blk), halo(g_blk),
            pl.BlockSpec((1, len(POOL_WINDOWS), LANES, LANES), lambda bi, i: (layer, 0, 0, 0)),
            vec(GROUP),
            pl.BlockSpec((1, CONV_WIDTH, GROUP), lambda bi, i: (layer, 0, 0)),
            vec(GROUP), vec(GROUP), vec(GROUP),
            pl.BlockSpec((1, GROUP, GROUP), lambda bi, i: (layer, 0, 0)),
            vec(GROUP),
        ],
        out_specs=[out_spec, out_spec],
        out_shape=[shp, shp],
        scratch_shapes=[pltpu.VMEM((tm + HALO, GROUP), F32), pltpu.VMEM((tm + HALO, GROUP), F32)],
        compiler_params=_params("arbitrary", "arbitrary"),
        name="pool_conv",
    )(z, z, z, z, z, z, pool_w_bf16, pool_scale, dw_w, dw_b, ln_g, ln_b, pw_w_bf16, pw_b)


def _diff_kernel(q_ref, k_ref, v_ref, lq1_ref, lk1_ref, lq2_ref, lk2_ref, g_ref, o_ref, *, tq, lambda_init):
    qi = pl.program_id(2)
    q = q_ref[0]
    lane = lax.broadcasted_iota(I32, (tq, HEAD), 1)
    zero = jnp.zeros_like(q)
    q2 = jnp.concatenate([jnp.where(lane < DIFF_QK, q, zero), jnp.where(lane >= DIFF_QK, q, zero)], axis=0)

    def step(j, carry, masked):
        m, l, acc = carry
        kb = k_ref[0, pl.ds(pl.multiple_of(j * tq, tq), tq), :]
        vb = v_ref[0, pl.ds(pl.multiple_of(j * tq, tq), tq), :]
        s = lax.dot_general(q2, kb, (((1,), (1,)), ((), ())), preferred_element_type=F32)
        if masked:
            row = lax.broadcasted_iota(I32, (2 * tq, tq), 0) % tq
            col = lax.broadcasted_iota(I32, (2 * tq, tq), 1)
            s = jnp.where(col <= row, s, NEG_INF)
        m_new = jnp.maximum(m, jnp.max(s, axis=-1, keepdims=True))
        alpha = jnp.exp(m - m_new)
        p = jnp.exp(s - m_new)
        l = alpha * l + jnp.sum(p, axis=-1, keepdims=True)
        acc = alpha * acc + jnp.dot(p.astype(BF16), vb, preferred_element_type=F32)
        return m_new, l, acc

    init = (jnp.full((2 * tq, 1), NEG_INF, F32), jnp.zeros((2 * tq, 1), F32), jnp.zeros((2 * tq, HEAD), F32))
    carry = lax.fori_loop(0, qi, lambda j, c: step(j, c, False), init)
    m, l, acc = step(qi, carry, True)

    lam = (jnp.exp(jnp.sum(lq1_ref[0] * lk1_ref[0], axis=-1, keepdims=True))
           - jnp.exp(jnp.sum(lq2_ref[0] * lk2_ref[0], axis=-1, keepdims=True)) + lambda_init)
    o = acc / l
    od = o[:tq] - lam * o[tq:]
    y = od * lax.rsqrt(jnp.mean(od * od, axis=-1, keepdims=True) + RMS_EPS) * g_ref[0]
    o_ref[0] = (y * (1.0 - lambda_init)).astype(o_ref.dtype)


def _diff_attention(z, lq1, lk1, lq2, lk2, subln_g, layer, lambda_init):
    b, t, _ = z.shape
    heads = GROUP // HEAD
    tq = min(t, 256)
    q0, k0, v0 = 1 * heads, 2 * heads, 3 * heads

    def vec(n):
        return pl.BlockSpec((1, 1, n), lambda bi, h, i: (layer, 0, 0))

    return pl.pallas_call(
        functools.partial(_diff_kernel, tq=tq, lambda_init=lambda_init),
        grid=(b, heads, t // tq),
        in_specs=[
            pl.BlockSpec((1, tq, HEAD), lambda bi, h, i: (bi, i, q0 + h)),
            pl.BlockSpec((1, t, HEAD), lambda bi, h, i: (bi, 0, k0 + h)),
            pl.BlockSpec((1, t, HEAD), lambda bi, h, i: (bi, 0, v0 + h)),
            vec(DIFF_QK), vec(DIFF_QK), vec(DIFF_QK), vec(DIFF_QK), vec(HEAD),
        ],
        out_specs=pl.BlockSpec((1, tq, HEAD), lambda bi, h, i: (bi, i, h)),
        out_shape=jax.ShapeDtypeStruct((b, t, GROUP), BF16),
        compiler_params=_params("arbitrary", "arbitrary", "arbitrary"),
        name="diff_attention",
    )(z, z, z, lq1, lk1, lq2, lk2, subln_g)


def _dilated_kernel(q_ref, k_ref, v_ref, o_ref, qf, kf, vf, m_scr, l_scr, acc_scr, *, t):
    qf[...] = q_ref[0].astype(F32)
    kf[...] = k_ref[0].astype(F32)
    vf[...] = v_ref[0].astype(F32)
    band = DIL_BAND
    a_idx = lax.broadcasted_iota(I32, (band, 2 * band), 0)
    b_idx = lax.broadcasted_iota(I32, (band, 2 * band), 1)
    in_band = (b_idx >= a_idx) & (b_idx <= a_idx + band)

    for pi, (window, dil) in enumerate(DILATED_PAIRS):
        nb = t // (dil * band)

        def block(it, carry, dil=dil, nb=nb, first=(pi == 0)):
            r = it // nb
            n = it % nb
            start = r + n * (band * dil)
            prev = r + jnp.maximum(n - 1, 0) * (band * dil)

            def rows(ref, s0):
                return ref[pl.ds(s0, band, stride=dil), :] if dil > 1 else ref[pl.ds(s0, band), :]

            qb = rows(qf, start).astype(BF16)
            kk = jnp.concatenate([rows(kf, prev), rows(kf, start)], axis=0).astype(BF16)
            vv = jnp.concatenate([rows(vf, prev), rows(vf, start)], axis=0).astype(BF16)
            s = lax.dot_general(qb, kk, (((1,), (1,)), ((), ())), preferred_element_type=F32)
            valid = in_band & (b_idx >= jnp.where(n > 0, 0, band))
            s = jnp.where(valid, s, NEG_INF)
            s_max = jnp.max(s, axis=-1, keepdims=True)
            if first:
                m_new = s_max
                p = jnp.exp(s - m_new)
                l_new = jnp.sum(p, axis=-1, keepdims=True)
                acc_new = jnp.dot(p.astype(BF16), vv, preferred_element_type=F32)
            else:
                m_old = rows(m_scr, start)[:, :1]
                m_new = jnp.maximum(m_old, s_max)
                alpha = jnp.exp(m_old - m_new)
                p = jnp.exp(s - m_new)
                l_new = alpha * rows(l_scr, start)[:, :1] + jnp.sum(p, axis=-1, keepdims=True)
                acc_new = alpha * rows(acc_scr, start) + jnp.dot(p.astype(BF16), vv, preferred_element_type=F32)
            idx = pl.ds(start, band, stride=dil) if dil > 1 else pl.ds(start, band)
            m_scr[idx, :] = jnp.broadcast_to(m_new, (band, HEAD))
            l_scr[idx, :] = jnp.broadcast_to(l_new, (band, HEAD))
            acc_scr[idx, :] = acc_new
            return carry

        lax.fori_loop(0, dil * nb, block, 0)

    o_ref[0] = (acc_scr[...] / l_scr[...]).astype(o_ref.dtype)


def _dilated_attention(z):
    b, t, _ = z.shape
    heads = GROUP // HEAD
    q0, k0, v0 = 4 * heads, 5 * heads, 6 * heads
    assert all(w // dl == DIL_BAND and t % w == 0 for w, dl in DILATED_PAIRS)

    def col(c0):
        return pl.BlockSpec((1, t, HEAD), lambda bi, h: (bi, 0, c0 + h))

    return pl.pallas_call(
        functools.partial(_dilated_kernel, t=t),
        grid=(b, heads),
        in_specs=[col(q0), col(k0), col(v0)],
        out_specs=pl.BlockSpec((1, t, HEAD), lambda bi, h: (bi, 0, h)),
        out_shape=jax.ShapeDtypeStruct((b, t, GROUP), BF16),
        scratch_shapes=[pltpu.VMEM((t, HEAD), F32)] * 6,
        compiler_params=_params("arbitrary", "arbitrary"),
        name="dilated_attention",
    )(z, z, z)


def _pack_rows(h, out_ref, tm):
    half = h.shape[1] // 2
    bits = pltpu.bitcast(h.astype(BF16).astype(F32), U32)
    word = (bits[:, half:] & jnp.uint32(0xFFFF0000)) | (bits[:, :half] >> 16)
    n_slab = half // LANES
    for s in range(n_slab):
        out_ref[pl.ds(s, tm, stride=n_slab), :] = word[:, s * LANES:(s + 1) * LANES]


def _outproj_router_kernel(yp_ref, yd_ref, yc_ref, yv_ref, wo_ref, x_ref, g1_ref, n2_ref, sc_ref, sh_ref,
                           rw_ref, rb_ref,
                           x1_ref, h2_ref, idx_ref, rank_ref, wts_ref, cnt_ref, carry_scr, *, tm):
    first = (pl.program_id(0) == 0) & (pl.program_id(1) == 0)

    @pl.when(first)
    def _():
        carry_scr[...] = jnp.zeros_like(carry_scr)

    mix = jnp.dot(yp_ref[0], wo_ref[0 * GROUP:1 * GROUP, :], preferred_element_type=F32)
    mix = mix + jnp.dot(yd_ref[0], wo_ref[1 * GROUP:2 * GROUP, :], preferred_element_type=F32)
    mix = mix + jnp.dot(yc_ref[0], wo_ref[2 * GROUP:3 * GROUP, :], preferred_element_type=F32)
    mix = mix + jnp.dot(yv_ref[0], wo_ref[3 * GROUP:4 * GROUP, :], preferred_element_type=F32)
    x1 = x_ref[0] + g1_ref[0] * mix
    x1_ref[0] = x1

    y = x1 * lax.rsqrt(jnp.mean(x1 * x1, axis=-1, keepdims=True) + RMS_EPS) * n2_ref[0]
    h2 = y * (1.0 + sc_ref[0]) + sh_ref[0]
    _pack_rows(h2, h2_ref, tm)

    logits = lax.dot_general(rw_ref[0], h2, (((1,), (1,)), ((), ())), precision=HIGHEST,
                             preferred_element_type=F32) + rb_ref[0]
    e_idx = lax.broadcasted_iota(I32, (N_EXPERTS, tm), 0)
    work = logits
    vals, sels, hots = [], [], []
    for _ in range(TOP_K):
        mx = jnp.max(work, axis=0, keepdims=True)
        sel = jnp.min(jnp.where(work == mx, e_idx, N_EXPERTS), axis=0, keepdims=True)
        hot = e_idx == sel
        vals.append(mx)
        sels.append(sel)
        hots.append(hot)
        work = jnp.where(hot, -jnp.inf, work)
    exps = [jnp.exp(v - vals[0]) for v in vals]
    denom = exps[0] + exps[1] + exps[2] + exps[3]

    chosen = jnp.zeros((N_EXPERTS, tm), F32)
    for hot in hots:
        chosen = chosen + hot.astype(F32)
    s_idx = lax.broadcasted_iota(I32, (tm, tm), 0)
    t_idx = lax.broadcasted_iota(I32, (tm, tm), 1)
    upper = (s_idx < t_idx).astype(BF16)
    before = jnp.dot(chosen.astype(BF16), upper, preferred_element_type=F32) + carry_scr[:, 0:1]
    for k in range(TOP_K):
        idx_ref[k:k + 1, :] = sels[k]
        rank_ref[k:k + 1, :] = jnp.sum(jnp.where(hots[k], before, 0.0), axis=0, keepdims=True).astype(I32)
        wts_ref[k:k + 1, :] = exps[k] / denom
    carry_scr[...] = carry_scr[...] + jnp.sum(chosen, axis=1, keepdims=True)
    cnt_ref[...] = carry_scr[...]


def _outproj_router(ys, w_out_bf16, x, gate1, norm2_g, scale2, shift2, router_wt, router_b, layer):
    b, t, d = x.shape
    tm = min(t, 512)
    n = b * t
    nt = t // tm
    slab = d // 2 // LANES

    def ytile():
        return pl.BlockSpec((1, tm, GROUP), lambda bi, i: (bi, i, 0))

    def bvec():
        return pl.BlockSpec((1, 1, d), lambda bi, i: (bi, 0, 0))

    tok = pl.BlockSpec((TOP_K, tm), lambda bi, i: (0, bi * nt + i))
    outs = pl.pallas_call(
        functools.partial(_outproj_router_kernel, tm=tm),
        grid=(b, nt),
        in_specs=[
            ytile(), ytile(), ytile(), ytile(),
            pl.BlockSpec((None, 4 * GROUP, d), lambda bi, i: (layer, 0, 0)),
            pl.BlockSpec((1, tm, d), lambda bi, i: (bi, i, 0)),
            bvec(),
            pl.BlockSpec((1, 1, d), lambda bi, i: (layer, 0, 0)),
            bvec(), bvec(),
            pl.BlockSpec((1, N_EXPERTS, d), lambda bi, i: (layer, 0, 0)),
            pl.BlockSpec((1, N_EXPERTS, 1), lambda bi, i: (layer, 0, 0)),
        ],
        out_specs=[
            pl.BlockSpec((1, tm, d), lambda bi, i: (bi, i, 0)),
            pl.BlockSpec((tm * slab, LANES), lambda bi, i: (bi * nt + i, 0)),
            tok, tok, tok,
            pl.BlockSpec((N_EXPERTS, LANES), lambda bi, i: (0, 0)),
        ],
        out_shape=[
            jax.ShapeDtypeStruct((b, t, d), F32),
            jax.ShapeDtypeStruct((n * slab, LANES), U32),
            jax.ShapeDtypeStruct((TOP_K, n), I32),
            jax.ShapeDtypeStruct((TOP_K, n), I32),
            jax.ShapeDtypeStruct((TOP_K, n), F32),
            jax.ShapeDtypeStruct((N_EXPERTS, LANES), F32),
        ],
        scratch_shapes=[pltpu.VMEM((N_EXPERTS, LANES), F32)],
        compiler_params=_params("arbitrary", "arbitrary"),
        name="outproj_router",
    )(*ys, w_out_bf16, x, gate1, norm2_g, scale2, shift2, router_wt, router_b)
    return outs


def _scatter_kernel(pos_ref, h_hbm, xs_hbm, sem, *, tm):
    base = pl.program_id(0) * tm

    def issue(tt, c):
        for k in range(TOP_K):
            pltpu.make_async_copy(h_hbm.at[base + tt], xs_hbm.at[pos_ref[k, tt]], sem).start()
        return c

    lax.fori_loop(0, tm, issue, 0)

    def drain(tt, c):
        for k in range(TOP_K):
            pltpu.make_async_copy(h_hbm.at[0], xs_hbm.at[0], sem).wait()
        return c

    lax.fori_loop(0, tm, drain, 0)


def _scatter_rows(pos, h_rows, m_pad):
    n, slab, _ = h_rows.shape
    tm = min(n, 256)
    return pl.pallas_call(
        functools.partial(_scatter_kernel, tm=tm),
        grid=(n // tm,),
        in_specs=[
            pl.BlockSpec((TOP_K, tm), lambda i: (0, i), memory_space=pltpu.SMEM),
            pl.BlockSpec(memory_space=pl.ANY),
        ],
        out_specs=pl.BlockSpec(memory_space=pl.ANY),
        out_shape=jax.ShapeDtypeStruct((m_pad, slab, LANES), U32),
        scratch_shapes=[pltpu.SemaphoreType.DMA(())],
        compiler_params=_params("arbitrary"),
        name="scatter_rows",
    )(pos, h_rows)


def _expert_kernel(te_ref, nu_ref, xs_ref, w1_ref, b1_ref, w2_ref, b2_ref, ys_ref, x_scr, acc_scr, *, tm, nc):
    i = pl.program_id(0)
    c = pl.program_id(1)
    d = x_scr.shape[1]
    n_slab = d // 2 // LANES

    @pl.when(i < nu_ref[0])
    def _():
        @pl.when(c == 0)
        def _():
            for s in range(n_slab):
                word = xs_ref[pl.ds(s, tm, stride=n_slab), :]
                lo = pltpu.bitcast(word << 16, F32)
                hi = pltpu.bitcast(word & jnp.uint32(0xFFFF0000), F32)
                x_scr[:, s * LANES:(s + 1) * LANES] = lo.astype(BF16)
                x_scr[:, d // 2 + s * LANES:d // 2 + (s + 1) * LANES] = hi.astype(BF16)
            acc_scr[...] = jnp.zeros_like(acc_scr) + b2_ref[...]

        hh = jnp.dot(x_scr[...], w1_ref[...].astype(BF16), preferred_element_type=F32) + b1_ref[...]
        width = hh.shape[1]
        nxt = pltpu.roll(hh, width - 1, 1)
        g = jnp.minimum(hh, SWIGLU_LIMIT)
        lin = jnp.clip(nxt, -SWIGLU_LIMIT, SWIGLU_LIMIT)
        act = g * _sigmoid(SWIGLU_ALPHA * g) * (lin + 1.0)
        row = lax.broadcasted_iota(I32, (width, width // 2), 0)
        colj = lax.broadcasted_iota(I32, (width, width // 2), 1)
        pick_even = (row == 2 * colj).astype(BF16)
        act = jnp.dot(act.astype(BF16), pick_even, preferred_element_type=F32)
        acc_scr[...] += jnp.dot(act.astype(BF16), w2_ref[...].astype(BF16), preferred_element_type=F32)

        @pl.when(c == nc - 1)
        def _():
            n_out = d // LANES
            for s in range(n_out):
                ys_ref[pl.ds(s, tm, stride=n_out), :] = acc_scr[:, s * LANES:(s + 1) * LANES]


def _experts(tile_expert, n_used, xs2d, w1, b1, w2, b2, layer, tm, m_pad):
    _, n_exp, d, h2 = w1.shape
    hid = h2 // 2
    tc = 256
    nc = hid // tc
    n_tiles = m_pad // tm
    slab_in = d // 2 // LANES
    slab_out = d // LANES

    def live(i, nu):
        return jnp.minimum(i, nu[0] - 1)

    def chunk(i, c, nu):
        return jnp.where(i < nu[0], c, nc - 1)

    grid_spec = pltpu.PrefetchScalarGridSpec(
        num_scalar_prefetch=2,
        grid=(n_tiles, nc),
        in_specs=[
            pl.BlockSpec((tm * slab_in, LANES), lambda i, c, te, nu: (live(i, nu), 0)),
            pl.BlockSpec((None, None, d, 2 * tc), lambda i, c, te, nu: (layer, te[i], 0, chunk(i, c, nu))),
            pl.BlockSpec((None, None, 1, 2 * tc), lambda i, c, te, nu: (layer, te[i], 0, chunk(i, c, nu))),
            pl.BlockSpec((None, None, tc, d), lambda i, c, te, nu: (layer, te[i], chunk(i, c, nu), 0)),
            pl.BlockSpec((None, None, 1, d), lambda i, c, te, nu: (layer, te[i], 0, 0)),
        ],
        out_specs=pl.BlockSpec((tm * slab_out, LANES), lambda i, c, te, nu: (live(i, nu), 0)),
        scratch_shapes=[pltpu.VMEM((tm, d), BF16), pltpu.VMEM((tm, d), F32)],
    )
    return pl.pallas_call(
        functools.partial(_expert_kernel, tm=tm, nc=nc),
        grid_spec=grid_spec,
        out_shape=jax.ShapeDtypeStruct((m_pad * slab_out, LANES), F32),
        compiler_params=_params("arbitrary", "arbitrary"),
        name="experts",
    )(tile_expert, n_used, xs2d, w1, b1, w2, b2)


def _combine_kernel(pos_ref, wts_ref, x_ref, g2_ref, fg_ref, ys_hbm, o_ref, buf, sem, *, tm, final):
    n_out = x_ref.shape[1] // LANES

    def issue(tt, c):
        for k in range(TOP_K):
            dst = buf.at[pl.ds(pl.multiple_of((k * tm + tt) * n_out, n_out), n_out)]
            pltpu.make_async_copy(ys_hbm.at[pos_ref[k, tt]], dst, sem).start()
        return c

    lax.fori_loop(0, tm, issue, 0)

    def drain(tt, c):
        for k in range(TOP_K):
            pltpu.make_async_copy(ys_hbm.at[0], buf.at[pl.ds(0, n_out)], sem).wait()
        return c

    lax.fori_loop(0, tm, drain, 0)

    w_sq = jnp.concatenate([wts_ref[...], jnp.zeros((tm - TOP_K, tm), F32)], axis=0)
    w_t = w_sq.T
    pieces = []
    for s in range(n_out):
        acc = jnp.zeros((tm, LANES), F32)
        for k in range(TOP_K):
            rows = buf[pl.ds(k * tm * n_out + s, tm, stride=n_out), :]
            acc = acc + w_t[:, k:k + 1] * rows
        pieces.append(acc)
    moe = jnp.concatenate(pieces, axis=1)
    x2 = x_ref[...] + g2_ref[0] * moe
    if final:
        x2 = x2 * lax.rsqrt(jnp.mean(x2 * x2, axis=-1, keepdims=True) + RMS_EPS) * fg_ref[...]
    o_ref[...] = x2


def _combine(pos, wts, x1, gate2, final_g, ys_rows, tokens_per_batch, final):
    n, d = x1.shape
    tm = LANES
    steps_per_batch = tokens_per_batch // tm
    return pl.pallas_call(
        functools.partial(_combine_kernel, tm=tm, final=final),
        grid=(n // tm,),
        in_specs=[
            pl.BlockSpec((TOP_K, tm), lambda i: (0, i), memory_space=pltpu.SMEM),
            pl.BlockSpec((TOP_K, tm), lambda i: (0, i)),
            pl.BlockSpec((tm, d), lambda i: (i, 0)),
            pl.BlockSpec((1, 1, d), lambda i: (i // steps_per_batch, 0, 0)),
            pl.BlockSpec((1, d), lambda i: (0, 0)),
            pl.BlockSpec(memory_space=pl.ANY),
        ],
        out_specs=pl.BlockSpec((tm, d), lambda i: (i, 0)),
        out_shape=jax.ShapeDtypeStruct((n, d), F32),
        scratch_shapes=[pltpu.VMEM((TOP_K * tm * (d // LANES), LANES), F32), pltpu.SemaphoreType.DMA(())],
        compiler_params=_params("arbitrary"),
        name="combine",
    )(pos, wts, x1, gate2, final_g, ys_rows)


def _routing_plan(idx, rank, counts, tm_e, n_tiles):
    cnt = counts[:, 0].astype(I32)
    tiles = (cnt + tm_e - 1) // tm_e
    tile_end = jnp.cumsum(tiles)
    offsets = (tile_end - tiles) * tm_e
    pos = offsets[idx] + rank
    n_used = tile_end[-1]
    tile_ids = jnp.arange(n_tiles, dtype=I32)
    te = jnp.sum((tile_ids[:, None] >= tile_end[None, :]).astype(I32), axis=1)
    te_last = jnp.sum((n_used - 1 >= tile_end).astype(I32))
    te = jnp.where(tile_ids < n_used, te, te_last).astype(I32)
    return pos.astype(I32), te, n_used.reshape(1).astype(I32)


def kernel(x, c, positions, mod_w, mod_b, norm1_g, norm2_g, w_in, pool_w, pool_scale, diff_lq1, diff_lk1,
           diff_lq2, diff_lk2, diff_subln_g, conv_dw_w, conv_dw_b, conv_ln_g, conv_ln_b, conv_pw_w, conv_pw_b,
           w_out, router_w, router_b, exp_w1, exp_b1, exp_w2, exp_b2, final_g):
    b, t, d = x.shape
    depth = mod_w.shape[0]
    n = b * t
    tm_e = 512
    m_pad = n * TOP_K + N_EXPERTS * tm_e
    n_tiles = m_pad // tm_e

    def row3(a):
        return a.reshape(a.shape[0], 1, a.shape[1])

    mod = _modulation(c, mod_w, mod_b)
    tables = _rope_tables(positions)
    w_in_b = w_in.astype(BF16)
    w_out_b = w_out.astype(BF16)
    pool_w_b = pool_w.astype(BF16)
    pw_w_b = conv_pw_w.astype(BF16)
    router_wt = jnp.swapaxes(router_w, 1, 2)
    router_b3 = router_b.reshape(depth, N_EXPERTS, 1)
    b1r = exp_b1.reshape(depth, N_EXPERTS, 1, exp_b1.shape[-1])
    b2r = exp_b2.reshape(depth, N_EXPERTS, 1, d)
    final_g2 = final_g.reshape(1, d)

    for l in range(depth):
        lambda_init = 0.8 - 0.6 * math.exp(-0.3 * l)
        sh1, sc1, g1, sh2, sc2, g2 = [m.reshape(b, 1, d) for m in jnp.split(mod[l], 6, axis=-1)]
        z = _inproj(x, row3(norm1_g), sc1, sh1, w_in_b, tables, l)
        y_pool, y_conv = _poolconv(z, pool_w_b, row3(pool_scale), conv_dw_w, row3(conv_dw_b), row3(conv_ln_g),
                                   row3(conv_ln_b), pw_w_b, row3(conv_pw_b), l)
        y_diff = _diff_attention(z, row3(diff_lq1), row3(diff_lk1), row3(diff_lq2), row3(diff_lk2),
                                 row3(diff_subln_g), l, lambda_init)
        y_dil = _dilated_attention(z)
        x1, h2, idx, rank, wts, counts = _outproj_router(
            (y_pool, y_diff, y_dil, y_conv), w_out_b, x, g1, row3(norm2_g), sc2, sh2, router_wt, router_b3, l)
        pos, tile_expert, n_used = _routing_plan(idx, rank, counts, tm_e, n_tiles)
        xs = _scatter_rows(pos, h2.reshape(n, d // 2 // LANES, LANES), m_pad)
        ys = _experts(tile_expert, n_used, xs.reshape(m_pad * (d // 2 // LANES), LANES), exp_w1, b1r, exp_w2, b2r,
                      l, tm_e, m_pad)
        x = _combine(pos, wts, x1.reshape(n, d), g2, final_g2, ys.reshape(m_pad, d // LANES, LANES), t,
                     final=(l == depth - 1)).reshape(b, t, d)
    return x
```

```python
import functools
import math

import numpy as np
import jax
import jax.numpy as jnp
from jax import lax
from jax.experimental import pallas as pl
from jax.experimental.pallas import tpu as pltpu

F32 = jnp.float32
BF16 = jnp.bfloat16
I32 = jnp.int32
U32 = jnp.uint32
HIGHEST = lax.Precision.HIGHEST

LANES = 128
VMEM_LIMIT_BYTES = 56 * 1024 * 1024

POOL_WINDOWS = (2, 4, 8, 16)
DILATED_PAIRS = ((128, 1), (512, 4), (2048, 16))
DIL_BAND = 128
CONV_WIDTH = 31
HALO = 32
N_EXPERTS = 32
TOP_K = 4
SWIGLU_ALPHA = 1.702
SWIGLU_LIMIT = 7.0
ROPE_THETA = 500000.0
RMS_EPS = 1e-6
LN_EPS = 1e-5
NEG_INF = -1e30
GROUP = 512
HEAD = 128
DIFF_QK = 64
LOG2E = math.log2(math.e)


def _params(*semantics):
    return pltpu.CompilerParams(dimension_semantics=semantics, vmem_limit_bytes=VMEM_LIMIT_BYTES)


def _sigmoid(x):
    return 1.0 / (1.0 + jnp.exp(-x))


def _mod_kernel(c_ref, w_ref, b_ref, o_ref):
    c = c_ref[...]
    ca = c * _sigmoid(c)
    o_ref[0] = jnp.dot(ca, w_ref[0], precision=HIGHEST, preferred_element_type=F32) + b_ref[0]


def _modulation(c, mod_w, mod_b):
    depth, d, n = mod_w.shape
    b = c.shape[0]
    rows = 8
    c_pad = jnp.zeros((rows, d), F32).at[:b].set(c)
    tn = 512
    out = pl.pallas_call(
        _mod_kernel,
        grid=(depth, n // tn),
        in_specs=[
            pl.BlockSpec((rows, d), lambda l, j: (0, 0)),
            pl.BlockSpec((1, d, tn), lambda l, j: (l, 0, j)),
            pl.BlockSpec((1, 1, tn), lambda l, j: (l, 0, j)),
        ],
        out_specs=pl.BlockSpec((1, rows, tn), lambda l, j: (l, 0, j)),
        out_shape=jax.ShapeDtypeStruct((depth, rows, n), F32),
        compiler_params=_params("arbitrary", "arbitrary"),
        name="modulation",
    )(c_pad, mod_w, mod_b.reshape(depth, 1, n))
    return out[:, :b]


def _rope_table_kernel(pos_ref, invd_ref, invc_ref, cd_ref, sd_ref, cc_ref, sc_ref):
    p = pos_ref[0]
    lane = lax.broadcasted_iota(I32, (1, LANES), 1)
    for inv_ref, c_ref, s_ref, hd in ((invd_ref, cd_ref, sd_ref, DIFF_QK), (invc_ref, cc_ref, sc_ref, HEAD)):
        half = hd // 8
        lm = lane % hd
        ang = p * inv_ref[...]
        c_ref[0] = jnp.cos(ang)
        s_ref[0] = jnp.where(lm < half, -jnp.sin(ang), jnp.sin(ang))


def _lane_inv_freq(hd):
    half = hd // 8
    inv = ROPE_THETA ** (-jnp.arange(half, dtype=F32) / half)
    lm = np.arange(LANES) % hd
    rotated = jnp.asarray(lm < 2 * half)
    return jnp.where(rotated, inv[lm % half], 0.0).reshape(1, LANES).astype(F32)


def _rope_tables(positions):
    b, t = positions.shape
    tm = min(t, 1024)
    pos = positions.astype(F32).reshape(b, t, 1)
    spec_t = pl.BlockSpec((1, tm, LANES), lambda bi, i: (bi, i, 0))
    spec_inv = pl.BlockSpec((1, LANES), lambda bi, i: (0, 0))
    shp = jax.ShapeDtypeStruct((b, t, LANES), F32)
    return pl.pallas_call(
        _rope_table_kernel,
        grid=(b, t // tm),
        in_specs=[pl.BlockSpec((1, tm, 1), lambda bi, i: (bi, i, 0)), spec_inv, spec_inv],
        out_specs=[spec_t] * 4,
        out_shape=[shp] * 4,
        compiler_params=_params("arbitrary", "arbitrary"),
        name="rope_tables",
    )(pos, _lane_inv_freq(DIFF_QK), _lane_inv_freq(HEAD))


def _rope_apply(z, cos, sin, hd):
    half = hd // 8
    lane = lax.broadcasted_iota(I32, (1, LANES), 1)
    first = (lane % hd) < half
    outs = []
    for cb in range(z.shape[1] // LANES):
        zc = z[:, cb * LANES:(cb + 1) * LANES]
        partner = jnp.where(first, pltpu.roll(zc, LANES - half, 1), pltpu.roll(zc, half, 1))
        outs.append(zc * cos + partner * sin)
    return jnp.concatenate(outs, axis=1)


def _inproj_kernel(x_ref, g_ref, sc_ref, sh_ref, w_ref, cd_ref, sd_ref, cc_ref, sc2_ref, z_ref, h_scr):
    j = pl.program_id(2)

    @pl.when(j == 0)
    def _():
        x = x_ref[0]
        y = x * lax.rsqrt(jnp.mean(x * x, axis=-1, keepdims=True) + RMS_EPS) * g_ref[0]
        h_scr[...] = (y * (1.0 + sc_ref[0]) + sh_ref[0]).astype(BF16)

    z = jnp.dot(h_scr[...], w_ref[...], preferred_element_type=F32)

    @pl.when((j == 1) | (j == 2))
    def _():
        r = _rope_apply(z, cd_ref[0], sd_ref[0], DIFF_QK)
        r = r * jnp.where(j == 1, DIFF_QK ** -0.5 * LOG2E, 1.0)
        z_ref[0] = r.astype(z_ref.dtype)

    @pl.when((j == 4) | (j == 5))
    def _():
        r = _rope_apply(z, cc_ref[0], sc2_ref[0], HEAD)
        r = r * jnp.where(j == 4, HEAD ** -0.5 * LOG2E, 1.0)
        z_ref[0] = r.astype(z_ref.dtype)

    @pl.when((j != 1) & (j != 2) & (j != 4) & (j != 5))
    def _():
        z_ref[0] = z.astype(z_ref.dtype)


def _inproj(x, norm_g, scale, shift, w_in_bf16, tables, layer):
    b, t, d = x.shape
    n = w_in_bf16.shape[-1]
    tm = min(t, 1024)
    tn = GROUP
    cd, sd, cc, sc = tables
    spec_tab = pl.BlockSpec((1, tm, LANES), lambda bi, i, j: (bi, i, 0))
    spec_vec = pl.BlockSpec((1, 1, d), lambda bi, i, j: (bi, 0, 0))
    return pl.pallas_call(
        _inproj_kernel,
        grid=(b, t // tm, n // tn),
        in_specs=[
            pl.BlockSpec((1, tm, d), lambda bi, i, j: (bi, i, 0)),
            pl.BlockSpec((1, 1, d), lambda bi, i, j: (layer, 0, 0)),
            spec_vec, spec_vec,
            pl.BlockSpec((None, d, tn), lambda bi, i, j: (layer, 0, j)),
            spec_tab, spec_tab, spec_tab, spec_tab,
        ],
        out_specs=pl.BlockSpec((1, tm, tn), lambda bi, i, j: (bi, i, j)),
        out_shape=jax.ShapeDtypeStruct((b, t, n), BF16),
        scratch_shapes=[pltpu.VMEM((tm, d), BF16)],
        compiler_params=_params("arbitrary", "arbitrary", "arbitrary"),
        name="inproj",
    )(x, norm_g, scale, shift, w_in_bf16, cd, sd, cc, sc)


def _poolconv_kernel(zp_ref, zph_ref, za_ref, zah_ref, zg_ref, zgh_ref,
                     pw_ref, ps_ref, dww_ref, dwb_ref, lng_ref, lnb_ref, pww_ref, pwb_ref,
                     yp_ref, yc_ref, xp_scr, u_scr, *, tm):
    i = pl.program_id(1)
    keep = jnp.where(i == 0, 0.0, 1.0)

    xp_scr[0:HALO, :] = zph_ref[0].astype(F32) * keep
    xp_scr[HALO:, :] = zp_ref[0].astype(F32)
    t_glob = i * tm + lax.broadcasted_iota(I32, (tm, 1), 0)
    for g, w in enumerate(POOL_WINDOWS):
        cols = slice(g * LANES, (g + 1) * LANES)
        xg = xp_scr[HALO:HALO + tm, cols]
        acc = xg
        for k in range(1, w):
            acc = acc + xp_scr[HALO - k:HALO - k + tm, cols]
        cnt = jnp.minimum(t_glob + 1, w).astype(F32)
        pooled = acc / cnt - xg
        yg = jnp.dot(pooled.astype(BF16), pw_ref[0, g], preferred_element_type=F32)
        yp_ref[0, :, cols] = (yg * ps_ref[0, :, cols]).astype(yp_ref.dtype)

    ah = zah_ref[0].astype(F32)
    gh = zgh_ref[0].astype(F32)
    u_scr[0:HALO, :] = ah * _sigmoid(gh) * keep
    a = za_ref[0].astype(F32)
    gg = zg_ref[0].astype(F32)
    u_scr[HALO:, :] = a * _sigmoid(gg)
    acc = jnp.zeros((tm, GROUP), F32) + dwb_ref[0]
    base = HALO - (CONV_WIDTH - 1)
    for k in range(CONV_WIDTH):
        acc = acc + u_scr[base + k:base + k + tm, :] * dww_ref[0, k:k + 1, :]
    mu = jnp.mean(acc, axis=-1, keepdims=True)
    cen = acc - mu
    var = jnp.mean(cen * cen, axis=-1, keepdims=True)
    v = cen * lax.rsqrt(var + LN_EPS) * lng_ref[0] + lnb_ref[0]
    v = v * _sigmoid(v)
    y = jnp.dot(v.astype(BF16), pww_ref[0], preferred_element_type=F32) + pwb_ref[0]
    yc_ref[0] = y.astype(yc_ref.dtype)


def _poolconv(z, pool_w_bf16, pool_scale, dw_w, dw_b, ln_g, ln_b, pw_w_bf16, pw_b, layer):
    b, t, _ = z.shape
    tm = min(t, 512)
    r = tm // HALO
    a_blk = 7
    g_blk = 8

    def cur(col):
        return pl.BlockSpec((1, tm, GROUP), lambda bi, i: (bi, i, col))

    def halo(col):
        return pl.BlockSpec((1, HALO, GROUP), lambda bi, i: (bi, jnp.maximum(i * r - 1, 0), col))

    def vec(n):
        return pl.BlockSpec((1, 1, n), lambda bi, i: (layer, 0, 0))

    out_spec = pl.BlockSpec((1, tm, GROUP), lambda bi, i: (bi, i, 0))
    shp = jax.ShapeDtypeStruct((b, t, GROUP), BF16)
    return pl.pallas_call(
        functools.partial(_poolconv_kernel, tm=tm),
        grid=(b, t // tm),
        in_specs=[
            cur(0), halo(0), cur(a_blk), halo(a_blk), cur(g_blk), halo(g_blk),
            pl.BlockSpec((1, len(POOL_WINDOWS), LANES, LANES), lambda bi, i: (layer, 0, 0, 0)),
            vec(GROUP),
            pl.BlockSpec((1, CONV_WIDTH, GROUP), lambda bi, i: (layer, 0, 0)),
            vec(GROUP), vec(GROUP), vec(GROUP),
            pl.BlockSpec((1, GROUP, GROUP), lambda bi, i: (layer, 0, 0)),
            vec(GROUP),
        ],
        out_specs=[out_spec, out_spec],
        out_shape=[shp, shp],
        scratch_shapes=[pltpu.VMEM((tm + HALO, GROUP), F32), pltpu.VMEM((tm + HALO, GROUP), F32)],
        compiler_params=_params("arbitrary", "arbitrary"),
        name="pool_conv",
    )(z, z, z, z, z, z, pool_w_bf16, pool_scale, dw_w, dw_b, ln_g, ln_b, pw_w_bf16, pw_b)


def _diff_kernel(q_ref, k_ref, v_ref, lq1_ref, lk1_ref, lq2_ref, lk2_ref, g_ref, o_ref, *, tq, lambda_init):
    qi = pl.program_id(2)
    q = q_ref[0]
    lane = lax.broadcasted_iota(I32, (tq, HEAD), 1)
    zero = jnp.zeros_like(q)
    q2 = jnp.concatenate([jnp.where(lane < DIFF_QK, q, zero), jnp.where(lane >= DIFF_QK, q, zero)], axis=0)
    rows = 2 * tq

    def scores(j):
        kb = k_ref[0, pl.ds(pl.multiple_of(j * tq, tq), tq), :]
        return lax.dot_general(q2, kb, (((1,), (1,)), ((), ())), preferred_element_type=F32)

    def lane_tiles(s):
        return [s[:, c * LANES:(c + 1) * LANES] for c in range(tq // LANES)]

    row = lax.broadcasted_iota(I32, (rows, tq), 0) % tq
    col = lax.broadcasted_iota(I32, (rows, tq), 1)
    s_diag = jnp.where(col <= row, scores(qi), NEG_INF)

    def sweep_max(j, m_run):
        for part in lane_tiles(scores(j)):
            m_run = jnp.maximum(m_run, part)
        return m_run

    m_run = lax.fori_loop(0, qi, sweep_max, jnp.full((rows, LANES), NEG_INF, F32))
    for part in lane_tiles(s_diag):
        m_run = jnp.maximum(m_run, part)
    m = jnp.max(m_run, axis=-1, keepdims=True)

    def accumulate(p, vb, l_run, acc):
        for part in lane_tiles(p):
            l_run = l_run + part
        return l_run, acc + jnp.dot(p.astype(BF16), vb, preferred_element_type=F32)

    def sweep_acc(j, carry):
        vb = v_ref[0, pl.ds(pl.multiple_of(j * tq, tq), tq), :]
        return accumulate(jnp.exp2(scores(j) - m), vb, *carry)

    carry = lax.fori_loop(0, qi, sweep_acc, (jnp.zeros((rows, LANES), F32), jnp.zeros((rows, HEAD), F32)))
    vb_diag = v_ref[0, pl.ds(pl.multiple_of(qi * tq, tq), tq), :]
    l_run, acc = accumulate(jnp.exp2(s_diag - m), vb_diag, *carry)
    l = jnp.sum(l_run, axis=-1, keepdims=True)

    lam =(jnp.exp(jnp.sum(lq1_ref[0] * lk1_ref[0], axis=-1, keepdims=True))
           - jnp.exp(jnp.sum(lq2_ref[0] * lk2_ref[0], axis=-1, keepdims=True)) + lambda_init)
    o = acc / l
    od = o[:tq] - lam * o[tq:]
    y = od * lax.rsqrt(jnp.mean(od * od, axis=-1, keepdims=True) + RMS_EPS) * g_ref[0]
    o_ref[0] = (y * (1.0 - lambda_init)).astype(o_ref.dtype)


def _diff_attention(z, lq1, lk1, lq2, lk2, subln_g, layer, lambda_init):
    b, t, _ = z.shape
    heads = GROUP // HEAD
    tq = min(t, 512)
    q0, k0, v0 = 1 * heads, 2 * heads, 3 * heads

    def vec(n):
        return pl.BlockSpec((1, 1, n), lambda bi, h, i: (layer, 0, 0))

    return pl.pallas_call(
        functools.partial(_diff_kernel, tq=tq, lambda_init=lambda_init),
        grid=(b, heads, t // tq),
        in_specs=[
            pl.BlockSpec((1, tq, HEAD), lambda bi, h, i: (bi, i, q0 + h)),
            pl.BlockSpec((1, t, HEAD), lambda bi, h, i: (bi, 0, k0 + h)),
            pl.BlockSpec((1, t, HEAD), lambda bi, h, i: (bi, 0, v0 + h)),
            vec(DIFF_QK), vec(DIFF_QK), vec(DIFF_QK), vec(DIFF_QK), vec(HEAD),
        ],
        out_specs=pl.BlockSpec((1, tq, HEAD), lambda bi, h, i: (bi, i, h)),
        out_shape=jax.ShapeDtypeStruct((b, t, GROUP), BF16),
        compiler_params=_params("arbitrary", "arbitrary", "arbitrary"),
        name="diff_attention",
    )(z, z, z, lq1, lk1, lq2, lk2, subln_g)


def _dilated_kernel(q_ref, k_ref, v_ref, o_ref, qf, kf, vf, o0, o1, o2, e0, e1, e2, *, t, group):
    qf[...] = q_ref[0].astype(F32)
    kf[...] = k_ref[0].astype(F32)
    vf[...] = v_ref[0].astype(F32)
    band = DIL_BAND
    a_idx = lax.broadcasted_iota(I32, (band, 2 * band), 0)
    b_idx = lax.broadcasted_iota(I32, (band, 2 * band), 1)
    in_band = (b_idx >= a_idx) & (b_idx <= a_idx + band)
    outs = ((o0, e0), (o1, e1), (o2, e2))

    for (window, dil), (o_scr, e_scr) in zip(DILATED_PAIRS, outs):
        nb = t // (dil * band)

        def one_block(it, dil=dil, nb=nb, o_scr=o_scr, e_scr=e_scr):
            r = it // nb
            n = it % nb
            start = r + n * (band * dil)
            prev = r + jnp.maximum(n - 1, 0) * (band * dil)

            def rows(ref, s0):
                return ref[pl.ds(s0, band, stride=dil), :] if dil > 1 else ref[pl.ds(s0, band), :]

            qb = rows(qf, start).astype(BF16)
            kk = jnp.concatenate([rows(kf, prev), rows(kf, start)], axis=0).astype(BF16)
            vv = jnp.concatenate([rows(vf, prev), rows(vf, start)], axis=0).astype(BF16)
            s = lax.dot_general(qb, kk, (((1,), (1,)), ((), ())), preferred_element_type=F32)
            valid = in_band & (b_idx >= jnp.where(n > 0, 0, band))
            s = jnp.where(valid, s, NEG_INF)
            m = jnp.max(s, axis=-1, keepdims=True)
            p = jnp.exp2(s - m)
            l = jnp.sum(p, axis=-1, keepdims=True)
            o = jnp.dot(p.astype(BF16), vv, preferred_element_type=F32) / l
            idx = pl.ds(start, band, stride=dil) if dil > 1 else pl.ds(start, band)
            o_scr[idx, :] = o
            e_scr[idx, :] = jnp.broadcast_to(m + jnp.log2(l), (band, HEAD))

        def blocks(g, carry, one_block=one_block):
            for u in range(group):
                one_block(g * group + u)
            return carry

        lax.fori_loop(0, dil * nb // group, blocks, 0)

    top = jnp.maximum(jnp.maximum(e0[...], e1[...]), e2[...])
    w0 = jnp.exp2(e0[...] - top)
    w1 = jnp.exp2(e1[...] - top)
    w2 = jnp.exp2(e2[...] - top)
    mix = (w0 * o0[...] + w1 * o1[...] + w2 * o2[...]) / (w0 + w1 + w2)
    o_ref[0] = mix.astype(o_ref.dtype)


def _dilated_attention(z):
    b, t, _ = z.shape
    heads = GROUP // HEAD
    q0, k0, v0 = 4 * heads, 5 * heads, 6 * heads
    group = 4
    assert all(w // dl == DIL_BAND and t % w == 0 for w, dl in DILATED_PAIRS)
    assert (t // DIL_BAND) % group == 0

    def col(c0):
        return pl.BlockSpec((1, t, HEAD), lambda bi, h: (bi, 0, c0 + h))

    return pl.pallas_call(
        functools.partial(_dilated_kernel, t=t, group=group),
        grid=(b, heads),
        in_specs=[col(q0), col(k0), col(v0)],
        out_specs=pl.BlockSpec((1, t, HEAD), lambda bi, h: (bi, 0, h)),
        out_shape=jax.ShapeDtypeStruct((b, t, GROUP), BF16),
        scratch_shapes=[pltpu.VMEM((t, HEAD), F32)] * 9,
        compiler_params=_params("arbitrary", "arbitrary"),
        name="dilated_attention",
    )(z, z, z)


def _pack_rows(h, out_ref, row0=0):
    m, half = h.shape[0], h.shape[1] // 2
    bits = pltpu.bitcast(h.astype(BF16).astype(F32), U32)
    word = (bits[:, half:] & jnp.uint32(0xFFFF0000)) | (bits[:, :half] >> 16)
    n_slab = half // LANES
    for s in range(n_slab):
        out_ref[pl.ds(row0 * n_slab + s, m, stride=n_slab), :] = word[:, s * LANES:(s + 1) * LANES]


def _unpack_rows(word):
    return pltpu.bitcast(word << 16, F32), pltpu.bitcast(word & jnp.uint32(0xFFFF0000), F32)


def _outproj_router_kernel(yp_ref, yd_ref, yc_ref, yv_ref, wo_ref, x_ref, g1_ref, n2_ref, sc_ref, sh_ref,
                           rw_ref, rb_ref,
                           x1_ref, h2_ref, idx_ref, rank_ref, wts_ref, cnt_ref, carry_scr, *, tm):
    first = (pl.program_id(0) == 0) & (pl.program_id(1) == 0)

    @pl.when(first)
    def _():
        carry_scr[...] = jnp.zeros_like(carry_scr)

    mix = jnp.dot(yp_ref[0], wo_ref[0 * GROUP:1 * GROUP, :], preferred_element_type=F32)
    mix = mix + jnp.dot(yd_ref[0], wo_ref[1 * GROUP:2 * GROUP, :], preferred_element_type=F32)
    mix = mix + jnp.dot(yc_ref[0], wo_ref[2 * GROUP:3 * GROUP, :], preferred_element_type=F32)
    mix = mix + jnp.dot(yv_ref[0], wo_ref[3 * GROUP:4 * GROUP, :], preferred_element_type=F32)
    x1 = x_ref[0] + g1_ref[0] * mix
    x1_ref[0] = x1

    y = x1 * lax.rsqrt(jnp.mean(x1 * x1, axis=-1, keepdims=True) + RMS_EPS) * n2_ref[0]
    h2 = y * (1.0 + sc_ref[0]) + sh_ref[0]
    _pack_rows(h2, h2_ref)

    logits = lax.dot_general(rw_ref[0], h2, (((1,), (1,)), ((), ())), precision=HIGHEST,
                             preferred_element_type=F32) + rb_ref[0]
    e_idx = lax.broadcasted_iota(I32, (N_EXPERTS, tm), 0)
    work = logits
    vals, sels, hots = [], [], []
    for _ in range(TOP_K):
        mx = jnp.max(work, axis=0, keepdims=True)
        sel = jnp.min(jnp.where(work == mx, e_idx, N_EXPERTS), axis=0, keepdims=True)
        hot = e_idx == sel
        vals.append(mx)
        sels.append(sel)
        hots.append(hot)
        work = jnp.where(hot, -jnp.inf, work)
    exps = [jnp.exp(v - vals[0]) for v in vals]
    denom = exps[0] + exps[1] + exps[2] + exps[3]

    chosen = jnp.zeros((N_EXPERTS, tm), F32)
    for hot in hots:
        chosen = chosen + hot.astype(F32)
    s_idx = lax.broadcasted_iota(I32, (tm, tm), 0)
    t_idx = lax.broadcasted_iota(I32, (tm, tm), 1)
    upper = (s_idx < t_idx).astype(BF16)
    before = jnp.dot(chosen.astype(BF16), upper, preferred_element_type=F32) + carry_scr[:, 0:1]
    for k in range(TOP_K):
        idx_ref[k:k + 1, :] = sels[k]
        rank_ref[k:k + 1, :] = jnp.sum(jnp.where(hots[k], before, 0.0), axis=0, keepdims=True).astype(I32)
        wts_ref[k:k + 1, :] = exps[k] / denom
    carry_scr[...] = carry_scr[...] + jnp.sum(chosen, axis=1, keepdims=True)
    cnt_ref[...] = carry_scr[...]


def _outproj_router(ys, w_out_bf16, x, gate1, norm2_g, scale2, shift2, router_wt, router_b, layer):
    b, t, d = x.shape
    tm = min(t, 512)
    n = b * t
    nt = t // tm
    slab = d // 2 // LANES

    def ytile():
        return pl.BlockSpec((1, tm, GROUP), lambda bi, i: (bi, i, 0))

    def bvec():
        return pl.BlockSpec((1, 1, d), lambda bi, i: (bi, 0, 0))

    tok = pl.BlockSpec((TOP_K, tm), lambda bi, i: (0, bi * nt + i))
    outs = pl.pallas_call(
        functools.partial(_outproj_router_kernel, tm=tm),
        grid=(b, nt),
        in_specs=[
            ytile(), ytile(), ytile(), ytile(),
            pl.BlockSpec((None, 4 * GROUP, d), lambda bi, i: (layer, 0, 0)),
            pl.BlockSpec((1, tm, d), lambda bi, i: (bi, i, 0)),
            bvec(),
            pl.BlockSpec((1, 1, d), lambda bi, i: (layer, 0, 0)),
            bvec(), bvec(),
            pl.BlockSpec((1, N_EXPERTS, d), lambda bi, i: (layer, 0, 0)),
            pl.BlockSpec((1, N_EXPERTS, 1), lambda bi, i: (layer, 0, 0)),
        ],
        out_specs=[
            pl.BlockSpec((1, tm, d), lambda bi, i: (bi, i, 0)),
            pl.BlockSpec((tm * slab, LANES), lambda bi, i: (bi * nt + i, 0)),
            tok, tok, tok,
            pl.BlockSpec((N_EXPERTS, LANES), lambda bi, i: (0, 0)),
        ],
        out_shape=[
            jax.ShapeDtypeStruct((b, t, d), F32),
            jax.ShapeDtypeStruct((n * slab, LANES), U32),
            jax.ShapeDtypeStruct((TOP_K, n), I32),
            jax.ShapeDtypeStruct((TOP_K, n), I32),
            jax.ShapeDtypeStruct((TOP_K, n), F32),
            jax.ShapeDtypeStruct((N_EXPERTS, LANES), F32),
        ],
        scratch_shapes=[pltpu.VMEM((N_EXPERTS, LANES), F32)],
        compiler_params=_params("arbitrary", "arbitrary"),
        name="outproj_router",
    )(*ys, w_out_bf16, x, gate1, norm2_g, scale2, shift2, router_wt, router_b)
    return outs


def _scatter_kernel(zs_ref, pos_ref, h_ref, xs_hbm, zbuf, zsem, sem, *, tm, tm_e):
    zrows = zbuf.shape[0]

    @pl.when(pl.program_id(0) == 0)
    def _():
        zbuf[...] = jnp.zeros_like(zbuf)

        def zero(e, c):
            @pl.when(zs_ref[e] >= 0)
            def _():
                for q in range(tm_e // zrows):
                    pltpu.make_async_copy(zbuf, xs_hbm.at[pl.ds(zs_ref[e] + q * zrows, zrows)], zsem).start()
            return c

        lax.fori_loop(0, N_EXPERTS, zero, 0)

        def zero_done(e, c):
            @pl.when(zs_ref[e] >= 0)
            def _():
                for q in range(tm_e // zrows):
                    pltpu.make_async_copy(zbuf, xs_hbm.at[pl.ds(0, zrows)], zsem).wait()
            return c

        lax.fori_loop(0, N_EXPERTS, zero_done, 0)

    def issue(tt, c):
        for k in range(TOP_K):
            pltpu.make_async_copy(h_ref.at[tt], xs_hbm.at[pos_ref[k, tt]], sem).start()
        return c

    lax.fori_loop(0, tm, issue, 0)

    def drain(tt, c):
        for k in range(TOP_K):
            pltpu.make_async_copy(h_ref.at[0], xs_hbm.at[0], sem).wait()
        return c

    lax.fori_loop(0, tm, drain, 0)


def _scatter_rows(zero_start, pos, h_rows, m_pad, tm_e):
    n, slab, _ = h_rows.shape
    tm = min(n, 256)
    zrows = min(tm_e, 256)
    grid_spec = pltpu.PrefetchScalarGridSpec(
        num_scalar_prefetch=1,
        grid=(n // tm,),
        in_specs=[
            pl.BlockSpec((TOP_K, tm), lambda i, zs: (0, i), memory_space=pltpu.SMEM),
            pl.BlockSpec((tm, slab, LANES), lambda i, zs: (i, 0, 0)),
        ],
        out_specs=pl.BlockSpec(memory_space=pl.ANY),
        scratch_shapes=[pltpu.VMEM((zrows, slab, LANES), U32), pltpu.SemaphoreType.DMA(()),
                        pltpu.SemaphoreType.DMA(())],
    )
    return pl.pallas_call(
        functools.partial(_scatter_kernel, tm=tm, tm_e=tm_e),
        grid_spec=grid_spec,
        out_shape=jax.ShapeDtypeStruct((m_pad, slab, LANES), U32),
        compiler_params=_params("arbitrary"),
        name="scatter_rows",
    )(zero_start, pos, h_rows)


def _deinterleave(hh):
    m, width = hh.shape
    lane = lax.broadcasted_iota(I32, (m, LANES), 1)
    low = lane < LANES // 2
    evens_then_odds = jnp.where(low, 2 * lane, 2 * lane - (LANES - 1))
    parts = [jnp.take_along_axis(hh[:, b * LANES:(b + 1) * LANES], evens_then_odds, axis=1)
             for b in range(width // LANES)]
    gates, lins = [], []
    for b in range(0, len(parts), 2):
        first, second = parts[b], parts[b + 1]
        gates.append(jnp.where(low, first, pltpu.roll(second, LANES // 2, 1)))
        lins.append(jnp.where(low, pltpu.roll(first, LANES // 2, 1), second))
    return jnp.concatenate(gates, axis=1), jnp.concatenate(lins, axis=1)


def _expert_kernel(te_ref, tv_ref, nu_ref, xs_ref, w1_ref, b1_ref, w2_ref, b2_ref, ys_ref,
                   x_scr, acc_scr, w1b, w2b, *, tm, sub, nc):
    i = pl.program_id(0)
    c = pl.program_id(1)
    d = x_scr.shape[1]
    n_slab = d // 2 // LANES
    live = i < nu_ref[0]

    @pl.when(live)
    def _():
        @pl.when(c == 0)
        def _():
            for s in range(n_slab):
                lo, hi = _unpack_rows(xs_ref[pl.ds(s, tm, stride=n_slab), :])
                x_scr[:, s * LANES:(s + 1) * LANES] = lo.astype(BF16)
                x_scr[:, d // 2 + s * LANES:d // 2 + (s + 1) * LANES] = hi.astype(BF16)
            acc_scr[...] = jnp.zeros_like(acc_scr) + b2_ref[...]

        w1b[...] = w1_ref[...].astype(BF16)
        w2b[...] = w2_ref[...].astype(BF16)
        for sb in range(tm // sub):
            @pl.when(sb * sub < tv_ref[i])
            def _(sb=sb):
                rows = slice(sb * sub, (sb + 1) * sub)
                hh = jnp.dot(x_scr[rows, :], w1b[...], preferred_element_type=F32) + b1_ref[...]
                g, lin = _deinterleave(hh)
                g = jnp.minimum(g, SWIGLU_LIMIT)
                lin = jnp.clip(lin, -SWIGLU_LIMIT, SWIGLU_LIMIT)
                act = g * _sigmoid(SWIGLU_ALPHA * g) * (lin + 1.0)
                acc_scr[rows, :] += jnp.dot(act.astype(BF16), w2b[...], preferred_element_type=F32)

        @pl.when(c == nc - 1)
        def _():
            for sb in range(tm // sub):
                _pack_rows(acc_scr[sb * sub:(sb + 1) * sub, :], ys_ref, row0=sb * sub)

    @pl.when(jnp.logical_not(live) & (c == 0))
    def _():
        ys_ref[...] = jnp.zeros_like(ys_ref)


def _experts(tile_expert, tile_valid, n_used, xs2d, w1, b1, w2, b2, layer, tm, sub, m_pad):
    _, n_exp, d, h2 = w1.shape
    hid = h2 // 2
    tc = min(hid, 256)
    nc = hid // tc
    n_tiles = m_pad // tm
    slab = d // 2 // LANES

    def live(i, nu):
        return jnp.minimum(i, nu[0] - 1)

    def chunk(i, c, nu):
        return jnp.where(i < nu[0], c, nc - 1)

    grid_spec = pltpu.PrefetchScalarGridSpec(
        num_scalar_prefetch=3,
        grid=(n_tiles, nc),
        in_specs=[
            pl.BlockSpec((tm * slab, LANES), lambda i, c, te, tv, nu: (live(i, nu), 0)),
            pl.BlockSpec((None, None, d, 2 * tc), lambda i, c, te, tv, nu: (layer, te[i], 0, chunk(i, c, nu))),
            pl.BlockSpec((None, None, 1, 2 * tc), lambda i, c, te, tv, nu: (layer, te[i], 0, chunk(i, c, nu))),
            pl.BlockSpec((None, None, tc, d), lambda i, c, te, tv, nu: (layer, te[i], chunk(i, c, nu), 0)),
            pl.BlockSpec((None, None, 1, d), lambda i, c, te, tv, nu: (layer, te[i], 0, 0)),
        ],
        out_specs=pl.BlockSpec((tm * slab, LANES), lambda i, c, te, tv, nu: (i, 0)),
        scratch_shapes=[pltpu.VMEM((tm, d), BF16), pltpu.VMEM((tm, d), F32),
                        pltpu.VMEM((d, 2 * tc), BF16), pltpu.VMEM((tc, d), BF16)],
    )
    return pl.pallas_call(
        functools.partial(_expert_kernel, tm=tm, sub=sub, nc=nc),
        grid_spec=grid_spec,
        out_shape=jax.ShapeDtypeStruct((m_pad * slab, LANES), U32),
        compiler_params=_params("arbitrary", "arbitrary"),
        name="experts",
    )(tile_expert, tile_valid, n_used, xs2d, w1, b1, w2, b2)


def _combine_kernel(pos_ref, wts_ref, x_ref, g2_ref, fg_ref, ys_hbm, o_ref, buf, sem, *, tm, final):
    n_slab = x_ref.shape[1] // 2 // LANES

    def issue(tt, c):
        for k in range(TOP_K):
            dst = buf.at[pl.ds(pl.multiple_of((k * tm + tt) * n_slab, n_slab), n_slab)]
            pltpu.make_async_copy(ys_hbm.at[pos_ref[k, tt]], dst, sem).start()
        return c

    lax.fori_loop(0, tm, issue, 0)

    def drain(tt, c):
        for k in range(TOP_K):
            pltpu.make_async_copy(ys_hbm.at[0], buf.at[pl.ds(0, n_slab)], sem).wait()
        return c

    lax.fori_loop(0, tm, drain, 0)

    w_sq = jnp.concatenate([wts_ref[...], jnp.zeros((tm - TOP_K, tm), F32)], axis=0)
    w_t = w_sq.T
    w_k = [jnp.broadcast_to(w_t[:, k:k + 1], (tm, LANES)) for k in range(TOP_K)]
    lows, highs = [], []
    for s in range(n_slab):
        acc_lo = jnp.zeros((tm, LANES), F32)
        acc_hi = jnp.zeros((tm, LANES), F32)
        for k in range(TOP_K):
            lo, hi = _unpack_rows(buf[pl.ds(k * tm * n_slab + s, tm, stride=n_slab), :])
            acc_lo = acc_lo + w_k[k] * lo
            acc_hi = acc_hi + w_k[k] * hi
        lows.append(acc_lo)
        highs.append(acc_hi)
    moe = jnp.concatenate(lows + highs, axis=1)
    x2 = x_ref[...] + g2_ref[0] * moe
    if final:
        x2 = x2 * lax.rsqrt(jnp.mean(x2 * x2, axis=-1, keepdims=True) + RMS_EPS) * fg_ref[...]
    o_ref[...] = x2


def _combine(pos, wts, x1, gate2, final_g, ys_rows, tokens_per_batch, final):
    n, d = x1.shape
    tm = 256
    steps_per_batch = tokens_per_batch // tm
    return pl.pallas_call(
        functools.partial(_combine_kernel, tm=tm, final=final),
        grid=(n // tm,),
        in_specs=[
            pl.BlockSpec((TOP_K, tm), lambda i: (0, i), memory_space=pltpu.SMEM),
            pl.BlockSpec((TOP_K, tm), lambda i: (0, i)),
            pl.BlockSpec((tm, d), lambda i: (i, 0)),
            pl.BlockSpec((1, 1, d), lambda i: (i // steps_per_batch, 0, 0)),
            pl.BlockSpec((1, d), lambda i: (0, 0)),
            pl.BlockSpec(memory_space=pl.ANY),
        ],
        out_specs=pl.BlockSpec((tm, d), lambda i: (i, 0)),
        out_shape=jax.ShapeDtypeStruct((n, d), F32),
        scratch_shapes=[pltpu.VMEM((TOP_K * tm * (d // 2 // LANES), LANES), U32), pltpu.SemaphoreType.DMA(())],
        compiler_params=_params("arbitrary"),
        name="combine",
    )(pos, wts, x1, gate2, final_g, ys_rows)


def _routing_plan(idx, rank, counts, tm_e, n_tiles):
    experts = jnp.arange(N_EXPERTS, dtype=I32)
    cnt = counts[:, 0].astype(I32)
    tiles = (cnt + tm_e - 1) // tm_e
    tile_end = jnp.cumsum(tiles)
    tile_start = tile_end - tiles
    offsets = tile_start * tm_e
    pos = rank + jnp.sum(jnp.where(idx[..., None] == experts, offsets, 0), axis=-1)
    n_used = tile_end[-1]
    tile_ids = jnp.arange(n_tiles, dtype=I32)
    te = jnp.sum((tile_ids[:, None] >= tile_end[None, :]).astype(I32), axis=1)
    te_last = jnp.sum((n_used - 1 >= tile_end).astype(I32))
    te = jnp.where(tile_ids < n_used, te, te_last).astype(I32)
    mine = te[:, None] == experts[None, :]
    rows_left = jnp.sum(jnp.where(mine, cnt[None, :] - (tile_ids[:, None] - tile_start[None, :]) * tm_e, 0), axis=1)
    tile_valid = jnp.where(tile_ids < n_used, jnp.clip(rows_left, 0, tm_e), 0).astype(I32)
    zero_start = jnp.where(tiles > 0, (tile_end - 1) * tm_e, -1).astype(I32)
    return pos.astype(I32), te, tile_valid, n_used.reshape(1).astype(I32), zero_start


def kernel(x, c, positions, mod_w, mod_b, norm1_g, norm2_g, w_in, pool_w, pool_scale, diff_lq1, diff_lk1,
           diff_lq2, diff_lk2, diff_subln_g, conv_dw_w, conv_dw_b, conv_ln_g, conv_ln_b, conv_pw_w, conv_pw_b,
           w_out, router_w, router_b, exp_w1, exp_b1, exp_w2, exp_b2, final_g):
    b, t, d = x.shape
    depth = mod_w.shape[0]
    n = b * t
    tm_e = 1024
    sub_e = 512
    m_pad = n * TOP_K + N_EXPERTS * tm_e
    n_tiles = m_pad // tm_e
    slab = d // 2 // LANES

    def row3(a):
        return a.reshape(a.shape[0], 1, a.shape[1])

    mod = _modulation(c, mod_w, mod_b)
    tables = _rope_tables(positions)
    w_in_b = w_in.astype(BF16)
    w_out_b = w_out.astype(BF16)
    pool_w_b = pool_w.astype(BF16)
    pw_w_b = conv_pw_w.astype(BF16)
    router_wt = jnp.swapaxes(router_w, 1, 2)
    router_b3 = router_b.reshape(depth, N_EXPERTS, 1)
    b1r = exp_b1.reshape(depth, N_EXPERTS, 1, exp_b1.shape[-1])
    b2r = exp_b2.reshape(depth, N_EXPERTS, 1, d)
    final_g2 = final_g.reshape(1, d)

    for l in range(depth):
        lambda_init = 0.8 - 0.6 * math.exp(-0.3 * l)
        sh1, sc1, g1, sh2, sc2, g2 = [m.reshape(b, 1, d) for m in jnp.split(mod[l], 6, axis=-1)]
        z = _inproj(x, row3(norm1_g), sc1, sh1, w_in_b, tables, l)
        y_pool, y_conv = _poolconv(z, pool_w_b, row3(pool_scale), conv_dw_w, row3(conv_dw_b), row3(conv_ln_g),
                                   row3(conv_ln_b), pw_w_b, row3(conv_pw_b), l)
        y_diff = _diff_attention(z, row3(diff_lq1), row3(diff_lk1), row3(diff_lq2), row3(diff_lk2),
                                 row3(diff_subln_g), l, lambda_init)
        y_dil = _dilated_attention(z)
        x1, h2, idx, rank, wts, counts = _outproj_router(
            (y_pool, y_diff, y_dil, y_conv), w_out_b, x, g1, row3(norm2_g), sc2, sh2, router_wt, router_b3, l)
        pos, tile_expert, tile_valid, n_used, zero_start = _routing_plan(idx, rank, counts, tm_e, n_tiles)
        xs = _scatter_rows(zero_start, pos, h2.reshape(n, slab, LANES), m_pad, tm_e)
        ys = _experts(tile_expert, tile_valid, n_used, xs.reshape(m_pad * slab, LANES), exp_w1, b1r, exp_w2, b2r,
                      l, tm_e, sub_e, m_pad)
        x = _combine(pos, wts, x1.reshape(n, d), g2, final_g2, ys.reshape(m_pad, slab, LANES), t,
                     final=(l == depth - 1)).reshape(b, t, d)
    return x
```

```python
import functools
import math

import numpy as np
import jax
import jax.numpy as jnp
from jax import lax
from jax.experimental import pallas as pl
from jax.experimental.pallas import tpu as pltpu

F32 = jnp.float32
BF16 = jnp.bfloat16
I32 = jnp.int32
U32 = jnp.uint32
HIGHEST = lax.Precision.HIGHEST

LANES = 128
VMEM_LIMIT_BYTES = 56 * 1024 * 1024

POOL_WINDOWS = (2, 4, 8, 16)
DILATED_PAIRS = ((128, 1), (512, 4), (2048, 16))
DIL_BAND = 128
CONV_WIDTH = 31
HALO = 32
ISSUE_UNROLL = 8
N_EXPERTS = 32
TOP_K = 4
SWIGLU_ALPHA = 1.702
SWIGLU_LIMIT = 7.0
ROPE_THETA = 500000.0
RMS_EPS = 1e-6
LN_EPS = 1e-5
NEG_INF = -1e30
GROUP = 512
HEAD = 128
DIFF_QK = 64
LOG2E = math.log2(math.e)


def _params(*semantics):
    return pltpu.CompilerParams(dimension_semantics=semantics, vmem_limit_bytes=VMEM_LIMIT_BYTES)


def _sigmoid(x):
    return 1.0 / (1.0 + jnp.exp(-x))


def _mod_kernel(c_ref, w_ref, b_ref, o_ref):
    c = c_ref[...]
    ca = c * _sigmoid(c)
    o_ref[0] = jnp.dot(ca, w_ref[0], precision=HIGHEST, preferred_element_type=F32) + b_ref[0]


def _modulation(c, mod_w, mod_b):
    depth, d, n = mod_w.shape
    b = c.shape[0]
    rows = 8
    c_pad = jnp.zeros((rows, d), F32).at[:b].set(c)
    tn = 512
    out = pl.pallas_call(
        _mod_kernel,
        grid=(depth, n // tn),
        in_specs=[
            pl.BlockSpec((rows, d), lambda l, j: (0, 0)),
            pl.BlockSpec((1, d, tn), lambda l, j: (l, 0, j)),
            pl.BlockSpec((1, 1, tn), lambda l, j: (l, 0, j)),
        ],
        out_specs=pl.BlockSpec((1, rows, tn), lambda l, j: (l, 0, j)),
        out_shape=jax.ShapeDtypeStruct((depth, rows, n), F32),
        compiler_params=_params("arbitrary", "arbitrary"),
        name="modulation",
    )(c_pad, mod_w, mod_b.reshape(depth, 1, n))
    return out[:, :b]


def _rope_table_kernel(pos_ref, invd_ref, invc_ref, cd_ref, sd_ref, cc_ref, sc_ref):
    p = pos_ref[0]
    lane = lax.broadcasted_iota(I32, (1, LANES), 1)
    for inv_ref, c_ref, s_ref, hd in ((invd_ref, cd_ref, sd_ref, DIFF_QK), (invc_ref, cc_ref, sc_ref, HEAD)):
        half = hd // 8
        lm = lane % hd
        ang = p * inv_ref[...]
        c_ref[0] = jnp.cos(ang)
        s_ref[0] = jnp.where(lm < half, -jnp.sin(ang), jnp.sin(ang))


def _lane_inv_freq(hd):
    half = hd // 8
    inv = ROPE_THETA ** (-jnp.arange(half, dtype=F32) / half)
    lm = np.arange(LANES) % hd
    rotated = jnp.asarray(lm < 2 * half)
    return jnp.where(rotated, inv[lm % half], 0.0).reshape(1, LANES).astype(F32)


def _rope_tables(positions):
    b, t = positions.shape
    tm = min(t, 1024)
    pos = positions.astype(F32).reshape(b, t, 1)
    spec_t = pl.BlockSpec((1, tm, LANES), lambda bi, i: (bi, i, 0))
    spec_inv = pl.BlockSpec((1, LANES), lambda bi, i: (0, 0))
    shp = jax.ShapeDtypeStruct((b, t, LANES), F32)
    return pl.pallas_call(
        _rope_table_kernel,
        grid=(b, t // tm),
        in_specs=[pl.BlockSpec((1, tm, 1), lambda bi, i: (bi, i, 0)), spec_inv, spec_inv],
        out_specs=[spec_t] * 4,
        out_shape=[shp] * 4,
        compiler_params=_params("arbitrary", "arbitrary"),
        name="rope_tables",
    )(pos, _lane_inv_freq(DIFF_QK), _lane_inv_freq(HEAD))


def _rope_apply(z, cos, sin, hd):
    half = hd // 8
    lane = lax.broadcasted_iota(I32, (1, LANES), 1)
    first = (lane % hd) < half
    outs = []
    for cb in range(z.shape[1] // LANES):
        zc = z[:, cb * LANES:(cb + 1) * LANES]
        partner = jnp.where(first, pltpu.roll(zc, LANES - half, 1), pltpu.roll(zc, half, 1))
        outs.append(zc * cos + partner * sin)
    return jnp.concatenate(outs, axis=1)


def _inproj_kernel(x_ref, g_ref, sc_ref, sh_ref, w_ref, cd_ref, sd_ref, cc_ref, sc2_ref, z_ref, h_scr):
    j = pl.program_id(2)

    @pl.when(j == 0)
    def _():
        x = x_ref[0]
        y = x * lax.rsqrt(jnp.mean(x * x, axis=-1, keepdims=True) + RMS_EPS) * g_ref[0]
        h_scr[...] = (y * (1.0 + sc_ref[0]) + sh_ref[0]).astype(BF16)

    z = jnp.dot(h_scr[...], w_ref[...], preferred_element_type=F32)

    @pl.when((j == 1) | (j == 2))
    def _():
        r = _rope_apply(z, cd_ref[0], sd_ref[0], DIFF_QK)
        r = r * jnp.where(j == 1, DIFF_QK ** -0.5 * LOG2E, 1.0)
        z_ref[0] = r.astype(z_ref.dtype)

    @pl.when((j == 4) | (j == 5))
    def _():
        r = _rope_apply(z, cc_ref[0], sc2_ref[0], HEAD)
        r = r * jnp.where(j == 4, HEAD ** -0.5 * LOG2E, 1.0)
        z_ref[0] = r.astype(z_ref.dtype)

    @pl.when((j != 1) & (j != 2) & (j != 4) & (j != 5))
    def _():
        z_ref[0] = z.astype(z_ref.dtype)


def _inproj(x, norm_g, scale, shift, w_in_bf16, tables, layer):
    b, t, d = x.shape
    n = w_in_bf16.shape[-1]
    tm = min(t, 1024)
    tn = GROUP
    cd, sd, cc, sc = tables
    spec_tab = pl.BlockSpec((1, tm, LANES), lambda bi, i, j: (bi, i, 0))
    spec_vec = pl.BlockSpec((1, 1, d), lambda bi, i, j: (bi, 0, 0))
    return pl.pallas_call(
        _inproj_kernel,
        grid=(b, t // tm, n // tn),
        in_specs=[
            pl.BlockSpec((1, tm, d), lambda bi, i, j: (bi, i, 0)),
            pl.BlockSpec((1, 1, d), lambda bi, i, j: (layer, 0, 0)),
            spec_vec, spec_vec,
            pl.BlockSpec((None, d, tn), lambda bi, i, j: (layer, 0, j)),
            spec_tab, spec_tab, spec_tab, spec_tab,
        ],
        out_specs=pl.BlockSpec((1, tm, tn), lambda bi, i, j: (bi, i, j)),
        out_shape=jax.ShapeDtypeStruct((b, t, n), BF16),
        scratch_shapes=[pltpu.VMEM((tm, d), BF16)],
        compiler_params=_params("arbitrary", "arbitrary", "arbitrary"),
        name="inproj",
    )(x, norm_g, scale, shift, w_in_bf16, cd, sd, cc, sc)


def _poolconv_kernel(zp_ref, zph_ref, za_ref, zah_ref, zg_ref, zgh_ref,
                     pw_ref, ps_ref, dww_ref, dwb_ref, lng_ref, lnb_ref, pww_ref, pwb_ref,
                     yp_ref, yc_ref, xp_scr, u_scr, *, tm):
    i = pl.program_id(1)
    keep = jnp.where(i == 0, 0.0, 1.0)

    xp_scr[0:HALO, :] = zph_ref[0].astype(F32) * keep
    xp_scr[HALO:, :] = zp_ref[0].astype(F32)
    t_glob = i * tm + lax.broadcasted_iota(I32, (tm, 1), 0)
    for g, w in enumerate(POOL_WINDOWS):
        cols = slice(g * LANES, (g + 1) * LANES)
        xg = xp_scr[HALO:HALO + tm, cols]
        acc = xg
        for k in range(1, w):
            acc = acc + xp_scr[HALO - k:HALO - k + tm, cols]
        cnt = jnp.minimum(t_glob + 1, w).astype(F32)
        pooled = acc / cnt - xg
        yg = jnp.dot(pooled.astype(BF16), pw_ref[0, g], preferred_element_type=F32)
        yp_ref[0, :, cols] = (yg * ps_ref[0, :, cols]).astype(yp_ref.dtype)

    ah = zah_ref[0].astype(F32)
    gh = zgh_ref[0].astype(F32)
    u_scr[0:HALO, :] = ah * _sigmoid(gh) * keep
    a = za_ref[0].astype(F32)
    gg = zg_ref[0].astype(F32)
    u_scr[HALO:, :] = a * _sigmoid(gg)
    acc = jnp.zeros((tm, GROUP), F32) + dwb_ref[0]
    base = HALO - (CONV_WIDTH - 1)
    for k in range(CONV_WIDTH):
        acc = acc + u_scr[base + k:base + k + tm, :] * dww_ref[0, k:k + 1, :]
    mu = jnp.mean(acc, axis=-1, keepdims=True)
    cen = acc - mu
    var = jnp.mean(cen * cen, axis=-1, keepdims=True)
    v = cen * lax.rsqrt(var + LN_EPS) * lng_ref[0] + lnb_ref[0]
    v = v * _sigmoid(v)
    y = jnp.dot(v.astype(BF16), pww_ref[0], preferred_element_type=F32) + pwb_ref[0]
    yc_ref[0] = y.astype(yc_ref.dtype)


def _poolconv(z, pool_w_bf16, pool_scale, dw_w, dw_b, ln_g, ln_b, pw_w_bf16, pw_b, layer):
    b, t, _ = z.shape
    tm = min(t, 512)
    r = tm // HALO
    a_blk = 7
    g_blk = 8

    def cur(col):
        return pl.BlockSpec((1, tm, GROUP), lambda bi, i: (bi, i, col))

    def halo(col):
        return pl.BlockSpec((1, HALO, GROUP), lambda bi, i: (bi, jnp.maximum(i * r - 1, 0), col))

    def vec(n):
        return pl.BlockSpec((1, 1, n), lambda bi, i: (layer, 0, 0))

    out_spec = pl.BlockSpec((1, tm, GROUP), lambda bi, i: (bi, i, 0))
    shp = jax.ShapeDtypeStruct((b, t, GROUP), BF16)
    return pl.pallas_call(
        functools.partial(_poolconv_kernel, tm=tm),
        grid=(b, t // tm),
        in_specs=[
            cur(0), halo(0), cur(a_blk), halo(a_blk), cur(g_blk), halo(g_blk),
            pl.BlockSpec((1, len(POOL_WINDOWS), LANES, LANES), lambda bi, i: (layer, 0, 0, 0)),
            vec(GROUP),
            pl.BlockSpec((1, CONV_WIDTH, GROUP), lambda bi, i: (layer, 0, 0)),
            vec(GROUP), vec(GROUP), vec(GROUP),
            pl.BlockSpec((1, GROUP, GROUP), lambda bi, i: (layer, 0, 0)),
            vec(GROUP),
        ],
        out_specs=[out_spec, out_spec],
        out_shape=[shp, shp],
        scratch_shapes=[pltpu.VMEM((tm + HALO, GROUP), F32), pltpu.VMEM((tm + HALO, GROUP), F32)],
        compiler_params=_params("arbitrary", "arbitrary"),
        name="pool_conv",
    )(z, z, z, z, z, z, pool_w_bf16, pool_scale, dw_w, dw_b, ln_g, ln_b, pw_w_bf16, pw_b)


def _diff_kernel(q_ref, k_ref, v_ref, lq1_ref, lk1_ref, lq2_ref, lk2_ref, g_ref, o_ref, *, tq, lambda_init):
    qi = pl.program_id(2)
    q = q_ref[0]
    lane = lax.broadcasted_iota(I32, (tq, HEAD), 1)
    zero = jnp.zeros_like(q)
    q2 = jnp.concatenate([jnp.where(lane < DIFF_QK, q, zero), jnp.where(lane >= DIFF_QK, q, zero)], axis=0)
    rows = 2 * tq

    def scores(j):
        kb = k_ref[0, pl.ds(pl.multiple_of(j * tq, tq), tq), :]
        return lax.dot_general(q2, kb, (((1,), (1,)), ((), ())), preferred_element_type=F32)

    def lane_tiles(s):
        return [s[:, c * LANES:(c + 1) * LANES] for c in range(tq // LANES)]

    row = lax.broadcasted_iota(I32, (rows, tq), 0) % tq
    col = lax.broadcasted_iota(I32, (rows, tq), 1)
    s_diag = jnp.where(col <= row, scores(qi), NEG_INF)

    def sweep_max(j, m_run):
        for part in lane_tiles(scores(j)):
            m_run = jnp.maximum(m_run, part)
        return m_run

    m_run = lax.fori_loop(0, qi, sweep_max, jnp.full((rows, LANES), NEG_INF, F32))
    for part in lane_tiles(s_diag):
        m_run = jnp.maximum(m_run, part)
    m = jnp.max(m_run, axis=-1, keepdims=True)

    def accumulate(p, vb, l_run, acc):
        for part in lane_tiles(p):
            l_run = l_run + part
        return l_run, acc + jnp.dot(p.astype(BF16), vb, preferred_element_type=F32)

    def sweep_acc(j, carry):
        vb = v_ref[0, pl.ds(pl.multiple_of(j * tq, tq), tq), :]
        return accumulate(jnp.exp2(scores(j) - m), vb, *carry)

    carry = lax.fori_loop(0, qi, sweep_acc, (jnp.zeros((rows, LANES), F32), jnp.zeros((rows, HEAD), F32)))
    vb_diag = v_ref[0, pl.ds(pl.multiple_of(qi * tq, tq), tq), :]
    l_run, acc = accumulate(jnp.exp2(s_diag - m), vb_diag, *carry)
    l = jnp.sum(l_run, axis=-1, keepdims=True)

    lam =(jnp.exp(jnp.sum(lq1_ref[0] * lk1_ref[0], axis=-1, keepdims=True))
           - jnp.exp(jnp.sum(lq2_ref[0] * lk2_ref[0], axis=-1, keepdims=True)) + lambda_init)
    o = acc / l
    od = o[:tq] - lam * o[tq:]
    y = od * lax.rsqrt(jnp.mean(od * od, axis=-1, keepdims=True) + RMS_EPS) * g_ref[0]
    o_ref[0] = (y * (1.0 - lambda_init)).astype(o_ref.dtype)


def _diff_attention(z, lq1, lk1, lq2, lk2, subln_g, layer, lambda_init):
    b, t, _ = z.shape
    heads = GROUP // HEAD
    tq = min(t, 512)
    q0, k0, v0 = 1 * heads, 2 * heads, 3 * heads

    def vec(n):
        return pl.BlockSpec((1, 1, n), lambda bi, h, i: (layer, 0, 0))

    return pl.pallas_call(
        functools.partial(_diff_kernel, tq=tq, lambda_init=lambda_init),
        grid=(b, heads, t // tq),
        in_specs=[
            pl.BlockSpec((1, tq, HEAD), lambda bi, h, i: (bi, i, q0 + h)),
            pl.BlockSpec((1, t, HEAD), lambda bi, h, i: (bi, 0, k0 + h)),
            pl.BlockSpec((1, t, HEAD), lambda bi, h, i: (bi, 0, v0 + h)),
            vec(DIFF_QK), vec(DIFF_QK), vec(DIFF_QK), vec(DIFF_QK), vec(HEAD),
        ],
        out_specs=pl.BlockSpec((1, tq, HEAD), lambda bi, h, i: (bi, i, h)),
        out_shape=jax.ShapeDtypeStruct((b, t, GROUP), BF16),
        compiler_params=_params("arbitrary", "arbitrary", "arbitrary"),
        name="diff_attention",
    )(z, z, z, lq1, lk1, lq2, lk2, subln_g)


def _dilated_kernel(q_ref, k_ref, v_ref, o_ref, qf, kf, vf, o0, o1, o2, e0, e1, e2, *, t, group):
    qf[...] = q_ref[0].astype(F32)
    kf[...] = k_ref[0].astype(F32)
    vf[...] = v_ref[0].astype(F32)
    band = DIL_BAND
    a_idx = lax.broadcasted_iota(I32, (band, 2 * band), 0)
    b_idx = lax.broadcasted_iota(I32, (band, 2 * band), 1)
    in_band = (b_idx >= a_idx) & (b_idx <= a_idx + band)
    outs = ((o0, e0), (o1, e1), (o2, e2))

    for (window, dil), (o_scr, e_scr) in zip(DILATED_PAIRS, outs):
        nb = t // (dil * band)

        def one_block(it, dil=dil, nb=nb, o_scr=o_scr, e_scr=e_scr):
            r = it // nb
            n = it % nb
            start = r + n * (band * dil)
            prev = r + jnp.maximum(n - 1, 0) * (band * dil)

            def rows(ref, s0):
                return ref[pl.ds(s0, band, stride=dil), :] if dil > 1 else ref[pl.ds(s0, band), :]

            qb = rows(qf, start).astype(BF16)
            kk = jnp.concatenate([rows(kf, prev), rows(kf, start)], axis=0).astype(BF16)
            vv = jnp.concatenate([rows(vf, prev), rows(vf, start)], axis=0).astype(BF16)
            s = lax.dot_general(qb, kk, (((1,), (1,)), ((), ())), preferred_element_type=F32)
            valid = in_band & (b_idx >= jnp.where(n > 0, 0, band))
            s = jnp.where(valid, s, NEG_INF)
            m = jnp.max(s, axis=-1, keepdims=True)
            p = jnp.exp2(s - m)
            l = jnp.sum(p, axis=-1, keepdims=True)
            o = jnp.dot(p.astype(BF16), vv, preferred_element_type=F32) / l
            idx = pl.ds(start, band, stride=dil) if dil > 1 else pl.ds(start, band)
            o_scr[idx, :] = o
            e_scr[idx, :] = jnp.broadcast_to(m + jnp.log2(l), (band, HEAD))

        def blocks(g, carry, one_block=one_block):
            for u in range(group):
                one_block(g * group + u)
            return carry

        lax.fori_loop(0, dil * nb // group, blocks, 0)

    top = jnp.maximum(jnp.maximum(e0[...], e1[...]), e2[...])
    w0 = jnp.exp2(e0[...] - top)
    w1 = jnp.exp2(e1[...] - top)
    w2 = jnp.exp2(e2[...] - top)
    mix = (w0 * o0[...] + w1 * o1[...] + w2 * o2[...]) / (w0 + w1 + w2)
    o_ref[0] = mix.astype(o_ref.dtype)


def _dilated_attention(z):
    b, t, _ = z.shape
    heads = GROUP // HEAD
    q0, k0, v0 = 4 * heads, 5 * heads, 6 * heads
    group = 4
    assert all(w // dl == DIL_BAND and t % w == 0 for w, dl in DILATED_PAIRS)
    assert (t // DIL_BAND) % group == 0

    def col(c0):
        return pl.BlockSpec((1, t, HEAD), lambda bi, h: (bi, 0, c0 + h))

    return pl.pallas_call(
        functools.partial(_dilated_kernel, t=t, group=group),
        grid=(b, heads),
        in_specs=[col(q0), col(k0), col(v0)],
        out_specs=pl.BlockSpec((1, t, HEAD), lambda bi, h: (bi, 0, h)),
        out_shape=jax.ShapeDtypeStruct((b, t, GROUP), BF16),
        scratch_shapes=[pltpu.VMEM((t, HEAD), F32)] * 9,
        compiler_params=_params("arbitrary", "arbitrary"),
        name="dilated_attention",
    )(z, z, z)


def _pack_rows(h, out_ref, row0=0):
    m, half = h.shape[0], h.shape[1] // 2
    bits = pltpu.bitcast(h.astype(BF16).astype(F32), U32)
    word = (bits[:, half:] & jnp.uint32(0xFFFF0000)) | (bits[:, :half] >> 16)
    n_slab = half // LANES
    for s in range(n_slab):
        out_ref[pl.ds(row0 * n_slab + s, m, stride=n_slab), :] = word[:, s * LANES:(s + 1) * LANES]


def _unpack_rows(word):
    return pltpu.bitcast(word << 16, F32), pltpu.bitcast(word & jnp.uint32(0xFFFF0000), F32)


def _outproj_router_kernel(yp_ref, yd_ref, yc_ref, yv_ref, wo_ref, x_ref, g1_ref, n2_ref, sc_ref, sh_ref,
                           rw_ref, rb_ref,
                           x1_ref, h2_ref, idx_ref, rank_ref, wts_ref, cnt_ref, carry_scr, *, tm):
    first = (pl.program_id(0) == 0) & (pl.program_id(1) == 0)

    @pl.when(first)
    def _():
        carry_scr[...] = jnp.zeros_like(carry_scr)

    mix = jnp.dot(yp_ref[0], wo_ref[0 * GROUP:1 * GROUP, :], preferred_element_type=F32)
    mix = mix + jnp.dot(yd_ref[0], wo_ref[1 * GROUP:2 * GROUP, :], preferred_element_type=F32)
    mix = mix + jnp.dot(yc_ref[0], wo_ref[2 * GROUP:3 * GROUP, :], preferred_element_type=F32)
    mix = mix + jnp.dot(yv_ref[0], wo_ref[3 * GROUP:4 * GROUP, :], preferred_element_type=F32)
    x1 = x_ref[0] + g1_ref[0] * mix
    x1_ref[0] = x1

    y = x1 * lax.rsqrt(jnp.mean(x1 * x1, axis=-1, keepdims=True) + RMS_EPS) * n2_ref[0]
    h2 = y * (1.0 + sc_ref[0]) + sh_ref[0]
    _pack_rows(h2, h2_ref)

    logits = lax.dot_general(rw_ref[0], h2, (((1,), (1,)), ((), ())), precision=HIGHEST,
                             preferred_element_type=F32) + rb_ref[0]
    e_idx = lax.broadcasted_iota(I32, (N_EXPERTS, tm), 0)
    work = logits
    vals, sels, hots = [], [], []
    for _ in range(TOP_K):
        mx = jnp.max(work, axis=0, keepdims=True)
        sel = jnp.min(jnp.where(work == mx, e_idx, N_EXPERTS), axis=0, keepdims=True)
        hot = e_idx == sel
        vals.append(mx)
        sels.append(sel)
        hots.append(hot)
        work = jnp.where(hot, -jnp.inf, work)
    exps = [jnp.exp(v - vals[0]) for v in vals]
    denom = exps[0] + exps[1] + exps[2] + exps[3]

    chosen = jnp.zeros((N_EXPERTS, tm), F32)
    for hot in hots:
        chosen = chosen + hot.astype(F32)
    s_idx = lax.broadcasted_iota(I32, (tm, tm), 0)
    t_idx = lax.broadcasted_iota(I32, (tm, tm), 1)
    upper = (s_idx < t_idx).astype(BF16)
    before = jnp.dot(chosen.astype(BF16), upper, preferred_element_type=F32) + carry_scr[:, 0:1]
    for k in range(TOP_K):
        idx_ref[k:k + 1, :] = sels[k]
        rank_ref[k:k + 1, :] = jnp.sum(jnp.where(hots[k], before, 0.0), axis=0, keepdims=True).astype(I32)
        wts_ref[k:k + 1, :] = exps[k] / denom
    carry_scr[...] = carry_scr[...] + jnp.sum(chosen, axis=1, keepdims=True)
    cnt_ref[...] = carry_scr[...]


def _outproj_router(ys, w_out_bf16, x, gate1, norm2_g, scale2, shift2, router_wt, router_b, layer):
    b, t, d = x.shape
    tm = min(t, 512)
    n = b * t
    nt = t // tm
    slab = d // 2 // LANES

    def ytile():
        return pl.BlockSpec((1, tm, GROUP), lambda bi, i: (bi, i, 0))

    def bvec():
        return pl.BlockSpec((1, 1, d), lambda bi, i: (bi, 0, 0))

    tok = pl.BlockSpec((TOP_K, tm), lambda bi, i: (0, bi * nt + i))
    outs = pl.pallas_call(
        functools.partial(_outproj_router_kernel, tm=tm),
        grid=(b, nt),
        in_specs=[
            ytile(), ytile(), ytile(), ytile(),
            pl.BlockSpec((None, 4 * GROUP, d), lambda bi, i: (layer, 0, 0)),
            pl.BlockSpec((1, tm, d), lambda bi, i: (bi, i, 0)),
            bvec(),
            pl.BlockSpec((1, 1, d), lambda bi, i: (layer, 0, 0)),
            bvec(), bvec(),
            pl.BlockSpec((1, N_EXPERTS, d), lambda bi, i: (layer, 0, 0)),
            pl.BlockSpec((1, N_EXPERTS, 1), lambda bi, i: (layer, 0, 0)),
        ],
        out_specs=[
            pl.BlockSpec((1, tm, d), lambda bi, i: (bi, i, 0)),
            pl.BlockSpec((tm * slab, LANES), lambda bi, i: (bi * nt + i, 0)),
            tok, tok, tok,
            pl.BlockSpec((N_EXPERTS, LANES), lambda bi, i: (0, 0)),
        ],
        out_shape=[
            jax.ShapeDtypeStruct((b, t, d), F32),
            jax.ShapeDtypeStruct((n * slab, LANES), U32),
            jax.ShapeDtypeStruct((TOP_K, n), I32),
            jax.ShapeDtypeStruct((TOP_K, n), I32),
            jax.ShapeDtypeStruct((TOP_K, n), F32),
            jax.ShapeDtypeStruct((N_EXPERTS, LANES), F32),
        ],
        scratch_shapes=[pltpu.VMEM((N_EXPERTS, LANES), F32)],
        compiler_params=_params("arbitrary", "arbitrary"),
        name="outproj_router",
    )(*ys, w_out_bf16, x, gate1, norm2_g, scale2, shift2, router_wt, router_b)
    return outs


def _scatter_kernel(zs_ref, pos_ref, h_ref, xs_hbm, zbuf, zsem, sem, *, tm, tm_e):
    zrows = zbuf.shape[0]

    @pl.when(pl.program_id(0) == 0)
    def _():
        zbuf[...] = jnp.zeros_like(zbuf)

        def zero(e, c):
            @pl.when(zs_ref[e] >= 0)
            def _():
                for q in range(tm_e // zrows):
                    pltpu.make_async_copy(zbuf, xs_hbm.at[pl.ds(zs_ref[e] + q * zrows, zrows)], zsem).start()
            return c

        lax.fori_loop(0, N_EXPERTS, zero, 0)

        def zero_done(e, c):
            @pl.when(zs_ref[e] >= 0)
            def _():
                for q in range(tm_e // zrows):
                    pltpu.make_async_copy(zbuf, xs_hbm.at[pl.ds(0, zrows)], zsem).wait()
            return c

        lax.fori_loop(0, N_EXPERTS, zero_done, 0)

    def issue(g, c):
        for u in range(ISSUE_UNROLL):
            tt = g * ISSUE_UNROLL + u
            for k in range(TOP_K):
                pltpu.make_async_copy(h_ref.at[tt], xs_hbm.at[pos_ref[k, tt]], sem).start()
        return c

    lax.fori_loop(0, tm // ISSUE_UNROLL, issue, 0)
    for k in range(TOP_K):
        pltpu.make_async_copy(h_ref, xs_hbm.at[pl.ds(0, tm)], sem).wait()


def _scatter_rows(zero_start, pos, h_rows, m_pad, tm_e):
    n, slab, _ = h_rows.shape
    tm = min(n, 256)
    zrows = min(tm_e, 256)
    grid_spec = pltpu.PrefetchScalarGridSpec(
        num_scalar_prefetch=1,
        grid=(n // tm,),
        in_specs=[
            pl.BlockSpec((TOP_K, tm), lambda i, zs: (0, i), memory_space=pltpu.SMEM),
            pl.BlockSpec((tm, slab, LANES), lambda i, zs: (i, 0, 0)),
        ],
        out_specs=pl.BlockSpec(memory_space=pl.ANY),
        scratch_shapes=[pltpu.VMEM((zrows, slab, LANES), U32), pltpu.SemaphoreType.DMA(()),
                        pltpu.SemaphoreType.DMA(())],
    )
    return pl.pallas_call(
        functools.partial(_scatter_kernel, tm=tm, tm_e=tm_e),
        grid_spec=grid_spec,
        out_shape=jax.ShapeDtypeStruct((m_pad, slab, LANES), U32),
        compiler_params=_params("arbitrary"),
        name="scatter_rows",
    )(zero_start, pos, h_rows)


def _deinterleave(hh):
    m, width = hh.shape
    lane = lax.broadcasted_iota(I32, (m, LANES), 1)
    low = lane < LANES // 2
    evens_then_odds = jnp.where(low, 2 * lane, 2 * lane - (LANES - 1))
    parts = [jnp.take_along_axis(hh[:, b * LANES:(b + 1) * LANES], evens_then_odds, axis=1)
             for b in range(width // LANES)]
    gates, lins = [], []
    for b in range(0, len(parts), 2):
        first, second = parts[b], parts[b + 1]
        gates.append(jnp.where(low, first, pltpu.roll(second, LANES // 2, 1)))
        lins.append(jnp.where(low, pltpu.roll(first, LANES // 2, 1), second))
    return jnp.concatenate(gates, axis=1), jnp.concatenate(lins, axis=1)


def _expert_kernel(te_ref, tv_ref, nu_ref, xs_ref, w1_ref, b1_ref, w2_ref, b2_ref, ys_ref,
                   x_scr, acc_scr, w1b, w2b, *, tm, sub, nc):
    i = pl.program_id(0)
    c = pl.program_id(1)
    d = x_scr.shape[1]
    n_slab = d // 2 // LANES
    live = i < nu_ref[0]

    @pl.when(live)
    def _():
        @pl.when(c == 0)
        def _():
            for s in range(n_slab):
                lo, hi = _unpack_rows(xs_ref[pl.ds(s, tm, stride=n_slab), :])
                x_scr[:, s * LANES:(s + 1) * LANES] = lo.astype(BF16)
                x_scr[:, d // 2 + s * LANES:d // 2 + (s + 1) * LANES] = hi.astype(BF16)
            acc_scr[...] = jnp.zeros_like(acc_scr) + b2_ref[...]

        def ffn_rows(sb, w1c, w2c):
            rows = slice(sb * sub, (sb + 1) * sub)
            hh = jnp.dot(x_scr[rows, :], w1c, preferred_element_type=F32) + b1_ref[...]
            g, lin = _deinterleave(hh)
            g = jnp.minimum(g, SWIGLU_LIMIT)
            lin = jnp.clip(lin, -SWIGLU_LIMIT, SWIGLU_LIMIT)
            act = g * _sigmoid(SWIGLU_ALPHA * g) * (lin + 1.0)
            acc_scr[rows, :] += jnp.dot(act.astype(BF16), w2c, preferred_element_type=F32)

        w1c = w1_ref[...].astype(BF16)
        w2c = w2_ref[...].astype(BF16)
        w1b[...] = w1c
        w2b[...] = w2c
        ffn_rows(0, w1c, w2c)
        for sb in range(1, tm // sub):
            @pl.when(sb * sub < tv_ref[i])
            def _(sb=sb):
                ffn_rows(sb, w1b[...], w2b[...])

        @pl.when(c == nc - 1)
        def _():
            for sb in range(tm // sub):
                _pack_rows(acc_scr[sb * sub:(sb + 1) * sub, :], ys_ref, row0=sb * sub)

    @pl.when(jnp.logical_not(live) & (c == 0))
    def _():
        ys_ref[...] = jnp.zeros_like(ys_ref)


def _experts(tile_expert, tile_valid, n_used, xs2d, w1, b1, w2, b2, layer, tm, sub, m_pad):
    _, n_exp, d, h2 = w1.shape
    hid = h2 // 2
    tc = min(hid, 512)
    nc = hid // tc
    n_tiles = m_pad // tm
    slab = d // 2 // LANES
    once = pl.Buffered(1)

    def live(i, nu):
        return jnp.minimum(i, nu[0] - 1)

    def chunk(i, c, nu):
        return jnp.where(i < nu[0], c, nc - 1)

    grid_spec = pltpu.PrefetchScalarGridSpec(
        num_scalar_prefetch=3,
        grid=(n_tiles, nc),
        in_specs=[
            pl.BlockSpec((tm * slab, LANES), lambda i, c, te, tv, nu: (live(i, nu), 0), pipeline_mode=once),
            pl.BlockSpec((None, None, d, 2 * tc), lambda i, c, te, tv, nu: (layer, te[i], 0, chunk(i, c, nu))),
            pl.BlockSpec((None, None, 1, 2 * tc), lambda i, c, te, tv, nu: (layer, te[i], 0, chunk(i, c, nu))),
            pl.BlockSpec((None, None, tc, d), lambda i, c, te, tv, nu: (layer, te[i], chunk(i, c, nu), 0)),
            pl.BlockSpec((None, None, 1, d), lambda i, c, te, tv, nu: (layer, te[i], 0, 0)),
        ],
        out_specs=pl.BlockSpec((tm * slab, LANES), lambda i, c, te, tv, nu: (i, 0), pipeline_mode=once),
        scratch_shapes=[pltpu.VMEM((tm, d), BF16), pltpu.VMEM((tm, d), F32),
                        pltpu.VMEM((d, 2 * tc), BF16), pltpu.VMEM((tc, d), BF16)],
    )
    return pl.pallas_call(
        functools.partial(_expert_kernel, tm=tm, sub=sub, nc=nc),
        grid_spec=grid_spec,
        out_shape=jax.ShapeDtypeStruct((m_pad * slab, LANES), U32),
        compiler_params=_params("arbitrary", "arbitrary"),
        name="experts",
    )(tile_expert, tile_valid, n_used, xs2d, w1, b1, w2, b2)


def _combine_kernel(pos_ref, pos_next_ref, wts_ref, x_ref, g2_ref, fg_ref, ys_hbm, ys_flat_hbm, o_ref, buf, sem,
                    *, tm, final, n_steps):
    i = pl.program_id(0)
    n_slab = x_ref.shape[1] // 2 // LANES
    slot_rows = TOP_K * tm * n_slab
    slot = i % 2
    base = pl.multiple_of(slot * slot_rows, slot_rows)

    def issue_block(p_ref, to_slot):
        to_base = to_slot * slot_rows

        def issue(g, c):
            for u in range(ISSUE_UNROLL):
                tt = g * ISSUE_UNROLL + u
                for k in range(TOP_K):
                    row0 = pl.multiple_of(to_base + (k * tm + tt) * n_slab, n_slab)
                    pltpu.make_async_copy(ys_hbm.at[p_ref[k, tt]], buf.at[pl.ds(row0, n_slab)], sem.at[to_slot]).start()
            return c

        lax.fori_loop(0, tm // ISSUE_UNROLL, issue, 0)

    @pl.when(i == 0)
    def _():
        issue_block(pos_ref, 0)

    @pl.when(i + 1 < n_steps)
    def _():
        issue_block(pos_next_ref, 1 - slot)

    pltpu.make_async_copy(ys_flat_hbm.at[pl.ds(0, slot_rows)], buf.at[pl.ds(base, slot_rows)], sem.at[slot]).wait()

    w_sq = jnp.concatenate([wts_ref[...], jnp.zeros((tm - TOP_K, tm), F32)], axis=0)
    w_t = w_sq.T
    w_k = [jnp.broadcast_to(w_t[:, k:k + 1], (tm, LANES)) for k in range(TOP_K)]
    lows, highs = [], []
    for s in range(n_slab):
        acc_lo = jnp.zeros((tm, LANES), F32)
        acc_hi = jnp.zeros((tm, LANES), F32)
        for k in range(TOP_K):
            lo, hi = _unpack_rows(buf[pl.ds(base + k * tm * n_slab + s, tm, stride=n_slab), :])
            acc_lo = acc_lo + w_k[k] * lo
            acc_hi = acc_hi + w_k[k] * hi
        lows.append(acc_lo)
        highs.append(acc_hi)
    moe = jnp.concatenate(lows + highs, axis=1)
    x2 = x_ref[...] + g2_ref[0] * moe
    if final:
        x2 = x2 * lax.rsqrt(jnp.mean(x2 * x2, axis=-1, keepdims=True) + RMS_EPS) * fg_ref[...]
    o_ref[...] = x2


def _combine(pos, wts, x1, gate2, final_g, ys_rows, tokens_per_batch, final):
    n, d = x1.shape
    tm = 256
    steps_per_batch = tokens_per_batch // tm
    n_steps = n // tm
    m_pad, slab, _ = ys_rows.shape
    return pl.pallas_call(
        functools.partial(_combine_kernel, tm=tm, final=final, n_steps=n_steps),
        grid=(n_steps,),
        in_specs=[
            pl.BlockSpec((TOP_K, tm), lambda i: (0, i), memory_space=pltpu.SMEM),
            pl.BlockSpec((TOP_K, tm), lambda i: (0, jnp.minimum(i + 1, n_steps - 1)), memory_space=pltpu.SMEM),
            pl.BlockSpec((TOP_K, tm), lambda i: (0, i)),
            pl.BlockSpec((tm, d), lambda i: (i, 0)),
            pl.BlockSpec((1, 1, d), lambda i: (i // steps_per_batch, 0, 0)),
            pl.BlockSpec((1, d), lambda i: (0, 0)),
            pl.BlockSpec(memory_space=pl.ANY),
            pl.BlockSpec(memory_space=pl.ANY),
        ],
        out_specs=pl.BlockSpec((tm, d), lambda i: (i, 0)),
        out_shape=jax.ShapeDtypeStruct((n, d), F32),
        scratch_shapes=[pltpu.VMEM((2 * TOP_K * tm * slab, LANES), U32), pltpu.SemaphoreType.DMA((2,))],
        compiler_params=_params("arbitrary"),
        name="combine",
    )(pos, pos, wts, x1, gate2, final_g, ys_rows, ys_rows.reshape(m_pad * slab, LANES))


def _routing_plan(idx, rank, counts, tm_e, n_tiles):
    experts = jnp.arange(N_EXPERTS, dtype=I32)
    cnt = counts[:, 0].astype(I32)
    tiles = (cnt + tm_e - 1) // tm_e
    tile_end = jnp.cumsum(tiles)
    tile_start = tile_end - tiles
    offsets = tile_start * tm_e
    pos = rank + jnp.sum(jnp.where(idx[..., None] == experts, offsets, 0), axis=-1)
    n_used = tile_end[-1]
    tile_ids = jnp.arange(n_tiles, dtype=I32)
    te = jnp.sum((tile_ids[:, None] >= tile_end[None, :]).astype(I32), axis=1)
    te_last = jnp.sum((n_used - 1 >= tile_end).astype(I32))
    te = jnp.where(tile_ids < n_used, te, te_last).astype(I32)
    mine = te[:, None] == experts[None, :]
    rows_left = jnp.sum(jnp.where(mine, cnt[None, :] - (tile_ids[:, None] - tile_start[None, :]) * tm_e, 0), axis=1)
    tile_valid = jnp.where(tile_ids < n_used, jnp.clip(rows_left, 0, tm_e), 0).astype(I32)
    zero_start = jnp.where(tiles > 0, (tile_end - 1) * tm_e, -1).astype(I32)
    return pos.astype(I32), te, tile_valid, n_used.reshape(1).astype(I32), zero_start


def kernel(x, c, positions, mod_w, mod_b, norm1_g, norm2_g, w_in, pool_w, pool_scale, diff_lq1, diff_lk1,
           diff_lq2, diff_lk2, diff_subln_g, conv_dw_w, conv_dw_b, conv_ln_g, conv_ln_b, conv_pw_w, conv_pw_b,
           w_out, router_w, router_b, exp_w1, exp_b1, exp_w2, exp_b2, final_g):
    b, t, d = x.shape
    depth = mod_w.shape[0]
    n = b * t
    tm_e = 1024
    sub_e = 512
    m_pad = n * TOP_K + N_EXPERTS * tm_e
    n_tiles = m_pad // tm_e
    slab = d // 2 // LANES

    def row3(a):
        return a.reshape(a.shape[0], 1, a.shape[1])

    mod = _modulation(c, mod_w, mod_b)
    tables = _rope_tables(positions)
    w_in_b = w_in.astype(BF16)
    w_out_b = w_out.astype(BF16)
    pool_w_b = pool_w.astype(BF16)
    pw_w_b = conv_pw_w.astype(BF16)
    router_wt = jnp.swapaxes(router_w, 1, 2)
    router_b3 = router_b.reshape(depth, N_EXPERTS, 1)
    b1r = exp_b1.reshape(depth, N_EXPERTS, 1, exp_b1.shape[-1])
    b2r = exp_b2.reshape(depth, N_EXPERTS, 1, d)
    final_g2 = final_g.reshape(1, d)

    for l in range(depth):
        lambda_init = 0.8 - 0.6 * math.exp(-0.3 * l)
        sh1, sc1, g1, sh2, sc2, g2 = [m.reshape(b, 1, d) for m in jnp.split(mod[l], 6, axis=-1)]
        z = _inproj(x, row3(norm1_g), sc1, sh1, w_in_b, tables, l)
        y_pool, y_conv = _poolconv(z, pool_w_b, row3(pool_scale), conv_dw_w, row3(conv_dw_b), row3(conv_ln_g),
                                   row3(conv_ln_b), pw_w_b, row3(conv_pw_b), l)
        y_diff = _diff_attention(z, row3(diff_lq1), row3(diff_lk1), row3(diff_lq2), row3(diff_lk2),
                                 row3(diff_subln_g), l, lambda_init)
        y_dil = _dilated_attention(z)
        x1, h2, idx, rank, wts, counts = _outproj_router(
            (y_pool, y_diff, y_dil, y_conv), w_out_b, x, g1, row3(norm2_g), sc2, sh2, router_wt, router_b3, l)
        pos, tile_expert, tile_valid, n_used, zero_start = _routing_plan(idx, rank, counts, tm_e, n_tiles)
        xs = _scatter_rows(zero_start, pos, h2.reshape(n, slab, LANES), m_pad, tm_e)
        ys = _experts(tile_expert, tile_valid, n_used, xs.reshape(m_pad * slab, LANES), exp_w1, b1r, exp_w2, b2r,
                      l, tm_e, sub_e, m_pad)
        x = _combine(pos, wts, x1.reshape(n, d), g2, final_g2, ys.reshape(m_pad, slab, LANES), t,
                     final=(l == depth - 1)).reshape(b, t, d)
    return x
```

```python
import functools
import math

import numpy as np
import jax
import jax.numpy as jnp
from jax import lax
from jax.experimental import pallas as pl
from jax.experimental.pallas import tpu as pltpu

F32 = jnp.float32
BF16 = jnp.bfloat16
I32 = jnp.int32
U32 = jnp.uint32
HIGHEST = lax.Precision.HIGHEST

LANES = 128
VMEM_LIMIT_BYTES = 60 * 1024 * 1024

POOL_WINDOWS = (2, 4, 8, 16)
DILATED_PAIRS = ((128, 1), (512, 4), (2048, 16))
DIL_BAND = 128
CONV_WIDTH = 31
HALO = 32
ISSUE_UNROLL = 8
N_EXPERTS = 32
TOP_K = 4
SWIGLU_ALPHA = 1.702
SWIGLU_LIMIT = 7.0
ROPE_THETA = 500000.0
RMS_EPS = 1e-6
LN_EPS = 1e-5
NEG_INF = -1e30
GROUP = 512
HEAD = 128
DIFF_QK = 64
LOG2E = math.log2(math.e)


def _params(*semantics):
    return pltpu.CompilerParams(dimension_semantics=semantics, vmem_limit_bytes=VMEM_LIMIT_BYTES)


def _sigmoid(x):
    return 1.0 / (1.0 + jnp.exp(-x))


def _mod_kernel(c_ref, w_ref, b_ref, o_ref):
    c = c_ref[...]
    ca = c * _sigmoid(c)
    o_ref[0] = jnp.dot(ca, w_ref[0], precision=HIGHEST, preferred_element_type=F32) + b_ref[0]


def _modulation(c, mod_w, mod_b):
    depth, d, n = mod_w.shape
    b = c.shape[0]
    rows = 8
    c_pad = jnp.zeros((rows, d), F32).at[:b].set(c)
    tn = 512
    out = pl.pallas_call(
        _mod_kernel,
        grid=(depth, n // tn),
        in_specs=[
            pl.BlockSpec((rows, d), lambda l, j: (0, 0)),
            pl.BlockSpec((1, d, tn), lambda l, j: (l, 0, j)),
            pl.BlockSpec((1, 1, tn), lambda l, j: (l, 0, j)),
        ],
        out_specs=pl.BlockSpec((1, rows, tn), lambda l, j: (l, 0, j)),
        out_shape=jax.ShapeDtypeStruct((depth, rows, n), F32),
        compiler_params=_params("arbitrary", "arbitrary"),
        name="modulation",
    )(c_pad, mod_w, mod_b.reshape(depth, 1, n))
    return out[:, :b]


def _rope_table_kernel(pos_ref, invd_ref, invc_ref, cd_ref, sd_ref, cc_ref, sc_ref):
    p = pos_ref[0]
    lane = lax.broadcasted_iota(I32, (1, LANES), 1)
    for inv_ref, c_ref, s_ref, hd in ((invd_ref, cd_ref, sd_ref, DIFF_QK), (invc_ref, cc_ref, sc_ref, HEAD)):
        half = hd // 8
        lm = lane % hd
        ang = p * inv_ref[...]
        c_ref[0] = jnp.cos(ang)
        s_ref[0] = jnp.where(lm < half, -jnp.sin(ang), jnp.sin(ang))


def _lane_inv_freq(hd):
    half = hd // 8
    inv = ROPE_THETA ** (-jnp.arange(half, dtype=F32) / half)
    lm = np.arange(LANES) % hd
    rotated = jnp.asarray(lm < 2 * half)
    return jnp.where(rotated, inv[lm % half], 0.0).reshape(1, LANES).astype(F32)


def _rope_tables(positions):
    b, t = positions.shape
    tm = min(t, 1024)
    pos = positions.astype(F32).reshape(b, t, 1)
    spec_t = pl.BlockSpec((1, tm, LANES), lambda bi, i: (bi, i, 0))
    spec_inv = pl.BlockSpec((1, LANES), lambda bi, i: (0, 0))
    shp = jax.ShapeDtypeStruct((b, t, LANES), F32)
    return pl.pallas_call(
        _rope_table_kernel,
        grid=(b, t // tm),
        in_specs=[pl.BlockSpec((1, tm, 1), lambda bi, i: (bi, i, 0)), spec_inv, spec_inv],
        out_specs=[spec_t] * 4,
        out_shape=[shp] * 4,
        compiler_params=_params("arbitrary", "arbitrary"),
        name="rope_tables",
    )(pos, _lane_inv_freq(DIFF_QK), _lane_inv_freq(HEAD))


def _rope_apply(z, cos, sin, hd):
    half = hd // 8
    lane = lax.broadcasted_iota(I32, (1, LANES), 1)
    first = (lane % hd) < half
    outs = []
    for cb in range(z.shape[1] // LANES):
        zc = z[:, cb * LANES:(cb + 1) * LANES]
        partner = jnp.where(first, pltpu.roll(zc, LANES - half, 1), pltpu.roll(zc, half, 1))
        outs.append(zc * cos + partner * sin)
    return jnp.concatenate(outs, axis=1)


def _inproj_kernel(x_ref, g_ref, sc_ref, sh_ref, w_ref, cd_ref, sd_ref, cc_ref, sc2_ref, z_ref, h_scr):
    j = pl.program_id(2)

    @pl.when(j == 0)
    def _():
        x = x_ref[0]
        y = x * lax.rsqrt(jnp.mean(x * x, axis=-1, keepdims=True) + RMS_EPS) * g_ref[0]
        h_scr[...] = (y * (1.0 + sc_ref[0]) + sh_ref[0]).astype(BF16)

    z = jnp.dot(h_scr[...], w_ref[...], preferred_element_type=F32)

    @pl.when((j == 1) | (j == 2))
    def _():
        r = _rope_apply(z, cd_ref[0], sd_ref[0], DIFF_QK)
        r = r * jnp.where(j == 1, DIFF_QK ** -0.5 * LOG2E, 1.0)
        z_ref[0] = r.astype(z_ref.dtype)

    @pl.when((j == 4) | (j == 5))
    def _():
        r = _rope_apply(z, cc_ref[0], sc2_ref[0], HEAD)
        r = r * jnp.where(j == 4, HEAD ** -0.5 * LOG2E, 1.0)
        z_ref[0] = r.astype(z_ref.dtype)

    @pl.when((j != 1) & (j != 2) & (j != 4) & (j != 5))
    def _():
        z_ref[0] = z.astype(z_ref.dtype)


def _inproj(x, norm_g, scale, shift, w_in_bf16, tables, layer):
    b, t, d = x.shape
    n = w_in_bf16.shape[-1]
    tm = min(t, 1024)
    tn = GROUP
    cd, sd, cc, sc = tables
    spec_tab = pl.BlockSpec((1, tm, LANES), lambda bi, i, j: (bi, i, 0))
    spec_vec = pl.BlockSpec((1, 1, d), lambda bi, i, j: (bi, 0, 0))
    return pl.pallas_call(
        _inproj_kernel,
        grid=(b, t // tm, n // tn),
        in_specs=[
            pl.BlockSpec((1, tm, d), lambda bi, i, j: (bi, i, 0)),
            pl.BlockSpec((1, 1, d), lambda bi, i, j: (layer, 0, 0)),
            spec_vec, spec_vec,
            pl.BlockSpec((None, d, tn), lambda bi, i, j: (layer, 0, j)),
            spec_tab, spec_tab, spec_tab, spec_tab,
        ],
        out_specs=pl.BlockSpec((1, tm, tn), lambda bi, i, j: (bi, i, j)),
        out_shape=jax.ShapeDtypeStruct((b, t, n), BF16),
        scratch_shapes=[pltpu.VMEM((tm, d), BF16)],
        compiler_params=_params("arbitrary", "arbitrary", "arbitrary"),
        name="inproj",
    )(x, norm_g, scale, shift, w_in_bf16, cd, sd, cc, sc)


def _poolconv_kernel(zp_ref, zph_ref, za_ref, zah_ref, zg_ref, zgh_ref,
                     pw_ref, ps_ref, dww_ref, dwb_ref, lng_ref, lnb_ref, pww_ref, pwb_ref,
                     yp_ref, yc_ref, xp_scr, u_scr, *, tm):
    i = pl.program_id(1)
    keep = jnp.where(i == 0, 0.0, 1.0)

    xp_scr[0:HALO, :] = zph_ref[0].astype(F32) * keep
    xp_scr[HALO:, :] = zp_ref[0].astype(F32)
    t_glob = i * tm + lax.broadcasted_iota(I32, (tm, 1), 0)
    for g, w in enumerate(POOL_WINDOWS):
        cols = slice(g * LANES, (g + 1) * LANES)
        xg = xp_scr[HALO:HALO + tm, cols]
        acc = xg
        for k in range(1, w):
            acc = acc + xp_scr[HALO - k:HALO - k + tm, cols]
        cnt = jnp.minimum(t_glob + 1, w).astype(F32)
        pooled = acc / cnt - xg
        yg = jnp.dot(pooled.astype(BF16), pw_ref[0, g], preferred_element_type=F32)
        yp_ref[0, :, cols] = (yg * ps_ref[0, :, cols]).astype(yp_ref.dtype)

    ah = zah_ref[0].astype(F32)
    gh = zgh_ref[0].astype(F32)
    u_scr[0:HALO, :] = ah * _sigmoid(gh) * keep
    a = za_ref[0].astype(F32)
    gg = zg_ref[0].astype(F32)
    u_scr[HALO:, :] = a * _sigmoid(gg)
    acc = jnp.zeros((tm, GROUP), F32) + dwb_ref[0]
    base = HALO - (CONV_WIDTH - 1)
    for k in range(CONV_WIDTH):
        acc = acc + u_scr[base + k:base + k + tm, :] * dww_ref[0, k:k + 1, :]
    mu = jnp.mean(acc, axis=-1, keepdims=True)
    cen = acc - mu
    var = jnp.mean(cen * cen, axis=-1, keepdims=True)
    v = cen * lax.rsqrt(var + LN_EPS) * lng_ref[0] + lnb_ref[0]
    v = v * _sigmoid(v)
    y = jnp.dot(v.astype(BF16), pww_ref[0], preferred_element_type=F32) + pwb_ref[0]
    yc_ref[0] = y.astype(yc_ref.dtype)


def _poolconv(z, pool_w_bf16, pool_scale, dw_w, dw_b, ln_g, ln_b, pw_w_bf16, pw_b, layer):
    b, t, _ = z.shape
    tm = min(t, 512)
    r = tm // HALO
    a_blk = 7
    g_blk = 8

    def cur(col):
        return pl.BlockSpec((1, tm, GROUP), lambda bi, i: (bi, i, col))

    def halo(col):
        return pl.BlockSpec((1, HALO, GROUP), lambda bi, i: (bi, jnp.maximum(i * r - 1, 0), col))

    def vec(n):
        return pl.BlockSpec((1, 1, n), lambda bi, i: (layer, 0, 0))

    out_spec = pl.BlockSpec((1, tm, GROUP), lambda bi, i: (bi, i, 0))
    shp = jax.ShapeDtypeStruct((b, t, GROUP), BF16)
    return pl.pallas_call(
        functools.partial(_poolconv_kernel, tm=tm),
        grid=(b, t // tm),
        in_specs=[
            cur(0), halo(0), cur(a_blk), halo(a_blk), cur(g_blk), halo(g_blk),
            pl.BlockSpec((1, len(POOL_WINDOWS), LANES, LANES), lambda bi, i: (layer, 0, 0, 0)),
            vec(GROUP),
            pl.BlockSpec((1, CONV_WIDTH, GROUP), lambda bi, i: (layer, 0, 0)),
            vec(GROUP), vec(GROUP), vec(GROUP),
            pl.BlockSpec((1, GROUP, GROUP), lambda bi, i: (layer, 0, 0)),
            vec(GROUP),
        ],
        out_specs=[out_spec, out_spec],
        out_shape=[shp, shp],
        scratch_shapes=[pltpu.VMEM((tm + HALO, GROUP), F32), pltpu.VMEM((tm + HALO, GROUP), F32)],
        compiler_params=_params("arbitrary", "arbitrary"),
        name="pool_conv",
    )(z, z, z, z, z, z, pool_w_bf16, pool_scale, dw_w, dw_b, ln_g, ln_b, pw_w_bf16, pw_b)


def _diff_kernel(q_ref, k_ref, v_ref, lq1_ref, lk1_ref, lq2_ref, lk2_ref, g_ref, o_ref, *, tq, lambda_init):
    qi = pl.program_id(2)
    q = q_ref[0]
    lane = lax.broadcasted_iota(I32, (tq, HEAD), 1)
    zero = jnp.zeros_like(q)
    q2 = jnp.concatenate([jnp.where(lane < DIFF_QK, q, zero), jnp.where(lane >= DIFF_QK, q, zero)], axis=0)
    rows = 2 * tq

    def scores(j):
        kb = k_ref[0, pl.ds(pl.multiple_of(j * tq, tq), tq), :]
        return lax.dot_general(q2, kb, (((1,), (1,)), ((), ())), preferred_element_type=F32)

    def lane_tiles(s):
        return [s[:, c * LANES:(c + 1) * LANES] for c in range(tq // LANES)]

    row = lax.broadcasted_iota(I32, (rows, tq), 0) % tq
    col = lax.broadcasted_iota(I32, (rows, tq), 1)
    s_diag = jnp.where(col <= row, scores(qi), NEG_INF)

    def sweep_max(j, m_run):
        for part in lane_tiles(scores(j)):
            m_run = jnp.maximum(m_run, part)
        return m_run

    m_run = lax.fori_loop(0, qi, sweep_max, jnp.full((rows, LANES), NEG_INF, F32))
    for part in lane_tiles(s_diag):
        m_run = jnp.maximum(m_run, part)
    m = jnp.max(m_run, axis=-1, keepdims=True)

    def accumulate(p, vb, l_run, acc):
        for part in lane_tiles(p):
            l_run = l_run + part
        return l_run, acc + jnp.dot(p.astype(BF16), vb, preferred_element_type=F32)

    def sweep_acc(j, carry):
        vb = v_ref[0, pl.ds(pl.multiple_of(j * tq, tq), tq), :]
        return accumulate(jnp.exp2(scores(j) - m), vb, *carry)

    carry = lax.fori_loop(0, qi, sweep_acc, (jnp.zeros((rows, LANES), F32), jnp.zeros((rows, HEAD), F32)))
    vb_diag = v_ref[0, pl.ds(pl.multiple_of(qi * tq, tq), tq), :]
    l_run, acc = accumulate(jnp.exp2(s_diag - m), vb_diag, *carry)
    l = jnp.sum(l_run, axis=-1, keepdims=True)

    lam =(jnp.exp(jnp.sum(lq1_ref[0] * lk1_ref[0], axis=-1, keepdims=True))
           - jnp.exp(jnp.sum(lq2_ref[0] * lk2_ref[0], axis=-1, keepdims=True)) + lambda_init)
    o = acc / l
    od = o[:tq] - lam * o[tq:]
    y = od * lax.rsqrt(jnp.mean(od * od, axis=-1, keepdims=True) + RMS_EPS) * g_ref[0]
    o_ref[0] = (y * (1.0 - lambda_init)).astype(o_ref.dtype)


def _diff_attention(z, lq1, lk1, lq2, lk2, subln_g, layer, lambda_init):
    b, t, _ = z.shape
    heads = GROUP // HEAD
    tq = min(t, 512)
    q0, k0, v0 = 1 * heads, 2 * heads, 3 * heads

    def vec(n):
        return pl.BlockSpec((1, 1, n), lambda bi, h, i: (layer, 0, 0))

    return pl.pallas_call(
        functools.partial(_diff_kernel, tq=tq, lambda_init=lambda_init),
        grid=(b, heads, t // tq),
        in_specs=[
            pl.BlockSpec((1, tq, HEAD), lambda bi, h, i: (bi, i, q0 + h)),
            pl.BlockSpec((1, t, HEAD), lambda bi, h, i: (bi, 0, k0 + h)),
            pl.BlockSpec((1, t, HEAD), lambda bi, h, i: (bi, 0, v0 + h)),
            vec(DIFF_QK), vec(DIFF_QK), vec(DIFF_QK), vec(DIFF_QK), vec(HEAD),
        ],
        out_specs=pl.BlockSpec((1, tq, HEAD), lambda bi, h, i: (bi, i, h)),
        out_shape=jax.ShapeDtypeStruct((b, t, GROUP), BF16),
        compiler_params=_params("arbitrary", "arbitrary", "arbitrary"),
        name="diff_attention",
    )(z, z, z, lq1, lk1, lq2, lk2, subln_g)


def _dilated_kernel(q_ref, k_ref, v_ref, o_ref, qf, kf, vf, o0, o1, o2, e0, e1, e2, *, t, group):
    qf[...] = q_ref[0].astype(F32)
    kf[...] = k_ref[0].astype(F32)
    vf[...] = v_ref[0].astype(F32)
    band = DIL_BAND
    a_idx = lax.broadcasted_iota(I32, (band, 2 * band), 0)
    b_idx = lax.broadcasted_iota(I32, (band, 2 * band), 1)
    in_band = (b_idx >= a_idx) & (b_idx <= a_idx + band)
    outs = ((o0, e0), (o1, e1), (o2, e2))

    for (window, dil), (o_scr, e_scr) in zip(DILATED_PAIRS, outs):
        nb = t // (dil * band)

        def one_block(it, dil=dil, nb=nb, o_scr=o_scr, e_scr=e_scr):
            r = it // nb
            n = it % nb
            start = r + n * (band * dil)
            prev = r + jnp.maximum(n - 1, 0) * (band * dil)

            def rows(ref, s0):
                return ref[pl.ds(s0, band, stride=dil), :] if dil > 1 else ref[pl.ds(s0, band), :]

            qb = rows(qf, start).astype(BF16)
            kk = jnp.concatenate([rows(kf, prev), rows(kf, start)], axis=0).astype(BF16)
            vv = jnp.concatenate([rows(vf, prev), rows(vf, start)], axis=0).astype(BF16)
            s = lax.dot_general(qb, kk, (((1,), (1,)), ((), ())), preferred_element_type=F32)
            valid = in_band & (b_idx >= jnp.where(n > 0, 0, band))
            s = jnp.where(valid, s, NEG_INF)
            m = jnp.max(s, axis=-1, keepdims=True)
            p = jnp.exp2(s - m)
            l = jnp.sum(p, axis=-1, keepdims=True)
            o = jnp.dot(p.astype(BF16), vv, preferred_element_type=F32) / l
            idx = pl.ds(start, band, stride=dil) if dil > 1 else pl.ds(start, band)
            o_scr[idx, :] = o
            e_scr[idx, :] = jnp.broadcast_to(m + jnp.log2(l), (band, HEAD))

        def blocks(g, carry, one_block=one_block):
            for u in range(group):
                one_block(g * group + u)
            return carry

        lax.fori_loop(0, dil * nb // group, blocks, 0)

    top = jnp.maximum(jnp.maximum(e0[...], e1[...]), e2[...])
    w0 = jnp.exp2(e0[...] - top)
    w1 = jnp.exp2(e1[...] - top)
    w2 = jnp.exp2(e2[...] - top)
    mix = (w0 * o0[...] + w1 * o1[...] + w2 * o2[...]) / (w0 + w1 + w2)
    o_ref[0] = mix.astype(o_ref.dtype)


def _dilated_attention(z):
    b, t, _ = z.shape
    heads = GROUP // HEAD
    q0, k0, v0 = 4 * heads, 5 * heads, 6 * heads
    group = 8
    assert all(w // dl == DIL_BAND and t % w == 0 for w, dl in DILATED_PAIRS)
    assert (t // DIL_BAND) % group == 0

    def col(c0):
        return pl.BlockSpec((1, t, HEAD), lambda bi, h: (bi, 0, c0 + h))

    return pl.pallas_call(
        functools.partial(_dilated_kernel, t=t, group=group),
        grid=(b, heads),
        in_specs=[col(q0), col(k0), col(v0)],
        out_specs=pl.BlockSpec((1, t, HEAD), lambda bi, h: (bi, 0, h)),
        out_shape=jax.ShapeDtypeStruct((b, t, GROUP), BF16),
        scratch_shapes=[pltpu.VMEM((t, HEAD), F32)] * 9,
        compiler_params=_params("arbitrary", "arbitrary"),
        name="dilated_attention",
    )(z, z, z)


def _pack_rows(h, out_ref, row0=0):
    m, half = h.shape[0], h.shape[1] // 2
    bits = pltpu.bitcast(h.astype(BF16).astype(F32), U32)
    word = (bits[:, half:] & jnp.uint32(0xFFFF0000)) | (bits[:, :half] >> 16)
    n_slab = half // LANES
    for s in range(n_slab):
        out_ref[pl.ds(row0 * n_slab + s, m, stride=n_slab), :] = word[:, s * LANES:(s + 1) * LANES]


def _unpack_rows(word):
    return pltpu.bitcast(word << 16, F32), pltpu.bitcast(word & jnp.uint32(0xFFFF0000), F32)


def _outproj_router_kernel(yp_ref, yd_ref, yc_ref, yv_ref, wo_ref, x_ref, g1_ref, n2_ref, sc_ref, sh_ref,
                           rw_ref, rb_ref,
                           x1_ref, h2_ref, idx_ref, rank_ref, wts_ref, cnt_ref, carry_scr, *, tm):
    first = (pl.program_id(0) == 0) & (pl.program_id(1) == 0)

    @pl.when(first)
    def _():
        carry_scr[...] = jnp.zeros_like(carry_scr)

    mix = jnp.dot(yp_ref[0], wo_ref[0 * GROUP:1 * GROUP, :], preferred_element_type=F32)
    mix = mix + jnp.dot(yd_ref[0], wo_ref[1 * GROUP:2 * GROUP, :], preferred_element_type=F32)
    mix = mix + jnp.dot(yc_ref[0], wo_ref[2 * GROUP:3 * GROUP, :], preferred_element_type=F32)
    mix = mix + jnp.dot(yv_ref[0], wo_ref[3 * GROUP:4 * GROUP, :], preferred_element_type=F32)
    x1 = x_ref[0] + g1_ref[0] * mix
    x1_ref[0] = x1

    y = x1 * lax.rsqrt(jnp.mean(x1 * x1, axis=-1, keepdims=True) + RMS_EPS) * n2_ref[0]
    h2 = y * (1.0 + sc_ref[0]) + sh_ref[0]
    _pack_rows(h2, h2_ref)

    logits = lax.dot_general(rw_ref[0], h2, (((1,), (1,)), ((), ())), precision=HIGHEST,
                             preferred_element_type=F32) + rb_ref[0]
    e_idx = lax.broadcasted_iota(I32, (N_EXPERTS, tm), 0)
    work = logits
    vals, sels, hots = [], [], []
    for _ in range(TOP_K):
        mx = jnp.max(work, axis=0, keepdims=True)
        sel = jnp.min(jnp.where(work == mx, e_idx, N_EXPERTS), axis=0, keepdims=True)
        hot = e_idx == sel
        vals.append(mx)
        sels.append(sel)
        hots.append(hot)
        work = jnp.where(hot, -jnp.inf, work)
    exps = [jnp.exp(v - vals[0]) for v in vals]
    denom = exps[0] + exps[1] + exps[2] + exps[3]

    chosen = jnp.zeros((N_EXPERTS, tm), F32)
    for hot in hots:
        chosen = chosen + hot.astype(F32)
    s_idx = lax.broadcasted_iota(I32, (tm, tm), 0)
    t_idx = lax.broadcasted_iota(I32, (tm, tm), 1)
    upper = (s_idx < t_idx).astype(BF16)
    before = jnp.dot(chosen.astype(BF16), upper, preferred_element_type=F32) + carry_scr[:, 0:1]
    for k in range(TOP_K):
        idx_ref[k:k + 1, :] = sels[k]
        rank_ref[k:k + 1, :] = jnp.sum(jnp.where(hots[k], before, 0.0), axis=0, keepdims=True).astype(I32)
        wts_ref[k:k + 1, :] = exps[k] / denom
    carry_scr[...] = carry_scr[...] + jnp.sum(chosen, axis=1, keepdims=True)
    cnt_ref[...] = carry_scr[...]


def _outproj_router(ys, w_out_bf16, x, gate1, norm2_g, scale2, shift2, router_wt, router_b, layer):
    b, t, d = x.shape
    tm = min(t, 512)
    n = b * t
    nt = t // tm
    slab = d // 2 // LANES

    def ytile():
        return pl.BlockSpec((1, tm, GROUP), lambda bi, i: (bi, i, 0))

    def bvec():
        return pl.BlockSpec((1, 1, d), lambda bi, i: (bi, 0, 0))

    tok = pl.BlockSpec((TOP_K, tm), lambda bi, i: (0, bi * nt + i))
    outs = pl.pallas_call(
        functools.partial(_outproj_router_kernel, tm=tm),
        grid=(b, nt),
        in_specs=[
            ytile(), ytile(), ytile(), ytile(),
            pl.BlockSpec((None, 4 * GROUP, d), lambda bi, i: (layer, 0, 0)),
            pl.BlockSpec((1, tm, d), lambda bi, i: (bi, i, 0)),
            bvec(),
            pl.BlockSpec((1, 1, d), lambda bi, i: (layer, 0, 0)),
            bvec(), bvec(),
            pl.BlockSpec((1, N_EXPERTS, d), lambda bi, i: (layer, 0, 0)),
            pl.BlockSpec((1, N_EXPERTS, 1), lambda bi, i: (layer, 0, 0)),
        ],
        out_specs=[
            pl.BlockSpec((1, tm, d), lambda bi, i: (bi, i, 0)),
            pl.BlockSpec((tm * slab, LANES), lambda bi, i: (bi * nt + i, 0)),
            tok, tok, tok,
            pl.BlockSpec((N_EXPERTS, LANES), lambda bi, i: (0, 0)),
        ],
        out_shape=[
            jax.ShapeDtypeStruct((b, t, d), F32),
            jax.ShapeDtypeStruct((n * slab, LANES), U32),
            jax.ShapeDtypeStruct((TOP_K, n), I32),
            jax.ShapeDtypeStruct((TOP_K, n), I32),
            jax.ShapeDtypeStruct((TOP_K, n), F32),
            jax.ShapeDtypeStruct((N_EXPERTS, LANES), F32),
        ],
        scratch_shapes=[pltpu.VMEM((N_EXPERTS, LANES), F32)],
        compiler_params=_params("arbitrary", "arbitrary"),
        name="outproj_router",
    )(*ys, w_out_bf16, x, gate1, norm2_g, scale2, shift2, router_wt, router_b)
    return outs


def _scatter_kernel(zs_ref, pos_ref, h_ref, xs_hbm, zbuf, zsem, sem, *, tm, tm_e):
    zrows = zbuf.shape[0]

    @pl.when(pl.program_id(0) == 0)
    def _():
        zbuf[...] = jnp.zeros_like(zbuf)

        def zero(e, c):
            @pl.when(zs_ref[e] >= 0)
            def _():
                for q in range(tm_e // zrows):
                    pltpu.make_async_copy(zbuf, xs_hbm.at[pl.ds(zs_ref[e] + q * zrows, zrows)], zsem).start()
            return c

        lax.fori_loop(0, N_EXPERTS, zero, 0)

        def zero_done(e, c):
            @pl.when(zs_ref[e] >= 0)
            def _():
                for q in range(tm_e // zrows):
                    pltpu.make_async_copy(zbuf, xs_hbm.at[pl.ds(0, zrows)], zsem).wait()
            return c

        lax.fori_loop(0, N_EXPERTS, zero_done, 0)

    def issue(g, c):
        for u in range(ISSUE_UNROLL):
            tt = g * ISSUE_UNROLL + u
            for k in range(TOP_K):
                pltpu.make_async_copy(h_ref.at[tt], xs_hbm.at[pos_ref[k, tt]], sem).start(priority=k % 2)
        return c

    lax.fori_loop(0, tm // ISSUE_UNROLL, issue, 0)
    for k in range(TOP_K):
        pltpu.make_async_copy(h_ref, xs_hbm.at[pl.ds(0, tm)], sem).wait()


def _scatter_rows(zero_start, pos, h_rows, m_pad, tm_e):
    n, slab, _ = h_rows.shape
    tm = min(n, 256)
    zrows = min(tm_e, 256)
    grid_spec = pltpu.PrefetchScalarGridSpec(
        num_scalar_prefetch=1,
        grid=(n // tm,),
        in_specs=[
            pl.BlockSpec((TOP_K, tm), lambda i, zs: (0, i), memory_space=pltpu.SMEM),
            pl.BlockSpec((tm, slab, LANES), lambda i, zs: (i, 0, 0)),
        ],
        out_specs=pl.BlockSpec(memory_space=pl.ANY),
        scratch_shapes=[pltpu.VMEM((zrows, slab, LANES), U32), pltpu.SemaphoreType.DMA(()),
                        pltpu.SemaphoreType.DMA(())],
    )
    return pl.pallas_call(
        functools.partial(_scatter_kernel, tm=tm, tm_e=tm_e),
        grid_spec=grid_spec,
        out_shape=jax.ShapeDtypeStruct((m_pad, slab, LANES), U32),
        compiler_params=_params("arbitrary"),
        name="scatter_rows",
    )(zero_start, pos, h_rows)


def _deinterleave(hh):
    m, width = hh.shape
    lane = lax.broadcasted_iota(I32, (m, LANES), 1)
    low = lane < LANES // 2
    evens_then_odds = jnp.where(low, 2 * lane, 2 * lane - (LANES - 1))
    parts = [jnp.take_along_axis(hh[:, b * LANES:(b + 1) * LANES], evens_then_odds, axis=1)
             for b in range(width // LANES)]
    gates, lins = [], []
    for b in range(0, len(parts), 2):
        first, second = parts[b], parts[b + 1]
        gates.append(jnp.where(low, first, pltpu.roll(second, LANES // 2, 1)))
        lins.append(jnp.where(low, pltpu.roll(first, LANES // 2, 1), second))
    return jnp.concatenate(gates, axis=1), jnp.concatenate(lins, axis=1)


def _expert_kernel(te_ref, tv_ref, nu_ref, xs_ref, w1_ref, b1_ref, w2_ref, b2_ref, ys_ref,
                   x_scr, acc_scr, *, tm, sub, nc):
    i = pl.program_id(0)
    c = pl.program_id(1)
    d = x_scr.shape[1]
    n_slab = d // 2 // LANES
    live = i < nu_ref[0]

    @pl.when(live)
    def _():
        @pl.when(c == 0)
        def _():
            for s in range(n_slab):
                lo, hi = _unpack_rows(xs_ref[pl.ds(s, tm, stride=n_slab), :])
                x_scr[:, s * LANES:(s + 1) * LANES] = lo.astype(BF16)
                x_scr[:, d // 2 + s * LANES:d // 2 + (s + 1) * LANES] = hi.astype(BF16)
            acc_scr[...] = jnp.zeros_like(acc_scr) + b2_ref[...]

        def ffn_rows(sb, w1c, w2c):
            rows = slice(sb * sub, (sb + 1) * sub)
            hh = jnp.dot(x_scr[rows, :], w1c, preferred_element_type=F32) + b1_ref[...]
            g, lin = _deinterleave(hh)
            g = jnp.minimum(g, SWIGLU_LIMIT)
            lin = jnp.clip(lin, -SWIGLU_LIMIT, SWIGLU_LIMIT)
            act = g * _sigmoid(SWIGLU_ALPHA * g) * (lin + 1.0)
            acc_scr[rows, :] += jnp.dot(act.astype(BF16), w2c, preferred_element_type=F32)

        ffn_rows(0, w1_ref[...].astype(BF16), w2_ref[...].astype(BF16))
        for sb in range(1, tm // sub):
            @pl.when(sb * sub < tv_ref[i])
            def _(sb=sb):
                ffn_rows(sb, w1_ref[...].astype(BF16), w2_ref[...].astype(BF16))

        @pl.when(c == nc - 1)
        def _():
            for sb in range(tm // sub):
                _pack_rows(acc_scr[sb * sub:(sb + 1) * sub, :], ys_ref, row0=sb * sub)

    @pl.when(jnp.logical_not(live) & (c == 0))
    def _():
        ys_ref[...] = jnp.zeros_like(ys_ref)


def _experts(tile_expert, tile_valid, n_used, xs2d, w1, b1, w2, b2, layer, tm, sub, m_pad):
    _, n_exp, d, h2 = w1.shape
    hid = h2 // 2
    tc = min(hid, 512)
    nc = hid // tc
    n_tiles = m_pad // tm
    slab = d // 2 // LANES

    def live(i, nu):
        return jnp.minimum(i, nu[0] - 1)

    def chunk(i, c, nu):
        return jnp.where(i < nu[0], c, nc - 1)

    grid_spec = pltpu.PrefetchScalarGridSpec(
        num_scalar_prefetch=3,
        grid=(n_tiles, nc),
        in_specs=[
            pl.BlockSpec((tm * slab, LANES), lambda i, c, te, tv, nu: (live(i, nu), 0)),
            pl.BlockSpec((None, None, d, 2 * tc), lambda i, c, te, tv, nu: (layer, te[i], 0, chunk(i, c, nu))),
            pl.BlockSpec((None, None, 1, 2 * tc), lambda i, c, te, tv, nu: (layer, te[i], 0, chunk(i, c, nu))),
            pl.BlockSpec((None, None, tc, d), lambda i, c, te, tv, nu: (layer, te[i], chunk(i, c, nu), 0)),
            pl.BlockSpec((None, None, 1, d), lambda i, c, te, tv, nu: (layer, te[i], 0, 0)),
        ],
        out_specs=pl.BlockSpec((tm * slab, LANES), lambda i, c, te, tv, nu: (i, 0)),
        scratch_shapes=[pltpu.VMEM((tm, d), BF16), pltpu.VMEM((tm, d), F32)],
    )
    return pl.pallas_call(
        functools.partial(_expert_kernel, tm=tm, sub=sub, nc=nc),
        grid_spec=grid_spec,
        out_shape=jax.ShapeDtypeStruct((m_pad * slab, LANES), U32),
        compiler_params=_params("arbitrary", "arbitrary"),
        name="experts",
    )(tile_expert, tile_valid, n_used, xs2d, w1, b1, w2, b2)


def _combine_kernel(pos_ref, pos_next_ref, wts_ref, x_ref, g2_ref, fg_ref, ys_hbm, ys_flat_hbm, o_ref, buf, sem,
                    *, tm, final, n_steps):
    i = pl.program_id(0)
    n_slab = x_ref.shape[1] // 2 // LANES
    slot_rows = TOP_K * tm * n_slab
    slot = i % 2
    base = pl.multiple_of(slot * slot_rows, slot_rows)

    def issue_block(p_ref, to_slot):
        to_base = to_slot * slot_rows

        def issue(g, c):
            for u in range(ISSUE_UNROLL):
                tt = g * ISSUE_UNROLL + u
                for k in range(TOP_K):
                    row0 = pl.multiple_of(to_base + (k * tm + tt) * n_slab, n_slab)
                    pltpu.make_async_copy(ys_hbm.at[p_ref[k, tt]], buf.at[pl.ds(row0, n_slab)],
                                          sem.at[to_slot]).start(priority=k % 2)
            return c

        lax.fori_loop(0, tm // ISSUE_UNROLL, issue, 0)

    @pl.when(i == 0)
    def _():
        issue_block(pos_ref, 0)

    @pl.when(i + 1 < n_steps)
    def _():
        issue_block(pos_next_ref, 1 - slot)

    pltpu.make_async_copy(ys_flat_hbm.at[pl.ds(0, slot_rows)], buf.at[pl.ds(base, slot_rows)], sem.at[slot]).wait()

    w_sq = jnp.concatenate([wts_ref[...], jnp.zeros((tm - TOP_K, tm), F32)], axis=0)
    w_t = w_sq.T
    w_k = [jnp.broadcast_to(w_t[:, k:k + 1], (tm, LANES)) for k in range(TOP_K)]
    lows, highs = [], []
    for s in range(n_slab):
        acc_lo = jnp.zeros((tm, LANES), F32)
        acc_hi = jnp.zeros((tm, LANES), F32)
        for k in range(TOP_K):
            lo, hi = _unpack_rows(buf[pl.ds(base + k * tm * n_slab + s, tm, stride=n_slab), :])
            acc_lo = acc_lo + w_k[k] * lo
            acc_hi = acc_hi + w_k[k] * hi
        lows.append(acc_lo)
        highs.append(acc_hi)
    moe = jnp.concatenate(lows + highs, axis=1)
    x2 = x_ref[...] + g2_ref[0] * moe
    if final:
        x2 = x2 * lax.rsqrt(jnp.mean(x2 * x2, axis=-1, keepdims=True) + RMS_EPS) * fg_ref[...]
    o_ref[...] = x2


def _combine(pos, wts, x1, gate2, final_g, ys_rows, tokens_per_batch, final):
    n, d = x1.shape
    tm = 256
    steps_per_batch = tokens_per_batch // tm
    n_steps = n // tm
    m_pad, slab, _ = ys_rows.shape
    return pl.pallas_call(
        functools.partial(_combine_kernel, tm=tm, final=final, n_steps=n_steps),
        grid=(n_steps,),
        in_specs=[
            pl.BlockSpec((TOP_K, tm), lambda i: (0, i), memory_space=pltpu.SMEM),
            pl.BlockSpec((TOP_K, tm), lambda i: (0, jnp.minimum(i + 1, n_steps - 1)), memory_space=pltpu.SMEM),
            pl.BlockSpec((TOP_K, tm), lambda i: (0, i)),
            pl.BlockSpec((tm, d), lambda i: (i, 0)),
            pl.BlockSpec((1, 1, d), lambda i: (i // steps_per_batch, 0, 0)),
            pl.BlockSpec((1, d), lambda i: (0, 0)),
            pl.BlockSpec(memory_space=pl.ANY),
            pl.BlockSpec(memory_space=pl.ANY),
        ],
        out_specs=pl.BlockSpec((tm, d), lambda i: (i, 0)),
        out_shape=jax.ShapeDtypeStruct((n, d), F32),
        scratch_shapes=[pltpu.VMEM((2 * TOP_K * tm * slab, LANES), U32), pltpu.SemaphoreType.DMA((2,))],
        compiler_params=_params("arbitrary"),
        name="combine",
    )(pos, pos, wts, x1, gate2, final_g, ys_rows, ys_rows.reshape(m_pad * slab, LANES))


def _routing_plan(idx, rank, counts, tm_e, n_tiles):
    experts = jnp.arange(N_EXPERTS, dtype=I32)
    cnt = counts[:, 0].astype(I32)
    tiles = (cnt + tm_e - 1) // tm_e
    tile_end = jnp.cumsum(tiles)
    tile_start = tile_end - tiles
    offsets = tile_start * tm_e
    pos = rank + jnp.sum(jnp.where(idx[..., None] == experts, offsets, 0), axis=-1)
    n_used = tile_end[-1]
    tile_ids = jnp.arange(n_tiles, dtype=I32)
    te = jnp.sum((tile_ids[:, None] >= tile_end[None, :]).astype(I32), axis=1)
    te_last = jnp.sum((n_used - 1 >= tile_end).astype(I32))
    te = jnp.where(tile_ids < n_used, te, te_last).astype(I32)
    mine = te[:, None] == experts[None, :]
    rows_left = jnp.sum(jnp.where(mine, cnt[None, :] - (tile_ids[:, None] - tile_start[None, :]) * tm_e, 0), axis=1)
    tile_valid = jnp.where(tile_ids < n_used, jnp.clip(rows_left, 0, tm_e), 0).astype(I32)
    zero_start = jnp.where(tiles > 0, (tile_end - 1) * tm_e, -1).astype(I32)
    return pos.astype(I32), te, tile_valid, n_used.reshape(1).astype(I32), zero_start


def kernel(x, c, positions, mod_w, mod_b, norm1_g, norm2_g, w_in, pool_w, pool_scale, diff_lq1, diff_lk1,
           diff_lq2, diff_lk2, diff_subln_g, conv_dw_w, conv_dw_b, conv_ln_g, conv_ln_b, conv_pw_w, conv_pw_b,
           w_out, router_w, router_b, exp_w1, exp_b1, exp_w2, exp_b2, final_g):
    b, t, d = x.shape
    depth = mod_w.shape[0]
    n = b * t
    tm_e = 1024
    sub_e = 512
    m_pad = n * TOP_K + N_EXPERTS * tm_e
    n_tiles = m_pad // tm_e
    slab = d // 2 // LANES

    def row3(a):
        return a.reshape(a.shape[0], 1, a.shape[1])

    mod = _modulation(c, mod_w, mod_b)
    tables = _rope_tables(positions)
    w_in_b = w_in.astype(BF16)
    w_out_b = w_out.astype(BF16)
    pool_w_b = pool_w.astype(BF16)
    pw_w_b = conv_pw_w.astype(BF16)
    router_wt = jnp.swapaxes(router_w, 1, 2)
    router_b3 = router_b.reshape(depth, N_EXPERTS, 1)
    b1r = exp_b1.reshape(depth, N_EXPERTS, 1, exp_b1.shape[-1])
    b2r = exp_b2.reshape(depth, N_EXPERTS, 1, d)
    final_g2 = final_g.reshape(1, d)

    for l in range(depth):
        lambda_init = 0.8 - 0.6 * math.exp(-0.3 * l)
        sh1, sc1, g1, sh2, sc2, g2 = [m.reshape(b, 1, d) for m in jnp.split(mod[l], 6, axis=-1)]
        z = _inproj(x, row3(norm1_g), sc1, sh1, w_in_b, tables, l)
        y_pool, y_conv = _poolconv(z, pool_w_b, row3(pool_scale), conv_dw_w, row3(conv_dw_b), row3(conv_ln_g),
                                   row3(conv_ln_b), pw_w_b, row3(conv_pw_b), l)
        y_diff = _diff_attention(z, row3(diff_lq1), row3(diff_lk1), row3(diff_lq2), row3(diff_lk2),
                                 row3(diff_subln_g), l, lambda_init)
        y_dil = _dilated_attention(z)
        x1, h2, idx, rank, wts, counts = _outproj_router(
            (y_pool, y_diff, y_dil, y_conv), w_out_b, x, g1, row3(norm2_g), sc2, sh2, router_wt, router_b3, l)
        pos, tile_expert, tile_valid, n_used, zero_start = _routing_plan(idx, rank, counts, tm_e, n_tiles)
        xs = _scatter_rows(zero_start, pos, h2.reshape(n, slab, LANES), m_pad, tm_e)
        ys = _experts(tile_expert, tile_valid, n_used, xs.reshape(m_pad * slab, LANES), exp_w1, b1r, exp_w2, b2r,
                      l, tm_e, sub_e, m_pad)
        x = _combine(pos, wts, x1.reshape(n, d), g2, final_g2, ys.reshape(m_pad, slab, LANES), t,
                     final=(l == depth - 1)).reshape(b, t, d)
    return x
```

```python
import functools
import math

import numpy as np
import jax
import jax.numpy as jnp
from jax import lax
from jax.experimental import pallas as pl
from jax.experimental.pallas import tpu as pltpu

F32 = jnp.float32
BF16 = jnp.bfloat16
I32 = jnp.int32
U32 = jnp.uint32
HIGHEST = lax.Precision.HIGHEST

LANES = 128
SUBLANES = 8
VMEM_LIMIT_BYTES = 60 * 1024 * 1024

POOL_WINDOWS = (2, 4, 8, 16)
DILATED_PAIRS = ((128, 1), (512, 4), (2048, 16))
DIL_BAND = 128
CONV_WIDTH = 31
HALO = 32
ISSUE_UNROLL = 8
N_EXPERTS = 32
TOP_K = 4
SWIGLU_ALPHA = 1.702
SWIGLU_LIMIT = 7.0
ROPE_THETA = 500000.0
RMS_EPS = 1e-6
LN_EPS = 1e-5
NEG_INF = -1e30
GROUP = 512
HEAD = 128
DIFF_QK = 64
LOG2E = math.log2(math.e)


def _params(*semantics):
    return pltpu.CompilerParams(dimension_semantics=semantics, vmem_limit_bytes=VMEM_LIMIT_BYTES)


def _sigmoid(x):
    return 1.0 / (1.0 + jnp.exp(-x))


def _mod_kernel(c_ref, w_ref, b_ref, o_ref):
    c = c_ref[...]
    ca = c * _sigmoid(c)
    o_ref[0] = jnp.dot(ca, w_ref[0], precision=HIGHEST, preferred_element_type=F32) + b_ref[0]


def _modulation(c, mod_w, mod_b):
    depth, d, n = mod_w.shape
    b = c.shape[0]
    rows = 8
    c_pad = jnp.zeros((rows, d), F32).at[:b].set(c)
    tn = 512
    out = pl.pallas_call(
        _mod_kernel,
        grid=(depth, n // tn),
        in_specs=[
            pl.BlockSpec((rows, d), lambda l, j: (0, 0)),
            pl.BlockSpec((1, d, tn), lambda l, j: (l, 0, j)),
            pl.BlockSpec((1, 1, tn), lambda l, j: (l, 0, j)),
        ],
        out_specs=pl.BlockSpec((1, rows, tn), lambda l, j: (l, 0, j)),
        out_shape=jax.ShapeDtypeStruct((depth, rows, n), F32),
        compiler_params=_params("arbitrary", "arbitrary"),
        name="modulation",
    )(c_pad, mod_w, mod_b.reshape(depth, 1, n))
    return out[:, :b]


def _rope_table_kernel(pos_ref, invd_ref, invc_ref, cd_ref, sd_ref, cc_ref, sc_ref):
    p = pos_ref[0]
    lane = lax.broadcasted_iota(I32, (1, LANES), 1)
    for inv_ref, c_ref, s_ref, hd in ((invd_ref, cd_ref, sd_ref, DIFF_QK), (invc_ref, cc_ref, sc_ref, HEAD)):
        half = hd // 8
        lm = lane % hd
        ang = p * inv_ref[...]
        c_ref[0] = jnp.cos(ang)
        s_ref[0] = jnp.where(lm < half, -jnp.sin(ang), jnp.sin(ang))


def _lane_inv_freq(hd):
    half = hd // 8
    inv = ROPE_THETA ** (-jnp.arange(half, dtype=F32) / half)
    lm = np.arange(LANES) % hd
    rotated = jnp.asarray(lm < 2 * half)
    return jnp.where(rotated, inv[lm % half], 0.0).reshape(1, LANES).astype(F32)


def _rope_tables(positions):
    b, t = positions.shape
    tm = min(t, 1024)
    pos = positions.astype(F32).reshape(b, t, 1)
    spec_t = pl.BlockSpec((1, tm, LANES), lambda bi, i: (bi, i, 0))
    spec_inv = pl.BlockSpec((1, LANES), lambda bi, i: (0, 0))
    shp = jax.ShapeDtypeStruct((b, t, LANES), F32)
    return pl.pallas_call(
        _rope_table_kernel,
        grid=(b, t // tm),
        in_specs=[pl.BlockSpec((1, tm, 1), lambda bi, i: (bi, i, 0)), spec_inv, spec_inv],
        out_specs=[spec_t] * 4,
        out_shape=[shp] * 4,
        compiler_params=_params("arbitrary", "arbitrary"),
        name="rope_tables",
    )(pos, _lane_inv_freq(DIFF_QK), _lane_inv_freq(HEAD))


def _rope_apply(z, cos, sin, hd):
    half = hd // 8
    lane = lax.broadcasted_iota(I32, (1, LANES), 1)
    first = (lane % hd) < half
    outs = []
    for cb in range(z.shape[1] // LANES):
        zc = z[:, cb * LANES:(cb + 1) * LANES]
        partner = jnp.where(first, pltpu.roll(zc, LANES - half, 1), pltpu.roll(zc, half, 1))
        outs.append(zc * cos + partner * sin)
    return jnp.concatenate(outs, axis=1)


def _inproj_kernel(x_ref, g_ref, sc_ref, sh_ref, w_ref, cd_ref, sd_ref, cc_ref, sc2_ref, z_ref, h_scr):
    j = pl.program_id(2)

    @pl.when(j == 0)
    def _():
        x = x_ref[0]
        y = x * lax.rsqrt(jnp.mean(x * x, axis=-1, keepdims=True) + RMS_EPS) * g_ref[0]
        h_scr[...] = (y * (1.0 + sc_ref[0]) + sh_ref[0]).astype(BF16)

    z = jnp.dot(h_scr[...], w_ref[...], preferred_element_type=F32)

    @pl.when((j == 1) | (j == 2))
    def _():
        r = _rope_apply(z, cd_ref[0], sd_ref[0], DIFF_QK)
        r = r * jnp.where(j == 1, DIFF_QK ** -0.5 * LOG2E, 1.0)
        z_ref[0] = r.astype(z_ref.dtype)

    @pl.when((j == 4) | (j == 5))
    def _():
        r = _rope_apply(z, cc_ref[0], sc2_ref[0], HEAD)
        r = r * jnp.where(j == 4, HEAD ** -0.5 * LOG2E, 1.0)
        z_ref[0] = r.astype(z_ref.dtype)

    @pl.when((j != 1) & (j != 2) & (j != 4) & (j != 5))
    def _():
        z_ref[0] = z.astype(z_ref.dtype)


def _inproj(x, norm_g, scale, shift, w_in_bf16, tables, layer):
    b, t, d = x.shape
    n = w_in_bf16.shape[-1]
    tm = min(t, 1024)
    tn = GROUP
    cd, sd, cc, sc = tables
    spec_tab = pl.BlockSpec((1, tm, LANES), lambda bi, i, j: (bi, i, 0))
    spec_vec = pl.BlockSpec((1, 1, d), lambda bi, i, j: (bi, 0, 0))
    return pl.pallas_call(
        _inproj_kernel,
        grid=(b, t // tm, n // tn),
        in_specs=[
            pl.BlockSpec((1, tm, d), lambda bi, i, j: (bi, i, 0)),
            pl.BlockSpec((1, 1, d), lambda bi, i, j: (layer, 0, 0)),
            spec_vec, spec_vec,
            pl.BlockSpec((None, d, tn), lambda bi, i, j: (layer, 0, j)),
            spec_tab, spec_tab, spec_tab, spec_tab,
        ],
        out_specs=pl.BlockSpec((1, tm, tn), lambda bi, i, j: (bi, i, j)),
        out_shape=jax.ShapeDtypeStruct((b, t, n), BF16),
        scratch_shapes=[pltpu.VMEM((tm, d), BF16)],
        compiler_params=_params("arbitrary", "arbitrary", "arbitrary"),
        name="inproj",
    )(x, norm_g, scale, shift, w_in_bf16, cd, sd, cc, sc)


def _poolconv_kernel(zp_ref, zph_ref, za_ref, zah_ref, zg_ref, zgh_ref,
                     pw_ref, ps_ref, dww_ref, dwb_ref, lng_ref, lnb_ref, pww_ref, pwb_ref,
                     yp_ref, yc_ref, xp_scr, u_scr, ush_scr, *, tm):
    i = pl.program_id(1)
    keep = jnp.where(i == 0, 0.0, 1.0)

    xp_scr[0:HALO, :] = zph_ref[0].astype(F32) * keep
    xp_scr[HALO:, :] = zp_ref[0].astype(F32)
    t_glob = i * tm + lax.broadcasted_iota(I32, (tm, 1), 0)
    for g, w in enumerate(POOL_WINDOWS):
        cols = slice(g * LANES, (g + 1) * LANES)
        xg = xp_scr[HALO:HALO + tm, cols]
        acc = xg
        for k in range(1, w):
            acc = acc + xp_scr[HALO - k:HALO - k + tm, cols]
        cnt = jnp.minimum(t_glob + 1, w).astype(F32)
        pooled = acc / cnt - xg
        yg = jnp.dot(pooled.astype(BF16), pw_ref[0, g], preferred_element_type=F32)
        yp_ref[0, :, cols] = (yg * ps_ref[0, :, cols]).astype(yp_ref.dtype)

    ah = zah_ref[0].astype(F32)
    gh = zgh_ref[0].astype(F32)
    u_scr[0:HALO, :] = ah * _sigmoid(gh) * keep
    a = za_ref[0].astype(F32)
    gg = zg_ref[0].astype(F32)
    u_scr[HALO:, :] = a * _sigmoid(gg)
    span = tm + HALO - SUBLANES
    for sh in range(1, SUBLANES):
        ush_scr[sh - 1] = u_scr[sh:sh + span, :]
    acc = jnp.zeros((tm, GROUP), F32) + dwb_ref[0]
    base = HALO - (CONV_WIDTH - 1)
    for k in range(CONV_WIDTH):
        off = base + k
        sh, start = off % SUBLANES, off - off % SUBLANES
        window = u_scr[start:start + tm, :] if sh == 0 else ush_scr[sh - 1, start:start + tm, :]
        acc = acc + window * dww_ref[0, k:k + 1, :]
    mu = jnp.mean(acc, axis=-1, keepdims=True)
    cen = acc - mu
    var = jnp.mean(cen * cen, axis=-1, keepdims=True)
    v = cen * lax.rsqrt(var + LN_EPS) * lng_ref[0] + lnb_ref[0]
    v = v * _sigmoid(v)
    y = jnp.dot(v.astype(BF16), pww_ref[0], preferred_element_type=F32) + pwb_ref[0]
    yc_ref[0] = y.astype(yc_ref.dtype)


def _poolconv(z, pool_w_bf16, pool_scale, dw_w, dw_b, ln_g, ln_b, pw_w_bf16, pw_b, layer):
    b, t, _ = z.shape
    tm = min(t, 512)
    r = tm // HALO
    a_blk = 7
    g_blk = 8

    def cur(col):
        return pl.BlockSpec((1, tm, GROUP), lambda bi, i: (bi, i, col))

    def halo(col):
        return pl.BlockSpec((1, HALO, GROUP), lambda bi, i: (bi, jnp.maximum(i * r - 1, 0), col))

    def vec(n):
        return pl.BlockSpec((1, 1, n), lambda bi, i: (layer, 0, 0))

    out_spec = pl.BlockSpec((1, tm, GROUP), lambda bi, i: (bi, i, 0))
    shp = jax.ShapeDtypeStruct((b, t, GROUP), BF16)
    return pl.pallas_call(
        functools.partial(_poolconv_kernel, tm=tm),
        grid=(b, t // tm),
        in_specs=[
            cur(0), halo(0), cur(a_blk), halo(a_blk), cur(g_blk), halo(g_blk),
            pl.BlockSpec((1, len(POOL_WINDOWS), LANES, LANES), lambda bi, i: (layer, 0, 0, 0)),
            vec(GROUP),
            pl.BlockSpec((1, CONV_WIDTH, GROUP), lambda bi, i: (layer, 0, 0)),
            vec(GROUP), vec(GROUP), vec(GROUP),
            pl.BlockSpec((1, GROUP, GROUP), lambda bi, i: (layer, 0, 0)),
            vec(GROUP),
        ],
        out_specs=[out_spec, out_spec],
        out_shape=[shp, shp],
        scratch_shapes=[pltpu.VMEM((tm + HALO, GROUP), F32), pltpu.VMEM((tm + HALO, GROUP), F32),
                        pltpu.VMEM((SUBLANES - 1, tm + HALO - SUBLANES, GROUP), F32)],
        compiler_params=_params("arbitrary", "arbitrary"),
        name="pool_conv",
    )(z, z, z, z, z, z, pool_w_bf16, pool_scale, dw_w, dw_b, ln_g, ln_b, pw_w_bf16, pw_b)


def _diff_kernel(q_ref, k_ref, v_ref, lq1_ref, lk1_ref, lq2_ref, lk2_ref, g_ref, o_ref, *, tq, lambda_init):
    qi = pl.program_id(2)
    q = q_ref[0]
    lane = lax.broadcasted_iota(I32, (tq, HEAD), 1)
    zero = jnp.zeros_like(q)
    q2 = jnp.concatenate([jnp.where(lane < DIFF_QK, q, zero), jnp.where(lane >= DIFF_QK, q, zero)], axis=0)
    rows = 2 * tq

    def lane_tiles(s):
        return [s[:, c * LANES:(c + 1) * LANES] for c in range(tq // LANES)]

    def fold(j, carry, masked):
        m, l_run, acc = carry
        kb = k_ref[0, pl.ds(pl.multiple_of(j * tq, tq), tq), :]
        vb = v_ref[0, pl.ds(pl.multiple_of(j * tq, tq), tq), :]
        s = lax.dot_general(q2, kb, (((1,), (1,)), ((), ())), preferred_element_type=F32)
        if masked:
            row = lax.broadcasted_iota(I32, (rows, tq), 0) % tq
            col = lax.broadcasted_iota(I32, (rows, tq), 1)
            s = jnp.where(col <= row, s, NEG_INF)
        tiles = lane_tiles(s)
        m_blk = tiles[0]
        for part in tiles[1:]:
            m_blk = jnp.maximum(m_blk, part)
        m_new = jnp.maximum(m, jnp.max(m_blk, axis=-1, keepdims=True))
        alpha = jnp.exp2(m - m_new)
        p = jnp.exp2(s - m_new)
        l_run = alpha * l_run
        for part in lane_tiles(p):
            l_run = l_run + part
        acc = alpha * acc + jnp.dot(p.astype(BF16), vb, preferred_element_type=F32)
        return m_new, l_run, acc

    init = (jnp.full((rows, 1), NEG_INF, F32), jnp.zeros((rows, LANES), F32), jnp.zeros((rows, HEAD), F32))
    carry = lax.fori_loop(0, qi, lambda j, c: fold(j, c, False), init)
    _, l_run, acc = fold(qi, carry, True)
    l = jnp.sum(l_run, axis=-1, keepdims=True)

    lam =(jnp.exp(jnp.sum(lq1_ref[0] * lk1_ref[0], axis=-1, keepdims=True))
           - jnp.exp(jnp.sum(lq2_ref[0] * lk2_ref[0], axis=-1, keepdims=True)) + lambda_init)
    o = acc / l
    od = o[:tq] - lam * o[tq:]
    y = od * lax.rsqrt(jnp.mean(od * od, axis=-1, keepdims=True) + RMS_EPS) * g_ref[0]
    o_ref[0] = (y * (1.0 - lambda_init)).astype(o_ref.dtype)


def _diff_attention(z, lq1, lk1, lq2, lk2, subln_g, layer, lambda_init):
    b, t, _ = z.shape
    heads = GROUP // HEAD
    tq = min(t, 512)
    q0, k0, v0 = 1 * heads, 2 * heads, 3 * heads

    def vec(n):
        return pl.BlockSpec((1, 1, n), lambda bi, h, i: (layer, 0, 0))

    return pl.pallas_call(
        functools.partial(_diff_kernel, tq=tq, lambda_init=lambda_init),
        grid=(b, heads, t // tq),
        in_specs=[
            pl.BlockSpec((1, tq, HEAD), lambda bi, h, i: (bi, i, q0 + h)),
            pl.BlockSpec((1, t, HEAD), lambda bi, h, i: (bi, 0, k0 + h)),
            pl.BlockSpec((1, t, HEAD), lambda bi, h, i: (bi, 0, v0 + h)),
            vec(DIFF_QK), vec(DIFF_QK), vec(DIFF_QK), vec(DIFF_QK), vec(HEAD),
        ],
        out_specs=pl.BlockSpec((1, tq, HEAD), lambda bi, h, i: (bi, i, h)),
        out_shape=jax.ShapeDtypeStruct((b, t, GROUP), BF16),
        compiler_params=_params("arbitrary", "arbitrary", "arbitrary"),
        name="diff_attention",
    )(z, z, z, lq1, lk1, lq2, lk2, subln_g)


def _dilated_kernel(q_ref, k_ref, v_ref, o_ref, qf, kf, vf, o0, o1, o2, e0, e1, e2, *, t, group):
    qf[...] = q_ref[0].astype(F32)
    kf[...] = k_ref[0].astype(F32)
    vf[...] = v_ref[0].astype(F32)
    band = DIL_BAND
    a_idx = lax.broadcasted_iota(I32, (band, 2 * band), 0)
    b_idx = lax.broadcasted_iota(I32, (band, 2 * band), 1)
    in_band = (b_idx >= a_idx) & (b_idx <= a_idx + band)
    outs = ((o0, e0), (o1, e1), (o2, e2))

    for (window, dil), (o_scr, e_scr) in zip(DILATED_PAIRS, outs):
        nb = t // (dil * band)

        def one_block(it, dil=dil, nb=nb, o_scr=o_scr, e_scr=e_scr):
            r = it // nb
            n = it % nb
            start = r + n * (band * dil)
            prev = r + jnp.maximum(n - 1, 0) * (band * dil)

            def rows(ref, s0):
                return ref[pl.ds(s0, band, stride=dil), :] if dil > 1 else ref[pl.ds(s0, band), :]

            qb = rows(qf, start).astype(BF16)
            kk = jnp.concatenate([rows(kf, prev), rows(kf, start)], axis=0).astype(BF16)
            vv = jnp.concatenate([rows(vf, prev), rows(vf, start)], axis=0).astype(BF16)
            s = lax.dot_general(qb, kk, (((1,), (1,)), ((), ())), preferred_element_type=F32)
            valid = in_band & (b_idx >= jnp.where(n > 0, 0, band))
            s = jnp.where(valid, s, NEG_INF)
            m = jnp.max(s, axis=-1, keepdims=True)
            p = jnp.exp2(s - m)
            l = jnp.sum(p, axis=-1, keepdims=True)
            o = jnp.dot(p.astype(BF16), vv, preferred_element_type=F32) / l
            idx = pl.ds(start, band, stride=dil) if dil > 1 else pl.ds(start, band)
            o_scr[idx, :] = o
            e_scr[idx, :] = jnp.broadcast_to(m + jnp.log2(l), (band, HEAD))

        def blocks(g, carry, one_block=one_block):
            for u in range(group):
                one_block(g * group + u)
            return carry

        lax.fori_loop(0, dil * nb // group, blocks, 0)

    top = jnp.maximum(jnp.maximum(e0[...], e1[...]), e2[...])
    w0 = jnp.exp2(e0[...] - top)
    w1 = jnp.exp2(e1[...] - top)
    w2 = jnp.exp2(e2[...] - top)
    mix = (w0 * o0[...] + w1 * o1[...] + w2 * o2[...]) / (w0 + w1 + w2)
    o_ref[0] = mix.astype(o_ref.dtype)


def _dilated_attention(z):
    b, t, _ = z.shape
    heads = GROUP // HEAD
    q0, k0, v0 = 4 * heads, 5 * heads, 6 * heads
    group = 8
    assert all(w // dl == DIL_BAND and t % w == 0 for w, dl in DILATED_PAIRS)
    assert (t // DIL_BAND) % group == 0

    def col(c0):
        return pl.BlockSpec((1, t, HEAD), lambda bi, h: (bi, 0, c0 + h))

    return pl.pallas_call(
        functools.partial(_dilated_kernel, t=t, group=group),
        grid=(b, heads),
        in_specs=[col(q0), col(k0), col(v0)],
        out_specs=pl.BlockSpec((1, t, HEAD), lambda bi, h: (bi, 0, h)),
        out_shape=jax.ShapeDtypeStruct((b, t, GROUP), BF16),
        scratch_shapes=[pltpu.VMEM((t, HEAD), F32)] * 9,
        compiler_params=_params("arbitrary", "arbitrary"),
        name="dilated_attention",
    )(z, z, z)


def _pack_rows(h, out_ref, row0=0):
    m, half = h.shape[0], h.shape[1] // 2
    bits = pltpu.bitcast(h.astype(BF16).astype(F32), U32)
    word = (bits[:, half:] & jnp.uint32(0xFFFF0000)) | (bits[:, :half] >> 16)
    n_slab = half // LANES
    for s in range(n_slab):
        out_ref[pl.ds(row0 * n_slab + s, m, stride=n_slab), :] = word[:, s * LANES:(s + 1) * LANES]


def _unpack_rows(word):
    return pltpu.bitcast(word << 16, F32), pltpu.bitcast(word & jnp.uint32(0xFFFF0000), F32)


def _outproj_router_kernel(yp_ref, yd_ref, yc_ref, yv_ref, wo_ref, x_ref, g1_ref, n2_ref, sc_ref, sh_ref,
                           rw_ref, rb_ref,
                           x1_ref, h2_ref, idx_ref, rank_ref, wts_ref, cnt_ref, carry_scr, *, tm):
    first = (pl.program_id(0) == 0) & (pl.program_id(1) == 0)

    @pl.when(first)
    def _():
        carry_scr[...] = jnp.zeros_like(carry_scr)

    mix = jnp.dot(yp_ref[0], wo_ref[0 * GROUP:1 * GROUP, :], preferred_element_type=F32)
    mix = mix + jnp.dot(yd_ref[0], wo_ref[1 * GROUP:2 * GROUP, :], preferred_element_type=F32)
    mix = mix + jnp.dot(yc_ref[0], wo_ref[2 * GROUP:3 * GROUP, :], preferred_element_type=F32)
    mix = mix + jnp.dot(yv_ref[0], wo_ref[3 * GROUP:4 * GROUP, :], preferred_element_type=F32)
    x1 = x_ref[0] + g1_ref[0] * mix
    x1_ref[0] = x1

    y = x1 * lax.rsqrt(jnp.mean(x1 * x1, axis=-1, keepdims=True) + RMS_EPS) * n2_ref[0]
    h2 = y * (1.0 + sc_ref[0]) + sh_ref[0]
    _pack_rows(h2, h2_ref)

    logits = lax.dot_general(rw_ref[0], h2, (((1,), (1,)), ((), ())), precision=HIGHEST,
                             preferred_element_type=F32) + rb_ref[0]
    e_idx = lax.broadcasted_iota(I32, (N_EXPERTS, tm), 0)
    work = logits
    vals, sels, hots = [], [], []
    for _ in range(TOP_K):
        mx = jnp.max(work, axis=0, keepdims=True)
        sel = jnp.min(jnp.where(work == mx, e_idx, N_EXPERTS), axis=0, keepdims=True)
        hot = e_idx == sel
        vals.append(mx)
        sels.append(sel)
        hots.append(hot)
        work = jnp.where(hot, -jnp.inf, work)
    exps = [jnp.exp(v - vals[0]) for v in vals]
    denom = exps[0] + exps[1] + exps[2] + exps[3]

    chosen = jnp.zeros((N_EXPERTS, tm), F32)
    for hot in hots:
        chosen = chosen + hot.astype(F32)
    s_idx = lax.broadcasted_iota(I32, (tm, tm), 0)
    t_idx = lax.broadcasted_iota(I32, (tm, tm), 1)
    upper = (s_idx < t_idx).astype(BF16)
    before = jnp.dot(chosen.astype(BF16), upper, preferred_element_type=F32) + carry_scr[:, 0:1]
    for k in range(TOP_K):
        idx_ref[k:k + 1, :] = sels[k]
        rank_ref[k:k + 1, :] = jnp.sum(jnp.where(hots[k], before, 0.0), axis=0, keepdims=True).astype(I32)
        wts_ref[k:k + 1, :] = exps[k] / denom
    carry_scr[...] = carry_scr[...] + jnp.sum(chosen, axis=1, keepdims=True)
    cnt_ref[...] = carry_scr[...]


def _outproj_router(ys, w_out_bf16, x, gate1, norm2_g, scale2, shift2, router_wt, router_b, layer):
    b, t, d = x.shape
    tm = min(t, 512)
    n = b * t
    nt = t // tm
    slab = d // 2 // LANES

    def ytile():
        return pl.BlockSpec((1, tm, GROUP), lambda bi, i: (bi, i, 0))

    def bvec():
        return pl.BlockSpec((1, 1, d), lambda bi, i: (bi, 0, 0))

    tok = pl.BlockSpec((TOP_K, tm), lambda bi, i: (0, bi * nt + i))
    outs = pl.pallas_call(
        functools.partial(_outproj_router_kernel, tm=tm),
        grid=(b, nt),
        in_specs=[
            ytile(), ytile(), ytile(), ytile(),
            pl.BlockSpec((None, 4 * GROUP, d), lambda bi, i: (layer, 0, 0)),
            pl.BlockSpec((1, tm, d), lambda bi, i: (bi, i, 0)),
            bvec(),
            pl.BlockSpec((1, 1, d), lambda bi, i: (layer, 0, 0)),
            bvec(), bvec(),
            pl.BlockSpec((1, N_EXPERTS, d), lambda bi, i: (layer, 0, 0)),
            pl.BlockSpec((1, N_EXPERTS, 1), lambda bi, i: (layer, 0, 0)),
        ],
        out_specs=[
            pl.BlockSpec((1, tm, d), lambda bi, i: (bi, i, 0)),
            pl.BlockSpec((tm * slab, LANES), lambda bi, i: (bi * nt + i, 0)),
            tok, tok, tok,
            pl.BlockSpec((N_EXPERTS, LANES), lambda bi, i: (0, 0)),
        ],
        out_shape=[
            jax.ShapeDtypeStruct((b, t, d), F32),
            jax.ShapeDtypeStruct((n * slab, LANES), U32),
            jax.ShapeDtypeStruct((TOP_K, n), I32),
            jax.ShapeDtypeStruct((TOP_K, n), I32),
            jax.ShapeDtypeStruct((TOP_K, n), F32),
            jax.ShapeDtypeStruct((N_EXPERTS, LANES), F32),
        ],
        scratch_shapes=[pltpu.VMEM((N_EXPERTS, LANES), F32)],
        compiler_params=_params("arbitrary", "arbitrary"),
        name="outproj_router",
    )(*ys, w_out_bf16, x, gate1, norm2_g, scale2, shift2, router_wt, router_b)
    return outs


def _scatter_kernel(zs_ref, pos_ref, h_ref, xs_hbm, zbuf, zsem, sem, *, tm, tm_e):
    zrows = zbuf.shape[0]

    @pl.when(pl.program_id(0) == 0)
    def _():
        zbuf[...] = jnp.zeros_like(zbuf)

        def zero(e, c):
            @pl.when(zs_ref[e] >= 0)
            def _():
                for q in range(tm_e // zrows):
                    pltpu.make_async_copy(zbuf, xs_hbm.at[pl.ds(zs_ref[e] + q * zrows, zrows)], zsem).start()
            return c

        lax.fori_loop(0, N_EXPERTS, zero, 0)

        def zero_done(e, c):
            @pl.when(zs_ref[e] >= 0)
            def _():
                for q in range(tm_e // zrows):
                    pltpu.make_async_copy(zbuf, xs_hbm.at[pl.ds(0, zrows)], zsem).wait()
            return c

        lax.fori_loop(0, N_EXPERTS, zero_done, 0)

    def issue(g, c):
        for u in range(ISSUE_UNROLL):
            tt = g * ISSUE_UNROLL + u
            for k in range(TOP_K):
                pltpu.make_async_copy(h_ref.at[tt], xs_hbm.at[pos_ref[k, tt]], sem).start(priority=k % 2)
        return c

    lax.fori_loop(0, tm // ISSUE_UNROLL, issue, 0)
    for k in range(TOP_K):
        pltpu.make_async_copy(h_ref, xs_hbm.at[pl.ds(0, tm)], sem).wait()


def _scatter_rows(zero_start, pos, h_rows, m_pad, tm_e):
    n, slab, _ = h_rows.shape
    tm = min(n, 256)
    zrows = min(tm_e, 256)
    grid_spec = pltpu.PrefetchScalarGridSpec(
        num_scalar_prefetch=1,
        grid=(n // tm,),
        in_specs=[
            pl.BlockSpec((TOP_K, tm), lambda i, zs: (0, i), memory_space=pltpu.SMEM),
            pl.BlockSpec((tm, slab, LANES), lambda i, zs: (i, 0, 0)),
        ],
        out_specs=pl.BlockSpec(memory_space=pl.ANY),
        scratch_shapes=[pltpu.VMEM((zrows, slab, LANES), U32), pltpu.SemaphoreType.DMA(()),
                        pltpu.SemaphoreType.DMA(())],
    )
    return pl.pallas_call(
        functools.partial(_scatter_kernel, tm=tm, tm_e=tm_e),
        grid_spec=grid_spec,
        out_shape=jax.ShapeDtypeStruct((m_pad, slab, LANES), U32),
        compiler_params=_params("arbitrary"),
        name="scatter_rows",
    )(zero_start, pos, h_rows)


def _deinterleave(hh):
    m, width = hh.shape
    lane = lax.broadcasted_iota(I32, (m, LANES), 1)
    low = lane < LANES // 2
    evens_then_odds = jnp.where(low, 2 * lane, 2 * lane - (LANES - 1))
    parts = [jnp.take_along_axis(hh[:, b * LANES:(b + 1) * LANES], evens_then_odds, axis=1)
             for b in range(width // LANES)]
    gates, lins = [], []
    for b in range(0, len(parts), 2):
        first, second = parts[b], parts[b + 1]
        gates.append(jnp.where(low, first, pltpu.roll(second, LANES // 2, 1)))
        lins.append(jnp.where(low, pltpu.roll(first, LANES // 2, 1), second))
    return jnp.concatenate(gates, axis=1), jnp.concatenate(lins, axis=1)


def _expert_kernel(te_ref, tv_ref, nu_ref, xs_ref, w1_ref, b1_ref, w2_ref, b2_ref, ys_ref,
                   x_scr, acc_scr, *, tm, sub, nc):
    i = pl.program_id(0)
    c = pl.program_id(1)
    d = x_scr.shape[1]
    n_slab = d // 2 // LANES
    live = i < nu_ref[0]

    @pl.when(live)
    def _():
        @pl.when(c == 0)
        def _():
            for s in range(n_slab):
                lo, hi = _unpack_rows(xs_ref[pl.ds(s, tm, stride=n_slab), :])
                x_scr[:, s * LANES:(s + 1) * LANES] = lo.astype(BF16)
                x_scr[:, d // 2 + s * LANES:d // 2 + (s + 1) * LANES] = hi.astype(BF16)
            acc_scr[...] = jnp.zeros_like(acc_scr) + b2_ref[...]

        def ffn_rows(n_rows):
            rows = slice(0, n_rows)
            hh = jnp.dot(x_scr[rows, :], w1_ref[...].astype(BF16), preferred_element_type=F32) + b1_ref[...]
            g, lin = _deinterleave(hh)
            g = jnp.minimum(g, SWIGLU_LIMIT)
            lin = jnp.clip(lin, -SWIGLU_LIMIT, SWIGLU_LIMIT)
            act = g * _sigmoid(SWIGLU_ALPHA * g) * (lin + 1.0)
            acc_scr[rows, :] += jnp.dot(act.astype(BF16), w2_ref[...].astype(BF16), preferred_element_type=F32)

        filled = (tv_ref[i] + sub - 1) // sub
        for nb in range(1, tm // sub + 1):
            @pl.when(filled == nb)
            def _(nb=nb):
                ffn_rows(nb * sub)

        @pl.when(c == nc - 1)
        def _():
            for sb in range(tm // sub):
                _pack_rows(acc_scr[sb * sub:(sb + 1) * sub, :], ys_ref, row0=sb * sub)

    @pl.when(jnp.logical_not(live) & (c == 0))
    def _():
        ys_ref[...] = jnp.zeros_like(ys_ref)


def _experts(tile_expert, tile_valid, n_used, xs2d, w1, b1, w2, b2, layer, tm, sub, m_pad):
    _, n_exp, d, h2 = w1.shape
    hid = h2 // 2
    tc = min(hid, 512)
    nc = hid // tc
    n_tiles = m_pad // tm
    slab = d // 2 // LANES

    def live(i, nu):
        return jnp.minimum(i, nu[0] - 1)

    def chunk(i, c, nu):
        return jnp.where(i < nu[0], c, nc - 1)

    grid_spec = pltpu.PrefetchScalarGridSpec(
        num_scalar_prefetch=3,
        grid=(n_tiles, nc),
        in_specs=[
            pl.BlockSpec((tm * slab, LANES), lambda i, c, te, tv, nu: (live(i, nu), 0)),
            pl.BlockSpec((None, None, d, 2 * tc), lambda i, c, te, tv, nu: (layer, te[i], 0, chunk(i, c, nu))),
            pl.BlockSpec((None, None, 1, 2 * tc), lambda i, c, te, tv, nu: (layer, te[i], 0, chunk(i, c, nu))),
            pl.BlockSpec((None, None, tc, d), lambda i, c, te, tv, nu: (layer, te[i], chunk(i, c, nu), 0)),
            pl.BlockSpec((None, None, 1, d), lambda i, c, te, tv, nu: (layer, te[i], 0, 0)),
        ],
        out_specs=pl.BlockSpec((tm * slab, LANES), lambda i, c, te, tv, nu: (i, 0)),
        scratch_shapes=[pltpu.VMEM((tm, d), BF16), pltpu.VMEM((tm, d), F32)],
    )
    return pl.pallas_call(
        functools.partial(_expert_kernel, tm=tm, sub=sub, nc=nc),
        grid_spec=grid_spec,
        out_shape=jax.ShapeDtypeStruct((m_pad * slab, LANES), U32),
        compiler_params=_params("arbitrary", "arbitrary"),
        name="experts",
    )(tile_expert, tile_valid, n_used, xs2d, w1, b1, w2, b2)


def _combine_kernel(pos_ref, pos_next_ref, wts_ref, x_ref, g2_ref, fg_ref, ys_hbm, ys_flat_hbm, o_ref, buf, sem,
                    *, tm, final, n_steps):
    i = pl.program_id(0)
    n_slab = x_ref.shape[1] // 2 // LANES
    slot_rows = TOP_K * tm * n_slab
    slot = i % 2
    base = pl.multiple_of(slot * slot_rows, slot_rows)

    def issue_block(p_ref, to_slot):
        to_base = to_slot * slot_rows

        def issue(g, c):
            for u in range(ISSUE_UNROLL):
                tt = g * ISSUE_UNROLL + u
                for k in range(TOP_K):
                    row0 = pl.multiple_of(to_base + (k * tm + tt) * n_slab, n_slab)
                    pltpu.make_async_copy(ys_hbm.at[p_ref[k, tt]], buf.at[pl.ds(row0, n_slab)],
                                          sem.at[to_slot]).start(priority=k % 2)
            return c

        lax.fori_loop(0, tm // ISSUE_UNROLL, issue, 0)

    @pl.when(i == 0)
    def _():
        issue_block(pos_ref, 0)

    @pl.when(i + 1 < n_steps)
    def _():
        issue_block(pos_next_ref, 1 - slot)

    pltpu.make_async_copy(ys_flat_hbm.at[pl.ds(0, slot_rows)], buf.at[pl.ds(base, slot_rows)], sem.at[slot]).wait()

    w_sq = jnp.concatenate([wts_ref[...], jnp.zeros((tm - TOP_K, tm), F32)], axis=0)
    w_t = w_sq.T
    w_k = [jnp.broadcast_to(w_t[:, k:k + 1], (tm, LANES)) for k in range(TOP_K)]
    lows, highs = [], []
    for s in range(n_slab):
        acc_lo = jnp.zeros((tm, LANES), F32)
        acc_hi = jnp.zeros((tm, LANES), F32)
        for k in range(TOP_K):
            lo, hi = _unpack_rows(buf[pl.ds(base + k * tm * n_slab + s, tm, stride=n_slab), :])
            acc_lo = acc_lo + w_k[k] * lo
            acc_hi = acc_hi + w_k[k] * hi
        lows.append(acc_lo)
        highs.append(acc_hi)
    moe = jnp.concatenate(lows + highs, axis=1)
    x2 = x_ref[...] + g2_ref[0] * moe
    if final:
        x2 = x2 * lax.rsqrt(jnp.mean(x2 * x2, axis=-1, keepdims=True) + RMS_EPS) * fg_ref[...]
    o_ref[...] = x2


def _combine(pos, wts, x1, gate2, final_g, ys_rows, tokens_per_batch, final):
    n, d = x1.shape
    tm = 256
    steps_per_batch = tokens_per_batch // tm
    n_steps = n // tm
    m_pad, slab, _ = ys_rows.shape
    return pl.pallas_call(
        functools.partial(_combine_kernel, tm=tm, final=final, n_steps=n_steps),
        grid=(n_steps,),
        in_specs=[
            pl.BlockSpec((TOP_K, tm), lambda i: (0, i), memory_space=pltpu.SMEM),
            pl.BlockSpec((TOP_K, tm), lambda i: (0, jnp.minimum(i + 1, n_steps - 1)), memory_space=pltpu.SMEM),
            pl.BlockSpec((TOP_K, tm), lambda i: (0, i)),
            pl.BlockSpec((tm, d), lambda i: (i, 0)),
            pl.BlockSpec((1, 1, d), lambda i: (i // steps_per_batch, 0, 0)),
            pl.BlockSpec((1, d), lambda i: (0, 0)),
            pl.BlockSpec(memory_space=pl.ANY),
            pl.BlockSpec(memory_space=pl.ANY),
        ],
        out_specs=pl.BlockSpec((tm, d), lambda i: (i, 0)),
        out_shape=jax.ShapeDtypeStruct((n, d), F32),
        scratch_shapes=[pltpu.VMEM((2 * TOP_K * tm * slab, LANES), U32), pltpu.SemaphoreType.DMA((2,))],
        compiler_params=_params("arbitrary"),
        name="combine",
    )(pos, pos, wts, x1, gate2, final_g, ys_rows, ys_rows.reshape(m_pad * slab, LANES))


def _routing_plan(idx, rank, counts, tm_e, n_tiles):
    experts = jnp.arange(N_EXPERTS, dtype=I32)
    cnt = counts[:, 0].astype(I32)
    tiles = (cnt + tm_e - 1) // tm_e
    tile_end = jnp.cumsum(tiles)
    tile_start = tile_end - tiles
    offsets = tile_start * tm_e
    pos = rank + jnp.sum(jnp.where(idx[..., None] == experts, offsets, 0), axis=-1)
    n_used = tile_end[-1]
    tile_ids = jnp.arange(n_tiles, dtype=I32)
    te = jnp.sum((tile_ids[:, None] >= tile_end[None, :]).astype(I32), axis=1)
    te_last = jnp.sum((n_used - 1 >= tile_end).astype(I32))
    te = jnp.where(tile_ids < n_used, te, te_last).astype(I32)
    mine = te[:, None] == experts[None, :]
    rows_left = jnp.sum(jnp.where(mine, cnt[None, :] - (tile_ids[:, None] - tile_start[None, :]) * tm_e, 0), axis=1)
    tile_valid = jnp.where(tile_ids < n_used, jnp.clip(rows_left, 0, tm_e), 0).astype(I32)
    zero_start = jnp.where(tiles > 0, (tile_end - 1) * tm_e, -1).astype(I32)
    return pos.astype(I32), te, tile_valid, n_used.reshape(1).astype(I32), zero_start


def kernel(x, c, positions, mod_w, mod_b, norm1_g, norm2_g, w_in, pool_w, pool_scale, diff_lq1, diff_lk1,
           diff_lq2, diff_lk2, diff_subln_g, conv_dw_w, conv_dw_b, conv_ln_g, conv_ln_b, conv_pw_w, conv_pw_b,
           w_out, router_w, router_b, exp_w1, exp_b1, exp_w2, exp_b2, final_g):
    b, t, d = x.shape
    depth = mod_w.shape[0]
    n = b * t
    tm_e = 1024
    sub_e = 512
    m_pad = n * TOP_K + N_EXPERTS * tm_e
    n_tiles = m_pad // tm_e
    slab = d // 2 // LANES

    def row3(a):
        return a.reshape(a.shape[0], 1, a.shape[1])

    mod = _modulation(c, mod_w, mod_b)
    tables = _rope_tables(positions)
    w_in_b = w_in.astype(BF16)
    w_out_b = w_out.astype(BF16)
    pool_w_b = pool_w.astype(BF16)
    pw_w_b = conv_pw_w.astype(BF16)
    router_wt = jnp.swapaxes(router_w, 1, 2)
    router_b3 = router_b.reshape(depth, N_EXPERTS, 1)
    b1r = exp_b1.reshape(depth, N_EXPERTS, 1, exp_b1.shape[-1])
    b2r = exp_b2.reshape(depth, N_EXPERTS, 1, d)
    final_g2 = final_g.reshape(1, d)

    for l in range(depth):
        lambda_init = 0.8 - 0.6 * math.exp(-0.3 * l)
        sh1, sc1, g1, sh2, sc2, g2 = [m.reshape(b, 1, d) for m in jnp.split(mod[l], 6, axis=-1)]
        z = _inproj(x, row3(norm1_g), sc1, sh1, w_in_b, tables, l)
        y_pool, y_conv = _poolconv(z, pool_w_b, row3(pool_scale), conv_dw_w, row3(conv_dw_b), row3(conv_ln_g),
                                   row3(conv_ln_b), pw_w_b, row3(conv_pw_b), l)
        y_diff = _diff_attention(z, row3(diff_lq1), row3(diff_lk1), row3(diff_lq2), row3(diff_lk2),
                                 row3(diff_subln_g), l, lambda_init)
        y_dil = _dilated_attention(z)
        x1, h2, idx, rank, wts, counts = _outproj_router(
            (y_pool, y_diff, y_dil, y_conv), w_out_b, x, g1, row3(norm2_g), sc2, sh2, router_wt, router_b3, l)
        pos, tile_expert, tile_valid, n_used, zero_start = _routing_plan(idx, rank, counts, tm_e, n_tiles)
        xs = _scatter_rows(zero_start, pos, h2.reshape(n, slab, LANES), m_pad, tm_e)
        ys = _experts(tile_expert, tile_valid, n_used, xs.reshape(m_pad * slab, LANES), exp_w1, b1r, exp_w2, b2r,
                      l, tm_e, sub_e, m_pad)
        x = _combine(pos, wts, x1.reshape(n, d), g2, final_g2, ys.reshape(m_pad, slab, LANES), t,
                     final=(l == depth - 1)).reshape(b, t, d)
    return x
```

```python
import functools
import math

import numpy as np
import jax
import jax.numpy as jnp
from jax import lax
from jax.experimental import pallas as pl
from jax.experimental.pallas import tpu as pltpu

F32 = jnp.float32
BF16 = jnp.bfloat16
I32 = jnp.int32
U32 = jnp.uint32
HIGHEST = lax.Precision.HIGHEST

LANES = 128
SUBLANES = 8
VMEM_LIMIT_BYTES = 60 * 1024 * 1024

POOL_WINDOWS = (2, 4, 8, 16)
DILATED_PAIRS = ((128, 1), (512, 4), (2048, 16))
DIL_BAND = 128
CONV_WIDTH = 31
HALO = 32
ISSUE_UNROLL = 8
N_EXPERTS = 32
TOP_K = 4
SWIGLU_ALPHA = 1.702
SWIGLU_LIMIT = 7.0
ROPE_THETA = 500000.0
RMS_EPS = 1e-6
LN_EPS = 1e-5
NEG_INF = -1e30
GROUP = 512
HEAD = 128
DIFF_QK = 64
LOG2E = math.log2(math.e)


def _params(*semantics):
    return pltpu.CompilerParams(dimension_semantics=semantics, vmem_limit_bytes=VMEM_LIMIT_BYTES)


def _sigmoid(x):
    return 1.0 / (1.0 + jnp.exp(-x))


def _mod_kernel(c_ref, w_ref, b_ref, o_ref):
    c = c_ref[...]
    ca = c * _sigmoid(c)
    o_ref[0] = jnp.dot(ca, w_ref[0], precision=HIGHEST, preferred_element_type=F32) + b_ref[0]


def _modulation(c, mod_w, mod_b):
    depth, d, n = mod_w.shape
    b = c.shape[0]
    rows = 8
    c_pad = jnp.zeros((rows, d), F32).at[:b].set(c)
    tn = 512
    out = pl.pallas_call(
        _mod_kernel,
        grid=(depth, n // tn),
        in_specs=[
            pl.BlockSpec((rows, d), lambda l, j: (0, 0)),
            pl.BlockSpec((1, d, tn), lambda l, j: (l, 0, j)),
            pl.BlockSpec((1, 1, tn), lambda l, j: (l, 0, j)),
        ],
        out_specs=pl.BlockSpec((1, rows, tn), lambda l, j: (l, 0, j)),
        out_shape=jax.ShapeDtypeStruct((depth, rows, n), F32),
        compiler_params=_params("arbitrary", "arbitrary"),
        name="modulation",
    )(c_pad, mod_w, mod_b.reshape(depth, 1, n))
    return out[:, :b]


def _rope_table_kernel(pos_ref, invd_ref, invc_ref, cd_ref, sd_ref, cc_ref, sc_ref):
    p = pos_ref[0]
    lane = lax.broadcasted_iota(I32, (1, LANES), 1)
    for inv_ref, c_ref, s_ref, hd in ((invd_ref, cd_ref, sd_ref, DIFF_QK), (invc_ref, cc_ref, sc_ref, HEAD)):
        half = hd // 8
        lm = lane % hd
        ang = p * inv_ref[...]
        c_ref[0] = jnp.cos(ang)
        s_ref[0] = jnp.where(lm < half, -jnp.sin(ang), jnp.sin(ang))


def _lane_inv_freq(hd):
    half = hd // 8
    inv = ROPE_THETA ** (-jnp.arange(half, dtype=F32) / half)
    lm = np.arange(LANES) % hd
    rotated = jnp.asarray(lm < 2 * half)
    return jnp.where(rotated, inv[lm % half], 0.0).reshape(1, LANES).astype(F32)


def _rope_tables(positions):
    b, t = positions.shape
    tm = min(t, 1024)
    pos = positions.astype(F32).reshape(b, t, 1)
    spec_t = pl.BlockSpec((1, tm, LANES), lambda bi, i: (bi, i, 0))
    spec_inv = pl.BlockSpec((1, LANES), lambda bi, i: (0, 0))
    shp = jax.ShapeDtypeStruct((b, t, LANES), F32)
    return pl.pallas_call(
        _rope_table_kernel,
        grid=(b, t // tm),
        in_specs=[pl.BlockSpec((1, tm, 1), lambda bi, i: (bi, i, 0)), spec_inv, spec_inv],
        out_specs=[spec_t] * 4,
        out_shape=[shp] * 4,
        compiler_params=_params("arbitrary", "arbitrary"),
        name="rope_tables",
    )(pos, _lane_inv_freq(DIFF_QK), _lane_inv_freq(HEAD))


def _rope_apply(z, cos, sin, hd):
    half = hd // 8
    lane = lax.broadcasted_iota(I32, (1, LANES), 1)
    first = (lane % hd) < half
    outs = []
    for cb in range(z.shape[1] // LANES):
        zc = z[:, cb * LANES:(cb + 1) * LANES]
        partner = jnp.where(first, pltpu.roll(zc, LANES - half, 1), pltpu.roll(zc, half, 1))
        outs.append(zc * cos + partner * sin)
    return jnp.concatenate(outs, axis=1)


def _inproj_kernel(x_ref, g_ref, sc_ref, sh_ref, w_ref, cd_ref, sd_ref, cc_ref, sc2_ref, z_ref, h_scr):
    j = pl.program_id(2)

    def project():
        return jnp.dot(h_scr[...], w_ref[...], preferred_element_type=F32)

    @pl.when(j == 0)
    def _():
        x = x_ref[0]
        y = x * lax.rsqrt(jnp.mean(x * x, axis=-1, keepdims=True) + RMS_EPS) * g_ref[0]
        h = (y * (1.0 + sc_ref[0]) + sh_ref[0]).astype(BF16)
        h_scr[...] = h
        z_ref[0] = jnp.dot(h, w_ref[...], preferred_element_type=F32).astype(z_ref.dtype)

    @pl.when((j == 1) | (j == 2))
    def _():
        r = _rope_apply(project(), cd_ref[0], sd_ref[0], DIFF_QK)
        r = r * jnp.where(j == 1, DIFF_QK ** -0.5 * LOG2E, 1.0)
        z_ref[0] = r.astype(z_ref.dtype)

    @pl.when((j == 4) | (j == 5))
    def _():
        r = _rope_apply(project(), cc_ref[0], sc2_ref[0], HEAD)
        r = r * jnp.where(j == 4, HEAD ** -0.5 * LOG2E, 1.0)
        z_ref[0] = r.astype(z_ref.dtype)

    @pl.when((j == 3) | (j >= 6))
    def _():
        z_ref[0] = project().astype(z_ref.dtype)


def _inproj(x, norm_g, scale, shift, w_in_bf16, tables, layer):
    b, t, d = x.shape
    n = w_in_bf16.shape[-1]
    tm = min(t, 1024)
    tn = GROUP
    cd, sd, cc, sc = tables
    spec_tab = pl.BlockSpec((1, tm, LANES), lambda bi, i, j: (bi, i, 0))
    spec_vec = pl.BlockSpec((1, 1, d), lambda bi, i, j: (bi, 0, 0))
    return pl.pallas_call(
        _inproj_kernel,
        grid=(b, t // tm, n // tn),
        in_specs=[
            pl.BlockSpec((1, tm, d), lambda bi, i, j: (bi, i, 0)),
            pl.BlockSpec((1, 1, d), lambda bi, i, j: (layer, 0, 0)),
            spec_vec, spec_vec,
            pl.BlockSpec((None, d, tn), lambda bi, i, j: (layer, 0, j)),
            spec_tab, spec_tab, spec_tab, spec_tab,
        ],
        out_specs=pl.BlockSpec((1, tm, tn), lambda bi, i, j: (bi, i, j)),
        out_shape=jax.ShapeDtypeStruct((b, t, n), BF16),
        scratch_shapes=[pltpu.VMEM((tm, d), BF16)],
        compiler_params=_params("arbitrary", "arbitrary", "arbitrary"),
        name="inproj",
    )(x, norm_g, scale, shift, w_in_bf16, cd, sd, cc, sc)


def _poolconv_kernel(zp_ref, zph_ref, za_ref, zah_ref, zg_ref, zgh_ref,
                     pw_ref, ps_ref, dww_ref, dwb_ref, lng_ref, lnb_ref, pww_ref, pwb_ref,
                     yp_ref, yc_ref, xp_scr, u_scr, ush_scr, *, tm):
    i = pl.program_id(1)
    keep = jnp.where(i == 0, 0.0, 1.0)

    xp_scr[0:HALO, :] = zph_ref[0].astype(F32) * keep
    xp_scr[HALO:, :] = zp_ref[0].astype(F32)
    t_glob = i * tm + lax.broadcasted_iota(I32, (tm, 1), 0)
    for g, w in enumerate(POOL_WINDOWS):
        cols = slice(g * LANES, (g + 1) * LANES)
        xg = xp_scr[HALO:HALO + tm, cols]
        acc = xg
        for k in range(1, w):
            acc = acc + xp_scr[HALO - k:HALO - k + tm, cols]
        cnt = jnp.minimum(t_glob + 1, w).astype(F32)
        pooled = acc / cnt - xg
        yg = jnp.dot(pooled.astype(BF16), pw_ref[0, g], preferred_element_type=F32)
        yp_ref[0, :, cols] = (yg * ps_ref[0, :, cols]).astype(yp_ref.dtype)

    ah = zah_ref[0].astype(F32)
    gh = zgh_ref[0].astype(F32)
    u_scr[0:HALO, :] = ah * _sigmoid(gh) * keep
    a = za_ref[0].astype(F32)
    gg = zg_ref[0].astype(F32)
    u_scr[HALO:, :] = a * _sigmoid(gg)
    span = tm + HALO - SUBLANES
    for sh in range(1, SUBLANES):
        ush_scr[sh - 1] = u_scr[sh:sh + span, :]
    acc = jnp.zeros((tm, GROUP), F32) + dwb_ref[0]
    base = HALO - (CONV_WIDTH - 1)
    for k in range(CONV_WIDTH):
        off = base + k
        sh, start = off % SUBLANES, off - off % SUBLANES
        window = u_scr[start:start + tm, :] if sh == 0 else ush_scr[sh - 1, start:start + tm, :]
        acc = acc + window * dww_ref[0, k:k + 1, :]
    mu = jnp.mean(acc, axis=-1, keepdims=True)
    cen = acc - mu
    var = jnp.mean(cen * cen, axis=-1, keepdims=True)
    v = cen * lax.rsqrt(var + LN_EPS) * lng_ref[0] + lnb_ref[0]
    v = v * _sigmoid(v)
    y = jnp.dot(v.astype(BF16), pww_ref[0], preferred_element_type=F32) + pwb_ref[0]
    yc_ref[0] = y.astype(yc_ref.dtype)


def _poolconv(z, pool_w_bf16, pool_scale, dw_w, dw_b, ln_g, ln_b, pw_w_bf16, pw_b, layer):
    b, t, _ = z.shape
    tm = min(t, 512)
    r = tm // HALO
    a_blk = 7
    g_blk = 8

    def cur(col):
        return pl.BlockSpec((1, tm, GROUP), lambda bi, i: (bi, i, col))

    def halo(col):
        return pl.BlockSpec((1, HALO, GROUP), lambda bi, i: (bi, jnp.maximum(i * r - 1, 0), col))

    def vec(n):
        return pl.BlockSpec((1, 1, n), lambda bi, i: (layer, 0, 0))

    out_spec = pl.BlockSpec((1, tm, GROUP), lambda bi, i: (bi, i, 0))
    shp = jax.ShapeDtypeStruct((b, t, GROUP), BF16)
    return pl.pallas_call(
        functools.partial(_poolconv_kernel, tm=tm),
        grid=(b, t // tm),
        in_specs=[
            cur(0), halo(0), cur(a_blk), halo(a_blk), cur(g_blk), halo(g_blk),
            pl.BlockSpec((1, len(POOL_WINDOWS), LANES, LANES), lambda bi, i: (layer, 0, 0, 0)),
            vec(GROUP),
            pl.BlockSpec((1, CONV_WIDTH, GROUP), lambda bi, i: (layer, 0, 0)),
            vec(GROUP), vec(GROUP), vec(GROUP),
            pl.BlockSpec((1, GROUP, GROUP), lambda bi, i: (layer, 0, 0)),
            vec(GROUP),
        ],
        out_specs=[out_spec, out_spec],
        out_shape=[shp, shp],
        scratch_shapes=[pltpu.VMEM((tm + HALO, GROUP), F32), pltpu.VMEM((tm + HALO, GROUP), F32),
                        pltpu.VMEM((SUBLANES - 1, tm + HALO - SUBLANES, GROUP), F32)],
        compiler_params=_params("arbitrary", "arbitrary"),
        name="pool_conv",
    )(z, z, z, z, z, z, pool_w_bf16, pool_scale, dw_w, dw_b, ln_g, ln_b, pw_w_bf16, pw_b)


def _diff_kernel(q_ref, k_ref, v_ref, lq1_ref, lk1_ref, lq2_ref, lk2_ref, g_ref, o_ref, *, tq, lambda_init):
    qi = pl.program_id(2)
    q = q_ref[0]
    lane = lax.broadcasted_iota(I32, (tq, HEAD), 1)
    zero = jnp.zeros_like(q)
    q2 = jnp.concatenate([jnp.where(lane < DIFF_QK, q, zero), jnp.where(lane >= DIFF_QK, q, zero)], axis=0)
    rows = 2 * tq

    def lane_tiles(s):
        return [s[:, c * LANES:(c + 1) * LANES] for c in range(tq // LANES)]

    def fold(j, carry, masked):
        m, l_run, acc = carry
        kb = k_ref[0, pl.ds(pl.multiple_of(j * tq, tq), tq), :]
        vb = v_ref[0, pl.ds(pl.multiple_of(j * tq, tq), tq), :]
        s = lax.dot_general(q2, kb, (((1,), (1,)), ((), ())), preferred_element_type=F32)
        if masked:
            row = lax.broadcasted_iota(I32, (rows, tq), 0) % tq
            col = lax.broadcasted_iota(I32, (rows, tq), 1)
            s = jnp.where(col <= row, s, NEG_INF)
        tiles = lane_tiles(s)
        m_blk = tiles[0]
        for part in tiles[1:]:
            m_blk = jnp.maximum(m_blk, part)
        m_new = jnp.maximum(m, jnp.max(m_blk, axis=-1, keepdims=True))
        alpha = jnp.exp2(m - m_new)
        p = jnp.exp2(s - m_new)
        l_run = alpha * l_run
        for part in lane_tiles(p):
            l_run = l_run + part
        acc = alpha * acc + jnp.dot(p.astype(BF16), vb, preferred_element_type=F32)
        return m_new, l_run, acc

    init = (jnp.full((rows, 1), NEG_INF, F32), jnp.zeros((rows, LANES), F32), jnp.zeros((rows, HEAD), F32))
    carry = lax.fori_loop(0, qi, lambda j, c: fold(j, c, False), init)
    _, l_run, acc = fold(qi, carry, True)
    l = jnp.sum(l_run, axis=-1, keepdims=True)

    lam =(jnp.exp(jnp.sum(lq1_ref[0] * lk1_ref[0], axis=-1, keepdims=True))
           - jnp.exp(jnp.sum(lq2_ref[0] * lk2_ref[0], axis=-1, keepdims=True)) + lambda_init)
    o = acc / l
    od = o[:tq] - lam * o[tq:]
    y = od * lax.rsqrt(jnp.mean(od * od, axis=-1, keepdims=True) + RMS_EPS) * g_ref[0]
    o_ref[0] = (y * (1.0 - lambda_init)).astype(o_ref.dtype)


def _diff_attention(z, lq1, lk1, lq2, lk2, subln_g, layer, lambda_init):
    b, t, _ = z.shape
    heads = GROUP // HEAD
    tq = min(t, 512)
    q0, k0, v0 = 1 * heads, 2 * heads, 3 * heads

    def vec(n):
        return pl.BlockSpec((1, 1, n), lambda bi, h, i: (layer, 0, 0))

    return pl.pallas_call(
        functools.partial(_diff_kernel, tq=tq, lambda_init=lambda_init),
        grid=(b, heads, t // tq),
        in_specs=[
            pl.BlockSpec((1, tq, HEAD), lambda bi, h, i: (bi, i, q0 + h)),
            pl.BlockSpec((1, t, HEAD), lambda bi, h, i: (bi, 0, k0 + h)),
            pl.BlockSpec((1, t, HEAD), lambda bi, h, i: (bi, 0, v0 + h)),
            vec(DIFF_QK), vec(DIFF_QK), vec(DIFF_QK), vec(DIFF_QK), vec(HEAD),
        ],
        out_specs=pl.BlockSpec((1, tq, HEAD), lambda bi, h, i: (bi, i, h)),
        out_shape=jax.ShapeDtypeStruct((b, t, GROUP), BF16),
        compiler_params=_params("arbitrary", "arbitrary", "arbitrary"),
        name="diff_attention",
    )(z, z, z, lq1, lk1, lq2, lk2, subln_g)


def _dilated_kernel(q_ref, k_ref, v_ref, o_ref, qf, kf, vf, o0, o1, o2, e0, e1, e2, *, t, group):
    qf[...] = q_ref[0].astype(F32)
    kf[...] = k_ref[0].astype(F32)
    vf[...] = v_ref[0].astype(F32)
    band = DIL_BAND
    a_idx = lax.broadcasted_iota(I32, (band, 2 * band), 0)
    b_idx = lax.broadcasted_iota(I32, (band, 2 * band), 1)
    in_band = (b_idx >= a_idx) & (b_idx <= a_idx + band)
    outs = ((o0, e0), (o1, e1), (o2, e2))

    for (window, dil), (o_scr, e_scr) in zip(DILATED_PAIRS, outs):
        nb = t // (dil * band)

        def one_block(it, dil=dil, nb=nb, o_scr=o_scr, e_scr=e_scr):
            r = it // nb
            n = it % nb
            start = r + n * (band * dil)
            prev = r + jnp.maximum(n - 1, 0) * (band * dil)

            def rows(ref, s0):
                return ref[pl.ds(s0, band, stride=dil), :] if dil > 1 else ref[pl.ds(s0, band), :]

            qb = rows(qf, start).astype(BF16)
            kk = jnp.concatenate([rows(kf, prev), rows(kf, start)], axis=0).astype(BF16)
            vv = jnp.concatenate([rows(vf, prev), rows(vf, start)], axis=0).astype(BF16)
            s = lax.dot_general(qb, kk, (((1,), (1,)), ((), ())), preferred_element_type=F32)
            valid = in_band & (b_idx >= jnp.where(n > 0, 0, band))
            s = jnp.where(valid, s, NEG_INF)
            m = jnp.max(s, axis=-1, keepdims=True)
            p = jnp.exp2(s - m)
            l = jnp.sum(p, axis=-1, keepdims=True)
            o = jnp.dot(p.astype(BF16), vv, preferred_element_type=F32) / l
            idx = pl.ds(start, band, stride=dil) if dil > 1 else pl.ds(start, band)
            o_scr[idx, :] = o
            e_scr[idx, :] = jnp.broadcast_to(m + jnp.log2(l), (band, HEAD))

        def blocks(g, carry, one_block=one_block):
            for u in range(group):
                one_block(g * group + u)
            return carry

        lax.fori_loop(0, dil * nb // group, blocks, 0)

    top = jnp.maximum(jnp.maximum(e0[...], e1[...]), e2[...])
    w0 = jnp.exp2(e0[...] - top)
    w1 = jnp.exp2(e1[...] - top)
    w2 = jnp.exp2(e2[...] - top)
    mix = (w0 * o0[...] + w1 * o1[...] + w2 * o2[...]) / (w0 + w1 + w2)
    o_ref[0] = mix.astype(o_ref.dtype)


def _dilated_attention(z):
    b, t, _ = z.shape
    heads = GROUP // HEAD
    q0, k0, v0 = 4 * heads, 5 * heads, 6 * heads
    group = 8
    assert all(w // dl == DIL_BAND and t % w == 0 for w, dl in DILATED_PAIRS)
    assert (t // DIL_BAND) % group == 0

    def col(c0):
        return pl.BlockSpec((1, t, HEAD), lambda bi, h: (bi, 0, c0 + h))

    return pl.pallas_call(
        functools.partial(_dilated_kernel, t=t, group=group),
        grid=(b, heads),
        in_specs=[col(q0), col(k0), col(v0)],
        out_specs=pl.BlockSpec((1, t, HEAD), lambda bi, h: (bi, 0, h)),
        out_shape=jax.ShapeDtypeStruct((b, t, GROUP), BF16),
        scratch_shapes=[pltpu.VMEM((t, HEAD), F32)] * 9,
        compiler_params=_params("arbitrary", "arbitrary"),
        name="dilated_attention",
    )(z, z, z)


def _pack_rows(h, out_ref, row0=0):
    m, half = h.shape[0], h.shape[1] // 2
    bits = pltpu.bitcast(h.astype(BF16).astype(F32), U32)
    word = (bits[:, half:] & jnp.uint32(0xFFFF0000)) | (bits[:, :half] >> 16)
    n_slab = half // LANES
    for s in range(n_slab):
        out_ref[pl.ds(row0 * n_slab + s, m, stride=n_slab), :] = word[:, s * LANES:(s + 1) * LANES]


def _unpack_rows(word):
    return pltpu.bitcast(word << 16, F32), pltpu.bitcast(word & jnp.uint32(0xFFFF0000), F32)


def _outproj_router_kernel(yp_ref, yd_ref, yc_ref, yv_ref, wo_ref, x_ref, g1_ref, n2_ref, sc_ref, sh_ref,
                           rw_ref, rb_ref,
                           x1_ref, h2_ref, idx_ref, rank_ref, wts_ref, cnt_ref, carry_scr, *, tm):
    first = (pl.program_id(0) == 0) & (pl.program_id(1) == 0)

    @pl.when(first)
    def _():
        carry_scr[...] = jnp.zeros_like(carry_scr)

    mix = jnp.dot(yp_ref[0], wo_ref[0 * GROUP:1 * GROUP, :], preferred_element_type=F32)
    mix = mix + jnp.dot(yd_ref[0], wo_ref[1 * GROUP:2 * GROUP, :], preferred_element_type=F32)
    mix = mix + jnp.dot(yc_ref[0], wo_ref[2 * GROUP:3 * GROUP, :], preferred_element_type=F32)
    mix = mix + jnp.dot(yv_ref[0], wo_ref[3 * GROUP:4 * GROUP, :], preferred_element_type=F32)
    x1 = x_ref[0] + g1_ref[0] * mix
    x1_ref[0] = x1

    y = x1 * lax.rsqrt(jnp.mean(x1 * x1, axis=-1, keepdims=True) + RMS_EPS) * n2_ref[0]
    h2 = y * (1.0 + sc_ref[0]) + sh_ref[0]
    _pack_rows(h2, h2_ref)

    logits = lax.dot_general(rw_ref[0], h2, (((1,), (1,)), ((), ())), precision=HIGHEST,
                             preferred_element_type=F32) + rb_ref[0]
    e_idx = lax.broadcasted_iota(I32, (N_EXPERTS, tm), 0)
    work = logits
    vals, sels, hots = [], [], []
    for _ in range(TOP_K):
        mx = jnp.max(work, axis=0, keepdims=True)
        sel = jnp.min(jnp.where(work == mx, e_idx, N_EXPERTS), axis=0, keepdims=True)
        hot = e_idx == sel
        vals.append(mx)
        sels.append(sel)
        hots.append(hot)
        work = jnp.where(hot, -jnp.inf, work)
    exps = [jnp.exp(v - vals[0]) for v in vals]
    denom = exps[0] + exps[1] + exps[2] + exps[3]

    chosen = jnp.zeros((N_EXPERTS, tm), F32)
    for hot in hots:
        chosen = chosen + hot.astype(F32)
    s_idx = lax.broadcasted_iota(I32, (tm, tm), 0)
    t_idx = lax.broadcasted_iota(I32, (tm, tm), 1)
    upper = (s_idx < t_idx).astype(BF16)
    before = jnp.dot(chosen.astype(BF16), upper, preferred_element_type=F32) + carry_scr[:, 0:1]
    for k in range(TOP_K):
        idx_ref[k:k + 1, :] = sels[k]
        rank_ref[k:k + 1, :] = jnp.sum(jnp.where(hots[k], before, 0.0), axis=0, keepdims=True).astype(I32)
        wts_ref[k:k + 1, :] = exps[k] / denom
    carry_scr[...] = carry_scr[...] + jnp.sum(chosen, axis=1, keepdims=True)
    cnt_ref[...] = carry_scr[...]


def _outproj_router(ys, w_out_bf16, x, gate1, norm2_g, scale2, shift2, router_wt, router_b, layer):
    b, t, d = x.shape
    tm = min(t, 512)
    n = b * t
    nt = t // tm
    slab = d // 2 // LANES

    def ytile():
        return pl.BlockSpec((1, tm, GROUP), lambda bi, i: (bi, i, 0))

    def bvec():
        return pl.BlockSpec((1, 1, d), lambda bi, i: (bi, 0, 0))

    tok = pl.BlockSpec((TOP_K, tm), lambda bi, i: (0, bi * nt + i))
    outs = pl.pallas_call(
        functools.partial(_outproj_router_kernel, tm=tm),
        grid=(b, nt),
        in_specs=[
            ytile(), ytile(), ytile(), ytile(),
            pl.BlockSpec((None, 4 * GROUP, d), lambda bi, i: (layer, 0, 0)),
            pl.BlockSpec((1, tm, d), lambda bi, i: (bi, i, 0)),
            bvec(),
            pl.BlockSpec((1, 1, d), lambda bi, i: (layer, 0, 0)),
            bvec(), bvec(),
            pl.BlockSpec((1, N_EXPERTS, d), lambda bi, i: (layer, 0, 0)),
            pl.BlockSpec((1, N_EXPERTS, 1), lambda bi, i: (layer, 0, 0)),
        ],
        out_specs=[
            pl.BlockSpec((1, tm, d), lambda bi, i: (bi, i, 0)),
            pl.BlockSpec((tm * slab, LANES), lambda bi, i: (bi * nt + i, 0)),
            tok, tok, tok,
            pl.BlockSpec((N_EXPERTS, LANES), lambda bi, i: (0, 0)),
        ],
        out_shape=[
            jax.ShapeDtypeStruct((b, t, d), F32),
            jax.ShapeDtypeStruct((n * slab, LANES), U32),
            jax.ShapeDtypeStruct((TOP_K, n), I32),
            jax.ShapeDtypeStruct((TOP_K, n), I32),
            jax.ShapeDtypeStruct((TOP_K, n), F32),
            jax.ShapeDtypeStruct((N_EXPERTS, LANES), F32),
        ],
        scratch_shapes=[pltpu.VMEM((N_EXPERTS, LANES), F32)],
        compiler_params=_params("arbitrary", "arbitrary"),
        name="outproj_router",
    )(*ys, w_out_bf16, x, gate1, norm2_g, scale2, shift2, router_wt, router_b)
    return outs


def _scatter_kernel(zs_ref, pos_ref, h_ref, xs_hbm, zbuf, zsem, sem, *, tm, tm_e):
    zrows = zbuf.shape[0]

    @pl.when(pl.program_id(0) == 0)
    def _():
        zbuf[...] = jnp.zeros_like(zbuf)

        def zero(e, c):
            @pl.when(zs_ref[e] >= 0)
            def _():
                for q in range(tm_e // zrows):
                    pltpu.make_async_copy(zbuf, xs_hbm.at[pl.ds(zs_ref[e] + q * zrows, zrows)], zsem).start()
            return c

        lax.fori_loop(0, N_EXPERTS, zero, 0)

        def zero_done(e, c):
            @pl.when(zs_ref[e] >= 0)
            def _():
                for q in range(tm_e // zrows):
                    pltpu.make_async_copy(zbuf, xs_hbm.at[pl.ds(0, zrows)], zsem).wait()
            return c

        lax.fori_loop(0, N_EXPERTS, zero_done, 0)

    def issue(g, c):
        for u in range(ISSUE_UNROLL):
            tt = g * ISSUE_UNROLL + u
            for k in range(TOP_K):
                pltpu.make_async_copy(h_ref.at[tt], xs_hbm.at[pos_ref[k, tt]], sem).start(priority=k % 2)
        return c

    lax.fori_loop(0, tm // ISSUE_UNROLL, issue, 0)
    for k in range(TOP_K):
        pltpu.make_async_copy(h_ref, xs_hbm.at[pl.ds(0, tm)], sem).wait()


def _scatter_rows(zero_start, pos, h_rows, m_pad, tm_e):
    n, slab, _ = h_rows.shape
    tm = min(n, 256)
    zrows = min(tm_e, 256)
    grid_spec = pltpu.PrefetchScalarGridSpec(
        num_scalar_prefetch=1,
        grid=(n // tm,),
        in_specs=[
            pl.BlockSpec((TOP_K, tm), lambda i, zs: (0, i), memory_space=pltpu.SMEM),
            pl.BlockSpec((tm, slab, LANES), lambda i, zs: (i, 0, 0)),
        ],
        out_specs=pl.BlockSpec(memory_space=pl.ANY),
        scratch_shapes=[pltpu.VMEM((zrows, slab, LANES), U32), pltpu.SemaphoreType.DMA(()),
                        pltpu.SemaphoreType.DMA(())],
    )
    return pl.pallas_call(
        functools.partial(_scatter_kernel, tm=tm, tm_e=tm_e),
        grid_spec=grid_spec,
        out_shape=jax.ShapeDtypeStruct((m_pad, slab, LANES), U32),
        compiler_params=_params("arbitrary"),
        name="scatter_rows",
    )(zero_start, pos, h_rows)


def _deinterleave(hh):
    m, width = hh.shape
    lane = lax.broadcasted_iota(I32, (m, LANES), 1)
    low = lane < LANES // 2
    evens_then_odds = jnp.where(low, 2 * lane, 2 * lane - (LANES - 1))
    parts = [jnp.take_along_axis(hh[:, b * LANES:(b + 1) * LANES], evens_then_odds, axis=1)
             for b in range(width // LANES)]
    gates, lins = [], []
    for b in range(0, len(parts), 2):
        first, second = parts[b], parts[b + 1]
        gates.append(jnp.where(low, first, pltpu.roll(second, LANES // 2, 1)))
        lins.append(jnp.where(low, pltpu.roll(first, LANES // 2, 1), second))
    return jnp.concatenate(gates, axis=1), jnp.concatenate(lins, axis=1)


def _expert_kernel(te_ref, tv_ref, nu_ref, xs_ref, w1_ref, b1_ref, w2_ref, b2_ref, ys_ref,
                   x_scr, acc_scr, *, tm, sub, nc):
    i = pl.program_id(0)
    c = pl.program_id(1)
    d = x_scr.shape[1]
    n_slab = d // 2 // LANES
    live = i < nu_ref[0]

    @pl.when(live)
    def _():
        @pl.when(c == 0)
        def _():
            for s in range(n_slab):
                lo, hi = _unpack_rows(xs_ref[pl.ds(s, tm, stride=n_slab), :])
                x_scr[:, s * LANES:(s + 1) * LANES] = lo.astype(BF16)
                x_scr[:, d // 2 + s * LANES:d // 2 + (s + 1) * LANES] = hi.astype(BF16)
            acc_scr[...] = jnp.zeros_like(acc_scr) + b2_ref[...]

        def ffn_rows(n_rows):
            rows = slice(0, n_rows)
            hh = jnp.dot(x_scr[rows, :], w1_ref[...].astype(BF16), preferred_element_type=F32) + b1_ref[...]
            g, lin = _deinterleave(hh)
            g = jnp.minimum(g, SWIGLU_LIMIT)
            lin = jnp.clip(lin, -SWIGLU_LIMIT, SWIGLU_LIMIT)
            act = g * _sigmoid(SWIGLU_ALPHA * g) * (lin + 1.0)
            acc_scr[rows, :] += jnp.dot(act.astype(BF16), w2_ref[...].astype(BF16), preferred_element_type=F32)

        filled = (tv_ref[i] + sub - 1) // sub
        for nb in range(1, tm // sub + 1):
            @pl.when(filled == nb)
            def _(nb=nb):
                ffn_rows(nb * sub)

        @pl.when(c == nc - 1)
        def _():
            for sb in range(tm // sub):
                _pack_rows(acc_scr[sb * sub:(sb + 1) * sub, :], ys_ref, row0=sb * sub)

    @pl.when(jnp.logical_not(live) & (c == 0))
    def _():
        ys_ref[...] = jnp.zeros_like(ys_ref)


def _experts(tile_expert, tile_valid, n_used, xs2d, w1, b1, w2, b2, layer, tm, sub, m_pad):
    _, n_exp, d, h2 = w1.shape
    hid = h2 // 2
    tc = min(hid, 512)
    nc = hid // tc
    n_tiles = m_pad // tm
    slab = d // 2 // LANES

    def live(i, nu):
        return jnp.minimum(i, nu[0] - 1)

    def chunk(i, c, nu):
        return jnp.where(i < nu[0], c, nc - 1)

    grid_spec = pltpu.PrefetchScalarGridSpec(
        num_scalar_prefetch=3,
        grid=(n_tiles, nc),
        in_specs=[
            pl.BlockSpec((tm * slab, LANES), lambda i, c, te, tv, nu: (live(i, nu), 0)),
            pl.BlockSpec((None, None, d, 2 * tc), lambda i, c, te, tv, nu: (layer, te[i], 0, chunk(i, c, nu))),
            pl.BlockSpec((None, None, 1, 2 * tc), lambda i, c, te, tv, nu: (layer, te[i], 0, chunk(i, c, nu))),
            pl.BlockSpec((None, None, tc, d), lambda i, c, te, tv, nu: (layer, te[i], chunk(i, c, nu), 0)),
            pl.BlockSpec((None, None, 1, d), lambda i, c, te, tv, nu: (layer, te[i], 0, 0)),
        ],
        out_specs=pl.BlockSpec((tm * slab, LANES), lambda i, c, te, tv, nu: (i, 0)),
        scratch_shapes=[pltpu.VMEM((tm, d), BF16), pltpu.VMEM((tm, d), F32)],
    )
    return pl.pallas_call(
        functools.partial(_expert_kernel, tm=tm, sub=sub, nc=nc),
        grid_spec=grid_spec,
        out_shape=jax.ShapeDtypeStruct((m_pad * slab, LANES), U32),
        compiler_params=_params("arbitrary", "arbitrary"),
        name="experts",
    )(tile_expert, tile_valid, n_used, xs2d, w1, b1, w2, b2)


def _combine_kernel(pos_ref, pos_next_ref, wts_ref, x_ref, g2_ref, fg_ref, ys_hbm, ys_flat_hbm, o_ref, buf, sem,
                    *, tm, final, n_steps):
    i = pl.program_id(0)
    n_slab = x_ref.shape[1] // 2 // LANES
    slot_rows = TOP_K * tm * n_slab
    slot = i % 2
    base = pl.multiple_of(slot * slot_rows, slot_rows)

    def issue_block(p_ref, to_slot):
        to_base = to_slot * slot_rows

        def issue(g, c):
            for u in range(ISSUE_UNROLL):
                tt = g * ISSUE_UNROLL + u
                for k in range(TOP_K):
                    row0 = pl.multiple_of(to_base + (k * tm + tt) * n_slab, n_slab)
                    pltpu.make_async_copy(ys_hbm.at[p_ref[k, tt]], buf.at[pl.ds(row0, n_slab)],
                                          sem.at[to_slot]).start(priority=k % 2)
            return c

        lax.fori_loop(0, tm // ISSUE_UNROLL, issue, 0)

    @pl.when(i == 0)
    def _():
        issue_block(pos_ref, 0)

    @pl.when(i + 1 < n_steps)
    def _():
        issue_block(pos_next_ref, 1 - slot)

    pltpu.make_async_copy(ys_flat_hbm.at[pl.ds(0, slot_rows)], buf.at[pl.ds(base, slot_rows)], sem.at[slot]).wait()

    w_sq = jnp.concatenate([wts_ref[...], jnp.zeros((tm - TOP_K, tm), F32)], axis=0)
    w_t = w_sq.T
    w_k = [jnp.broadcast_to(w_t[:, k:k + 1], (tm, LANES)) for k in range(TOP_K)]
    lows, highs = [], []
    for s in range(n_slab):
        acc_lo = jnp.zeros((tm, LANES), F32)
        acc_hi = jnp.zeros((tm, LANES), F32)
        for k in range(TOP_K):
            lo, hi = _unpack_rows(buf[pl.ds(base + k * tm * n_slab + s, tm, stride=n_slab), :])
            acc_lo = acc_lo + w_k[k] * lo
            acc_hi = acc_hi + w_k[k] * hi
        lows.append(acc_lo)
        highs.append(acc_hi)
    moe = jnp.concatenate(lows + highs, axis=1)
    x2 = x_ref[...] + g2_ref[0] * moe
    if final:
        x2 = x2 * lax.rsqrt(jnp.mean(x2 * x2, axis=-1, keepdims=True) + RMS_EPS) * fg_ref[...]
    o_ref[...] = x2


def _combine(pos, wts, x1, gate2, final_g, ys_rows, tokens_per_batch, final):
    n, d = x1.shape
    tm = 256
    steps_per_batch = tokens_per_batch // tm
    n_steps = n // tm
    m_pad, slab, _ = ys_rows.shape
    return pl.pallas_call(
        functools.partial(_combine_kernel, tm=tm, final=final, n_steps=n_steps),
        grid=(n_steps,),
        in_specs=[
            pl.BlockSpec((TOP_K, tm), lambda i: (0, i), memory_space=pltpu.SMEM),
            pl.BlockSpec((TOP_K, tm), lambda i: (0, jnp.minimum(i + 1, n_steps - 1)), memory_space=pltpu.SMEM),
            pl.BlockSpec((TOP_K, tm), lambda i: (0, i)),
            pl.BlockSpec((tm, d), lambda i: (i, 0)),
            pl.BlockSpec((1, 1, d), lambda i: (i // steps_per_batch, 0, 0)),
            pl.BlockSpec((1, d), lambda i: (0, 0)),
            pl.BlockSpec(memory_space=pl.ANY),
            pl.BlockSpec(memory_space=pl.ANY),
        ],
        out_specs=pl.BlockSpec((tm, d), lambda i: (i, 0)),
        out_shape=jax.ShapeDtypeStruct((n, d), F32),
        scratch_shapes=[pltpu.VMEM((2 * TOP_K * tm * slab, LANES), U32), pltpu.SemaphoreType.DMA((2,))],
        compiler_params=_params("arbitrary"),
        name="combine",
    )(pos, pos, wts, x1, gate2, final_g, ys_rows, ys_rows.reshape(m_pad * slab, LANES))


def _routing_plan(idx, rank, counts, tm_e, n_tiles):
    experts = jnp.arange(N_EXPERTS, dtype=I32)
    cnt = counts[:, 0].astype(I32)
    tiles = (cnt + tm_e - 1) // tm_e
    tile_end = jnp.cumsum(tiles)
    tile_start = tile_end - tiles
    offsets = tile_start * tm_e
    pos = rank + jnp.sum(jnp.where(idx[..., None] == experts, offsets, 0), axis=-1)
    n_used = tile_end[-1]
    tile_ids = jnp.arange(n_tiles, dtype=I32)
    te = jnp.sum((tile_ids[:, None] >= tile_end[None, :]).astype(I32), axis=1)
    te_last = jnp.sum((n_used - 1 >= tile_end).astype(I32))
    te = jnp.where(tile_ids < n_used, te, te_last).astype(I32)
    mine = te[:, None] == experts[None, :]
    rows_left = jnp.sum(jnp.where(mine, cnt[None, :] - (tile_ids[:, None] - tile_start[None, :]) * tm_e, 0), axis=1)
    tile_valid = jnp.where(tile_ids < n_used, jnp.clip(rows_left, 0, tm_e), 0).astype(I32)
    zero_start = jnp.where(tiles > 0, (tile_end - 1) * tm_e, -1).astype(I32)
    return pos.astype(I32), te, tile_valid, n_used.reshape(1).astype(I32), zero_start


def kernel(x, c, positions, mod_w, mod_b, norm1_g, norm2_g, w_in, pool_w, pool_scale, diff_lq1, diff_lk1,
           diff_lq2, diff_lk2, diff_subln_g, conv_dw_w, conv_dw_b, conv_ln_g, conv_ln_b, conv_pw_w, conv_pw_b,
           w_out, router_w, router_b, exp_w1, exp_b1, exp_w2, exp_b2, final_g):
    b, t, d = x.shape
    depth = mod_w.shape[0]
    n = b * t
    tm_e = 1024
    sub_e = 256
    m_pad = n * TOP_K + N_EXPERTS * tm_e
    n_tiles = m_pad // tm_e
    slab = d // 2 // LANES

    def row3(a):
        return a.reshape(a.shape[0], 1, a.shape[1])

    mod = _modulation(c, mod_w, mod_b)
    tables = _rope_tables(positions)
    w_in_b = w_in.astype(BF16)
    w_out_b = w_out.astype(BF16)
    pool_w_b = pool_w.astype(BF16)
    pw_w_b = conv_pw_w.astype(BF16)
    router_wt = jnp.swapaxes(router_w, 1, 2)
    router_b3 = router_b.reshape(depth, N_EXPERTS, 1)
    b1r = exp_b1.reshape(depth, N_EXPERTS, 1, exp_b1.shape[-1])
    b2r = exp_b2.reshape(depth, N_EXPERTS, 1, d)
    final_g2 = final_g.reshape(1, d)

    for l in range(depth):
        lambda_init = 0.8 - 0.6 * math.exp(-0.3 * l)
        sh1, sc1, g1, sh2, sc2, g2 = [m.reshape(b, 1, d) for m in jnp.split(mod[l], 6, axis=-1)]
        z = _inproj(x, row3(norm1_g), sc1, sh1, w_in_b, tables, l)
        y_pool, y_conv = _poolconv(z, pool_w_b, row3(pool_scale), conv_dw_w, row3(conv_dw_b), row3(conv_ln_g),
                                   row3(conv_ln_b), pw_w_b, row3(conv_pw_b), l)
        y_diff = _diff_attention(z, row3(diff_lq1), row3(diff_lk1), row3(diff_lq2), row3(diff_lk2),
                                 row3(diff_subln_g), l, lambda_init)
        y_dil = _dilated_attention(z)
        x1, h2, idx, rank, wts, counts = _outproj_router(
            (y_pool, y_diff, y_dil, y_conv), w_out_b, x, g1, row3(norm2_g), sc2, sh2, router_wt, router_b3, l)
        pos, tile_expert, tile_valid, n_used, zero_start = _routing_plan(idx, rank, counts, tm_e, n_tiles)
        xs = _scatter_rows(zero_start, pos, h2.reshape(n, slab, LANES), m_pad, tm_e)
        ys = _experts(tile_expert, tile_valid, n_used, xs.reshape(m_pad * slab, LANES), exp_w1, b1r, exp_w2, b2r,
                      l, tm_e, sub_e, m_pad)
        x = _combine(pos, wts, x1.reshape(n, d), g2, final_g2, ys.reshape(m_pad, slab, LANES), t,
                     final=(l == depth - 1)).reshape(b, t, d)
    return x
```

```python
import functools
import math

import numpy as np
import jax
import jax.numpy as jnp
from jax import lax
from jax.experimental import pallas as pl
from jax.experimental.pallas import tpu as pltpu

F32 = jnp.float32
BF16 = jnp.bfloat16
I32 = jnp.int32
U32 = jnp.uint32
HIGHEST = lax.Precision.HIGHEST

LANES = 128
SUBLANES = 8
VMEM_LIMIT_BYTES = 60 * 1024 * 1024

POOL_WINDOWS = (2, 4, 8, 16)
DILATED_PAIRS = ((128, 1), (512, 4), (2048, 16))
DIL_BAND = 128
CONV_WIDTH = 31
HALO = 32
ISSUE_UNROLL = 8
N_EXPERTS = 32
TOP_K = 4
SWIGLU_ALPHA = 1.702
SWIGLU_LIMIT = 7.0
ROPE_THETA = 500000.0
RMS_EPS = 1e-6
LN_EPS = 1e-5
NEG_INF = -1e30
GROUP = 512
HEAD = 128
DIFF_QK = 64
LOG2E = math.log2(math.e)


def _params(*semantics):
    return pltpu.CompilerParams(dimension_semantics=semantics, vmem_limit_bytes=VMEM_LIMIT_BYTES)


def _sigmoid(x):
    return 1.0 / (1.0 + jnp.exp(-x))


def _mod_kernel(c_ref, w_ref, b_ref, o_ref):
    c = c_ref[...]
    ca = c * _sigmoid(c)
    o_ref[0] = jnp.dot(ca, w_ref[0], precision=HIGHEST, preferred_element_type=F32) + b_ref[0]


def _modulation(c, mod_w, mod_b):
    depth, d, n = mod_w.shape
    b = c.shape[0]
    rows = 8
    c_pad = jnp.zeros((rows, d), F32).at[:b].set(c)
    tn = 512
    out = pl.pallas_call(
        _mod_kernel,
        grid=(depth, n // tn),
        in_specs=[
            pl.BlockSpec((rows, d), lambda l, j: (0, 0)),
            pl.BlockSpec((1, d, tn), lambda l, j: (l, 0, j)),
            pl.BlockSpec((1, 1, tn), lambda l, j: (l, 0, j)),
        ],
        out_specs=pl.BlockSpec((1, rows, tn), lambda l, j: (l, 0, j)),
        out_shape=jax.ShapeDtypeStruct((depth, rows, n), F32),
        compiler_params=_params("arbitrary", "arbitrary"),
        name="modulation",
    )(c_pad, mod_w, mod_b.reshape(depth, 1, n))
    return out[:, :b]


def _rope_table_kernel(pos_ref, invd_ref, invc_ref, cd_ref, sd_ref, cc_ref, sc_ref):
    p = pos_ref[0]
    lane = lax.broadcasted_iota(I32, (1, LANES), 1)
    for inv_ref, c_ref, s_ref, hd in ((invd_ref, cd_ref, sd_ref, DIFF_QK), (invc_ref, cc_ref, sc_ref, HEAD)):
        half = hd // 8
        lm = lane % hd
        ang = p * inv_ref[...]
        c_ref[0] = jnp.cos(ang)
        s_ref[0] = jnp.where(lm < half, -jnp.sin(ang), jnp.sin(ang))


def _lane_inv_freq(hd):
    half = hd // 8
    inv = ROPE_THETA ** (-jnp.arange(half, dtype=F32) / half)
    lm = np.arange(LANES) % hd
    rotated = jnp.asarray(lm < 2 * half)
    return jnp.where(rotated, inv[lm % half], 0.0).reshape(1, LANES).astype(F32)


def _rope_tables(positions):
    b, t = positions.shape
    tm = min(t, 1024)
    pos = positions.astype(F32).reshape(b, t, 1)
    spec_t = pl.BlockSpec((1, tm, LANES), lambda bi, i: (bi, i, 0))
    spec_inv = pl.BlockSpec((1, LANES), lambda bi, i: (0, 0))
    shp = jax.ShapeDtypeStruct((b, t, LANES), F32)
    return pl.pallas_call(
        _rope_table_kernel,
        grid=(b, t // tm),
        in_specs=[pl.BlockSpec((1, tm, 1), lambda bi, i: (bi, i, 0)), spec_inv, spec_inv],
        out_specs=[spec_t] * 4,
        out_shape=[shp] * 4,
        compiler_params=_params("arbitrary", "arbitrary"),
        name="rope_tables",
    )(pos, _lane_inv_freq(DIFF_QK), _lane_inv_freq(HEAD))


def _rope_apply(z, cos, sin, hd):
    half = hd // 8
    lane = lax.broadcasted_iota(I32, (1, LANES), 1)
    first = (lane % hd) < half
    outs = []
    for cb in range(z.shape[1] // LANES):
        zc = z[:, cb * LANES:(cb + 1) * LANES]
        partner = jnp.where(first, pltpu.roll(zc, LANES - half, 1), pltpu.roll(zc, half, 1))
        outs.append(zc * cos + partner * sin)
    return jnp.concatenate(outs, axis=1)


def _inproj_kernel(x_ref, g_ref, sc_ref, sh_ref, w_ref, cd_ref, sd_ref, cc_ref, sc2_ref, z_ref, h_scr):
    j = pl.program_id(2)

    def project():
        return jnp.dot(h_scr[...], w_ref[...], preferred_element_type=F32)

    @pl.when(j == 0)
    def _():
        x = x_ref[0]
        y = x * lax.rsqrt(jnp.mean(x * x, axis=-1, keepdims=True) + RMS_EPS) * g_ref[0]
        h = (y * (1.0 + sc_ref[0]) + sh_ref[0]).astype(BF16)
        h_scr[...] = h
        z_ref[0] = jnp.dot(h, w_ref[...], preferred_element_type=F32).astype(z_ref.dtype)

    @pl.when((j == 1) | (j == 2))
    def _():
        r = _rope_apply(project(), cd_ref[0], sd_ref[0], DIFF_QK)
        r = r * jnp.where(j == 1, DIFF_QK ** -0.5 * LOG2E, 1.0)
        z_ref[0] = r.astype(z_ref.dtype)

    @pl.when((j == 4) | (j == 5))
    def _():
        r = _rope_apply(project(), cc_ref[0], sc2_ref[0], HEAD)
        r = r * jnp.where(j == 4, HEAD ** -0.5 * LOG2E, 1.0)
        z_ref[0] = r.astype(z_ref.dtype)

    @pl.when((j == 3) | (j >= 6))
    def _():
        z_ref[0] = project().astype(z_ref.dtype)


def _inproj(x, norm_g, scale, shift, w_in_bf16, tables, layer):
    b, t, d = x.shape
    n = w_in_bf16.shape[-1]
    tm = min(t, 1024)
    tn = GROUP
    cd, sd, cc, sc = tables
    spec_tab = pl.BlockSpec((1, tm, LANES), lambda bi, i, j: (bi, i, 0))
    spec_vec = pl.BlockSpec((1, 1, d), lambda bi, i, j: (bi, 0, 0))
    return pl.pallas_call(
        _inproj_kernel,
        grid=(b, t // tm, n // tn),
        in_specs=[
            pl.BlockSpec((1, tm, d), lambda bi, i, j: (bi, i, 0)),
            pl.BlockSpec((1, 1, d), lambda bi, i, j: (layer, 0, 0)),
            spec_vec, spec_vec,
            pl.BlockSpec((None, d, tn), lambda bi, i, j: (layer, 0, j)),
            spec_tab, spec_tab, spec_tab, spec_tab,
        ],
        out_specs=pl.BlockSpec((1, tm, tn), lambda bi, i, j: (bi, i, j)),
        out_shape=jax.ShapeDtypeStruct((b, t, n), BF16),
        scratch_shapes=[pltpu.VMEM((tm, d), BF16)],
        compiler_params=_params("arbitrary", "arbitrary", "arbitrary"),
        name="inproj",
    )(x, norm_g, scale, shift, w_in_bf16, cd, sd, cc, sc)


def _poolconv_kernel(zp_ref, zph_ref, za_ref, zah_ref, zg_ref, zgh_ref,
                     pw_ref, ps_ref, dww_ref, dwb_ref, lng_ref, lnb_ref, pww_ref, pwb_ref,
                     yp_ref, yc_ref, xp_scr, u_scr, ush_scr, *, tm):
    i = pl.program_id(1)
    keep = jnp.where(i == 0, 0.0, 1.0)

    xp_scr[0:HALO, :] = zph_ref[0].astype(F32) * keep
    xp_scr[HALO:, :] = zp_ref[0].astype(F32)
    t_glob = i * tm + lax.broadcasted_iota(I32, (tm, 1), 0)
    for g, w in enumerate(POOL_WINDOWS):
        cols = slice(g * LANES, (g + 1) * LANES)
        xg = xp_scr[HALO:HALO + tm, cols]
        acc = xg
        for k in range(1, w):
            acc = acc + xp_scr[HALO - k:HALO - k + tm, cols]
        cnt = jnp.minimum(t_glob + 1, w).astype(F32)
        pooled = acc / cnt - xg
        yg = jnp.dot(pooled.astype(BF16), pw_ref[0, g], preferred_element_type=F32)
        yp_ref[0, :, cols] = (yg * ps_ref[0, :, cols]).astype(yp_ref.dtype)

    ah = zah_ref[0].astype(F32)
    gh = zgh_ref[0].astype(F32)
    u_scr[0:HALO, :] = ah * _sigmoid(gh) * keep
    a = za_ref[0].astype(F32)
    gg = zg_ref[0].astype(F32)
    u_scr[HALO:, :] = a * _sigmoid(gg)
    span = tm + HALO - SUBLANES
    for sh in range(1, SUBLANES):
        ush_scr[sh - 1] = u_scr[sh:sh + span, :]
    acc = jnp.zeros((tm, GROUP), F32) + dwb_ref[0]
    base = HALO - (CONV_WIDTH - 1)
    for k in range(CONV_WIDTH):
        off = base + k
        sh, start = off % SUBLANES, off - off % SUBLANES
        window = u_scr[start:start + tm, :] if sh == 0 else ush_scr[sh - 1, start:start + tm, :]
        acc = acc + window * dww_ref[0, k:k + 1, :]
    mu = jnp.mean(acc, axis=-1, keepdims=True)
    cen = acc - mu
    var = jnp.mean(cen * cen, axis=-1, keepdims=True)
    v = cen * lax.rsqrt(var + LN_EPS) * lng_ref[0] + lnb_ref[0]
    v = v * _sigmoid(v)
    y = jnp.dot(v.astype(BF16), pww_ref[0], preferred_element_type=F32) + pwb_ref[0]
    yc_ref[0] = y.astype(yc_ref.dtype)


def _poolconv(z, pool_w_bf16, pool_scale, dw_w, dw_b, ln_g, ln_b, pw_w_bf16, pw_b, layer):
    b, t, _ = z.shape
    tm = min(t, 512)
    r = tm // HALO
    a_blk = 7
    g_blk = 8

    def cur(col):
        return pl.BlockSpec((1, tm, GROUP), lambda bi, i: (bi, i, col))

    def halo(col):
        return pl.BlockSpec((1, HALO, GROUP), lambda bi, i: (bi, jnp.maximum(i * r - 1, 0), col))

    def vec(n):
        return pl.BlockSpec((1, 1, n), lambda bi, i: (layer, 0, 0))

    out_spec = pl.BlockSpec((1, tm, GROUP), lambda bi, i: (bi, i, 0))
    shp = jax.ShapeDtypeStruct((b, t, GROUP), BF16)
    return pl.pallas_call(
        functools.partial(_poolconv_kernel, tm=tm),
        grid=(b, t // tm),
        in_specs=[
            cur(0), halo(0), cur(a_blk), halo(a_blk), cur(g_blk), halo(g_blk),
            pl.BlockSpec((1, len(POOL_WINDOWS), LANES, LANES), lambda bi, i: (layer, 0, 0, 0)),
            vec(GROUP),
            pl.BlockSpec((1, CONV_WIDTH, GROUP), lambda bi, i: (layer, 0, 0)),
            vec(GROUP), vec(GROUP), vec(GROUP),
            pl.BlockSpec((1, GROUP, GROUP), lambda bi, i: (layer, 0, 0)),
            vec(GROUP),
        ],
        out_specs=[out_spec, out_spec],
        out_shape=[shp, shp],
        scratch_shapes=[pltpu.VMEM((tm + HALO, GROUP), F32), pltpu.VMEM((tm + HALO, GROUP), F32),
                        pltpu.VMEM((SUBLANES - 1, tm + HALO - SUBLANES, GROUP), F32)],
        compiler_params=_params("arbitrary", "arbitrary"),
        name="pool_conv",
    )(z, z, z, z, z, z, pool_w_bf16, pool_scale, dw_w, dw_b, ln_g, ln_b, pw_w_bf16, pw_b)


def _diff_kernel(q_ref, k_ref, v_ref, lq1_ref, lk1_ref, lq2_ref, lk2_ref, g_ref, o_ref, *, tq, lambda_init):
    qi = pl.program_id(2)
    q = q_ref[0]
    lane = lax.broadcasted_iota(I32, (tq, HEAD), 1)
    zero = jnp.zeros_like(q)
    q2 = jnp.concatenate([jnp.where(lane < DIFF_QK, q, zero), jnp.where(lane >= DIFF_QK, q, zero)], axis=0)
    rows = 2 * tq

    def lane_tiles(s):
        return [s[:, c * LANES:(c + 1) * LANES] for c in range(tq // LANES)]

    def fold(j, carry, masked):
        m, l_run, acc = carry
        kb = k_ref[0, pl.ds(pl.multiple_of(j * tq, tq), tq), :]
        vb = v_ref[0, pl.ds(pl.multiple_of(j * tq, tq), tq), :]
        s = lax.dot_general(q2, kb, (((1,), (1,)), ((), ())), preferred_element_type=F32)
        if masked:
            row = lax.broadcasted_iota(I32, (rows, tq), 0) % tq
            col = lax.broadcasted_iota(I32, (rows, tq), 1)
            s = jnp.where(col <= row, s, NEG_INF)
        tiles = lane_tiles(s)
        m_blk = tiles[0]
        for part in tiles[1:]:
            m_blk = jnp.maximum(m_blk, part)
        m_new = jnp.maximum(m, jnp.max(m_blk, axis=-1, keepdims=True))
        alpha = jnp.exp2(m - m_new)
        p = jnp.exp2(s - m_new)
        l_run = alpha * l_run
        for part in lane_tiles(p):
            l_run = l_run + part
        acc = alpha * acc + jnp.dot(p.astype(BF16), vb, preferred_element_type=F32)
        return m_new, l_run, acc

    init = (jnp.full((rows, 1), NEG_INF, F32), jnp.zeros((rows, LANES), F32), jnp.zeros((rows, HEAD), F32))
    carry = lax.fori_loop(0, qi, lambda j, c: fold(j, c, False), init)
    _, l_run, acc = fold(qi, carry, True)
    l = jnp.sum(l_run, axis=-1, keepdims=True)

    lam =(jnp.exp(jnp.sum(lq1_ref[0] * lk1_ref[0], axis=-1, keepdims=True))
           - jnp.exp(jnp.sum(lq2_ref[0] * lk2_ref[0], axis=-1, keepdims=True)) + lambda_init)
    o = acc / l
    od = o[:tq] - lam * o[tq:]
    y = od * lax.rsqrt(jnp.mean(od * od, axis=-1, keepdims=True) + RMS_EPS) * g_ref[0]
    o_ref[0] = (y * (1.0 - lambda_init)).astype(o_ref.dtype)


def _diff_attention(z, lq1, lk1, lq2, lk2, subln_g, layer, lambda_init):
    b, t, _ = z.shape
    heads = GROUP // HEAD
    tq = min(t, 512)
    q0, k0, v0 = 1 * heads, 2 * heads, 3 * heads

    def vec(n):
        return pl.BlockSpec((1, 1, n), lambda bi, h, i: (layer, 0, 0))

    return pl.pallas_call(
        functools.partial(_diff_kernel, tq=tq, lambda_init=lambda_init),
        grid=(b, heads, t // tq),
        in_specs=[
            pl.BlockSpec((1, tq, HEAD), lambda bi, h, i: (bi, i, q0 + h)),
            pl.BlockSpec((1, t, HEAD), lambda bi, h, i: (bi, 0, k0 + h)),
            pl.BlockSpec((1, t, HEAD), lambda bi, h, i: (bi, 0, v0 + h)),
            vec(DIFF_QK), vec(DIFF_QK), vec(DIFF_QK), vec(DIFF_QK), vec(HEAD),
        ],
        out_specs=pl.BlockSpec((1, tq, HEAD), lambda bi, h, i: (bi, i, h)),
        out_shape=jax.ShapeDtypeStruct((b, t, GROUP), BF16),
        compiler_params=_params("arbitrary", "arbitrary", "arbitrary"),
        name="diff_attention",
    )(z, z, z, lq1, lk1, lq2, lk2, subln_g)


def _dilated_kernel(q_ref, k_ref, v_ref, o_ref, qf, kf, vf, o0, o1, o2, e0, e1, e2, *, t, group):
    qf[...] = q_ref[0].astype(F32)
    kf[...] = k_ref[0].astype(F32)
    vf[...] = v_ref[0].astype(F32)
    band = DIL_BAND
    a_idx = lax.broadcasted_iota(I32, (band, 2 * band), 0)
    b_idx = lax.broadcasted_iota(I32, (band, 2 * band), 1)
    in_band = (b_idx >= a_idx) & (b_idx <= a_idx + band)
    outs = ((o0, e0), (o1, e1), (o2, e2))

    for (window, dil), (o_scr, e_scr) in zip(DILATED_PAIRS, outs):
        nb = t // (dil * band)

        def one_block(it, dil=dil, nb=nb, o_scr=o_scr, e_scr=e_scr):
            r = it // nb
            n = it % nb
            start = r + n * (band * dil)
            prev = r + jnp.maximum(n - 1, 0) * (band * dil)

            def rows(ref, s0):
                return ref[pl.ds(s0, band, stride=dil), :] if dil > 1 else ref[pl.ds(s0, band), :]

            qb = rows(qf, start).astype(BF16)
            kk = jnp.concatenate([rows(kf, prev), rows(kf, start)], axis=0).astype(BF16)
            vv = jnp.concatenate([rows(vf, prev), rows(vf, start)], axis=0).astype(BF16)
            s = lax.dot_general(qb, kk, (((1,), (1,)), ((), ())), preferred_element_type=F32)
            valid = in_band & (b_idx >= jnp.where(n > 0, 0, band))
            s = jnp.where(valid, s, NEG_INF)
            m = jnp.max(s, axis=-1, keepdims=True)
            p = jnp.exp2(s - m)
            l = jnp.sum(p, axis=-1, keepdims=True)
            o = jnp.dot(p.astype(BF16), vv, preferred_element_type=F32) / l
            idx = pl.ds(start, band, stride=dil) if dil > 1 else pl.ds(start, band)
            o_scr[idx, :] = o
            e_scr[idx, :] = jnp.broadcast_to(m + jnp.log2(l), (band, HEAD))

        def blocks(g, carry, one_block=one_block):
            for u in range(group):
                one_block(g * group + u)
            return carry

        lax.fori_loop(0, dil * nb // group, blocks, 0)

    top = jnp.maximum(jnp.maximum(e0[...], e1[...]), e2[...])
    w0 = jnp.exp2(e0[...] - top)
    w1 = jnp.exp2(e1[...] - top)
    w2 = jnp.exp2(e2[...] - top)
    mix = (w0 * o0[...] + w1 * o1[...] + w2 * o2[...]) / (w0 + w1 + w2)
    o_ref[0] = mix.astype(o_ref.dtype)


def _dilated_attention(z):
    b, t, _ = z.shape
    heads = GROUP // HEAD
    q0, k0, v0 = 4 * heads, 5 * heads, 6 * heads
    group = 8
    assert all(w // dl == DIL_BAND and t % w == 0 for w, dl in DILATED_PAIRS)
    assert (t // DIL_BAND) % group == 0

    def col(c0):
        return pl.BlockSpec((1, t, HEAD), lambda bi, h: (bi, 0, c0 + h))

    return pl.pallas_call(
        functools.partial(_dilated_kernel, t=t, group=group),
        grid=(b, heads),
        in_specs=[col(q0), col(k0), col(v0)],
        out_specs=pl.BlockSpec((1, t, HEAD), lambda bi, h: (bi, 0, h)),
        out_shape=jax.ShapeDtypeStruct((b, t, GROUP), BF16),
        scratch_shapes=[pltpu.VMEM((t, HEAD), F32)] * 9,
        compiler_params=_params("arbitrary", "arbitrary"),
        name="dilated_attention",
    )(z, z, z)


def _pack_rows(h, out_ref, row0=0):
    m, half = h.shape[0], h.shape[1] // 2
    bits = pltpu.bitcast(h.astype(BF16).astype(F32), U32)
    word = (bits[:, half:] & jnp.uint32(0xFFFF0000)) | (bits[:, :half] >> 16)
    n_slab = half // LANES
    for s in range(n_slab):
        out_ref[pl.ds(row0 * n_slab + s, m, stride=n_slab), :] = word[:, s * LANES:(s + 1) * LANES]


def _unpack_rows(word):
    return pltpu.bitcast(word << 16, F32), pltpu.bitcast(word & jnp.uint32(0xFFFF0000), F32)


def _outproj_router_kernel(yp_ref, yd_ref, yc_ref, yv_ref, wo_ref, x_ref, g1_ref, n2_ref, sc_ref, sh_ref,
                           rw_ref, rb_ref,
                           x1_ref, h2_ref, idx_ref, rank_ref, wts_ref, cnt_ref, carry_scr, *, tm):
    first = (pl.program_id(0) == 0) & (pl.program_id(1) == 0)

    @pl.when(first)
    def _():
        carry_scr[...] = jnp.zeros_like(carry_scr)

    mix = jnp.dot(yp_ref[0], wo_ref[0 * GROUP:1 * GROUP, :], preferred_element_type=F32)
    mix = mix + jnp.dot(yd_ref[0], wo_ref[1 * GROUP:2 * GROUP, :], preferred_element_type=F32)
    mix = mix + jnp.dot(yc_ref[0], wo_ref[2 * GROUP:3 * GROUP, :], preferred_element_type=F32)
    mix = mix + jnp.dot(yv_ref[0], wo_ref[3 * GROUP:4 * GROUP, :], preferred_element_type=F32)
    x1 = x_ref[0] + g1_ref[0] * mix
    x1_ref[0] = x1

    y = x1 * lax.rsqrt(jnp.mean(x1 * x1, axis=-1, keepdims=True) + RMS_EPS) * n2_ref[0]
    h2 = y * (1.0 + sc_ref[0]) + sh_ref[0]
    _pack_rows(h2, h2_ref)

    logits = lax.dot_general(rw_ref[0], h2, (((1,), (1,)), ((), ())), precision=HIGHEST,
                             preferred_element_type=F32) + rb_ref[0]
    e_idx = lax.broadcasted_iota(I32, (N_EXPERTS, tm), 0)
    work = logits
    vals, sels, hots = [], [], []
    for _ in range(TOP_K):
        mx = jnp.max(work, axis=0, keepdims=True)
        sel = jnp.min(jnp.where(work == mx, e_idx, N_EXPERTS), axis=0, keepdims=True)
        hot = e_idx == sel
        vals.append(mx)
        sels.append(sel)
        hots.append(hot)
        work = jnp.where(hot, -jnp.inf, work)
    exps = [jnp.exp(v - vals[0]) for v in vals]
    denom = exps[0] + exps[1] + exps[2] + exps[3]

    chosen = jnp.zeros((N_EXPERTS, tm), F32)
    for hot in hots:
        chosen = chosen + hot.astype(F32)
    s_idx = lax.broadcasted_iota(I32, (tm, tm), 0)
    t_idx = lax.broadcasted_iota(I32, (tm, tm), 1)
    upper = (s_idx < t_idx).astype(BF16)
    before = jnp.dot(chosen.astype(BF16), upper, preferred_element_type=F32) + carry_scr[:, 0:1]
    for k in range(TOP_K):
        idx_ref[k:k + 1, :] = sels[k]
        rank_ref[k:k + 1, :] = jnp.sum(jnp.where(hots[k], before, 0.0), axis=0, keepdims=True).astype(I32)
        wts_ref[k:k + 1, :] = exps[k] / denom
    carry_scr[...] = carry_scr[...] + jnp.sum(chosen, axis=1, keepdims=True)
    cnt_ref[...] = carry_scr[...]


def _outproj_router(ys, w_out_bf16, x, gate1, norm2_g, scale2, shift2, router_wt, router_b, layer):
    b, t, d = x.shape
    tm = min(t, 512)
    n = b * t
    nt = t // tm
    slab = d // 2 // LANES

    def ytile():
        return pl.BlockSpec((1, tm, GROUP), lambda bi, i: (bi, i, 0))

    def bvec():
        return pl.BlockSpec((1, 1, d), lambda bi, i: (bi, 0, 0))

    tok = pl.BlockSpec((TOP_K, tm), lambda bi, i: (0, bi * nt + i))
    outs = pl.pallas_call(
        functools.partial(_outproj_router_kernel, tm=tm),
        grid=(b, nt),
        in_specs=[
            ytile(), ytile(), ytile(), ytile(),
            pl.BlockSpec((None, 4 * GROUP, d), lambda bi, i: (layer, 0, 0)),
            pl.BlockSpec((1, tm, d), lambda bi, i: (bi, i, 0)),
            bvec(),
            pl.BlockSpec((1, 1, d), lambda bi, i: (layer, 0, 0)),
            bvec(), bvec(),
            pl.BlockSpec((1, N_EXPERTS, d), lambda bi, i: (layer, 0, 0)),
            pl.BlockSpec((1, N_EXPERTS, 1), lambda bi, i: (layer, 0, 0)),
        ],
        out_specs=[
            pl.BlockSpec((1, tm, d), lambda bi, i: (bi, i, 0)),
            pl.BlockSpec((tm * slab, LANES), lambda bi, i: (bi * nt + i, 0)),
            tok, tok, tok,
            pl.BlockSpec((N_EXPERTS, LANES), lambda bi, i: (0, 0)),
        ],
        out_shape=[
            jax.ShapeDtypeStruct((b, t, d), F32),
            jax.ShapeDtypeStruct((n * slab, LANES), U32),
            jax.ShapeDtypeStruct((TOP_K, n), I32),
            jax.ShapeDtypeStruct((TOP_K, n), I32),
            jax.ShapeDtypeStruct((TOP_K, n), F32),
            jax.ShapeDtypeStruct((N_EXPERTS, LANES), F32),
        ],
        scratch_shapes=[pltpu.VMEM((N_EXPERTS, LANES), F32)],
        compiler_params=_params("arbitrary", "arbitrary"),
        name="outproj_router",
    )(*ys, w_out_bf16, x, gate1, norm2_g, scale2, shift2, router_wt, router_b)
    return outs


def _scatter_kernel(zs_ref, pos_ref, h_ref, xs_hbm, zbuf, zsem, sem, *, tm, tm_e):
    zrows = zbuf.shape[0]

    @pl.when(pl.program_id(0) == 0)
    def _():
        zbuf[...] = jnp.zeros_like(zbuf)

        def zero(e, c):
            @pl.when(zs_ref[e] >= 0)
            def _():
                for q in range(tm_e // zrows):
                    pltpu.make_async_copy(zbuf, xs_hbm.at[pl.ds(zs_ref[e] + q * zrows, zrows)], zsem).start()
            return c

        lax.fori_loop(0, N_EXPERTS, zero, 0)

        def zero_done(e, c):
            @pl.when(zs_ref[e] >= 0)
            def _():
                for q in range(tm_e // zrows):
                    pltpu.make_async_copy(zbuf, xs_hbm.at[pl.ds(0, zrows)], zsem).wait()
            return c

        lax.fori_loop(0, N_EXPERTS, zero_done, 0)

    def issue(g, c):
        for u in range(ISSUE_UNROLL):
            tt = g * ISSUE_UNROLL + u
            for k in range(TOP_K):
                pltpu.make_async_copy(h_ref.at[tt], xs_hbm.at[pos_ref[k, tt]], sem).start(priority=k % 2)
        return c

    lax.fori_loop(0, tm // ISSUE_UNROLL, issue, 0)
    for k in range(TOP_K):
        pltpu.make_async_copy(h_ref, xs_hbm.at[pl.ds(0, tm)], sem).wait()


def _scatter_rows(zero_start, pos, h_rows, m_pad, tm_e):
    n, slab, _ = h_rows.shape
    tm = min(n, 256)
    zrows = min(tm_e, 256)
    grid_spec = pltpu.PrefetchScalarGridSpec(
        num_scalar_prefetch=1,
        grid=(n // tm,),
        in_specs=[
            pl.BlockSpec((TOP_K, tm), lambda i, zs: (0, i), memory_space=pltpu.SMEM),
            pl.BlockSpec((tm, slab, LANES), lambda i, zs: (i, 0, 0)),
        ],
        out_specs=pl.BlockSpec(memory_space=pl.ANY),
        scratch_shapes=[pltpu.VMEM((zrows, slab, LANES), U32), pltpu.SemaphoreType.DMA(()),
                        pltpu.SemaphoreType.DMA(())],
    )
    return pl.pallas_call(
        functools.partial(_scatter_kernel, tm=tm, tm_e=tm_e),
        grid_spec=grid_spec,
        out_shape=jax.ShapeDtypeStruct((m_pad, slab, LANES), U32),
        compiler_params=_params("arbitrary"),
        name="scatter_rows",
    )(zero_start, pos, h_rows)


def _deinterleave(hh):
    m, width = hh.shape
    lane = lax.broadcasted_iota(I32, (m, LANES), 1)
    low = lane < LANES // 2
    evens_then_odds = jnp.where(low, 2 * lane, 2 * lane - (LANES - 1))
    parts = [jnp.take_along_axis(hh[:, b * LANES:(b + 1) * LANES], evens_then_odds, axis=1)
             for b in range(width // LANES)]
    gates, lins = [], []
    for b in range(0, len(parts), 2):
        first, second = parts[b], parts[b + 1]
        gates.append(jnp.where(low, first, pltpu.roll(second, LANES // 2, 1)))
        lins.append(jnp.where(low, pltpu.roll(first, LANES // 2, 1), second))
    return jnp.concatenate(gates, axis=1), jnp.concatenate(lins, axis=1)


def _expert_kernel(te_ref, tv_ref, nu_ref, xs_ref, w1_ref, b1_ref, w2_ref, b2_ref, ys_ref,
                   x_scr, acc_scr, *, tm, sub, nc):
    i = pl.program_id(0)
    c = pl.program_id(1)
    d = x_scr.shape[1]
    n_slab = d // 2 // LANES
    live = i < nu_ref[0]

    def unpack_x():
        for s in range(n_slab):
            lo, hi = _unpack_rows(xs_ref[pl.ds(s, tm, stride=n_slab), :])
            x_scr[:, s * LANES:(s + 1) * LANES] = lo.astype(BF16)
            x_scr[:, d // 2 + s * LANES:d // 2 + (s + 1) * LANES] = hi.astype(BF16)

    def ffn_rows(n_rows):
        hh = jnp.dot(x_scr[0:n_rows, :], w1_ref[...].astype(BF16), preferred_element_type=F32) + b1_ref[...]
        g, lin = _deinterleave(hh)
        g = jnp.minimum(g, SWIGLU_LIMIT)
        lin = jnp.clip(lin, -SWIGLU_LIMIT, SWIGLU_LIMIT)
        act = g * _sigmoid(SWIGLU_ALPHA * g) * (lin + 1.0)
        return jnp.dot(act.astype(BF16), w2_ref[...].astype(BF16), preferred_element_type=F32)

    def pack_y(read_rows):
        for sb in range(tm // sub):
            _pack_rows(read_rows(sb * sub, (sb + 1) * sub), ys_ref, row0=sb * sub)

    @pl.when(live)
    def _():
        n_sub = tm // sub
        filled = (tv_ref[i] + sub - 1) // sub
        full = filled == n_sub
        first = c == 0
        last = c == nc - 1

        @pl.when(full & first)
        def _():
            unpack_x()
            acc_scr[...] = b2_ref[...] + ffn_rows(tm)

        @pl.when(full & last)
        def _():
            y = acc_scr[...] + ffn_rows(tm)
            pack_y(lambda r0, r1: y[r0:r1])

        @pl.when(full & jnp.logical_not(first | last))
        def _():
            acc_scr[...] += ffn_rows(tm)

        @pl.when(jnp.logical_not(full) & first)
        def _():
            unpack_x()
            acc_scr[...] = jnp.zeros_like(acc_scr) + b2_ref[...]

        for nb in range(1, n_sub):
            @pl.when(filled == nb)
            def _(nb=nb):
                acc_scr[0:nb * sub, :] += ffn_rows(nb * sub)

        @pl.when(jnp.logical_not(full) & last)
        def _():
            pack_y(lambda r0, r1: acc_scr[r0:r1, :])

    @pl.when(jnp.logical_not(live) & (c == 0))
    def _():
        ys_ref[...] = jnp.zeros_like(ys_ref)


def _experts(tile_expert, tile_valid, n_used, xs2d, w1, b1, w2, b2, layer, tm, sub, m_pad):
    _, n_exp, d, h2 = w1.shape
    hid = h2 // 2
    tc = min(hid // 2, 512)
    nc = hid // tc
    assert nc >= 2
    n_tiles = m_pad // tm
    slab = d // 2 // LANES

    def live(i, nu):
        return jnp.minimum(i, nu[0] - 1)

    def chunk(i, c, nu):
        return jnp.where(i < nu[0], c, nc - 1)

    grid_spec = pltpu.PrefetchScalarGridSpec(
        num_scalar_prefetch=3,
        grid=(n_tiles, nc),
        in_specs=[
            pl.BlockSpec((tm * slab, LANES), lambda i, c, te, tv, nu: (live(i, nu), 0)),
            pl.BlockSpec((None, None, d, 2 * tc), lambda i, c, te, tv, nu: (layer, te[i], 0, chunk(i, c, nu))),
            pl.BlockSpec((None, None, 1, 2 * tc), lambda i, c, te, tv, nu: (layer, te[i], 0, chunk(i, c, nu))),
            pl.BlockSpec((None, None, tc, d), lambda i, c, te, tv, nu: (layer, te[i], chunk(i, c, nu), 0)),
            pl.BlockSpec((None, None, 1, d), lambda i, c, te, tv, nu: (layer, te[i], 0, 0)),
        ],
        out_specs=pl.BlockSpec((tm * slab, LANES), lambda i, c, te, tv, nu: (i, 0)),
        scratch_shapes=[pltpu.VMEM((tm, d), BF16), pltpu.VMEM((tm, d), F32)],
    )
    return pl.pallas_call(
        functools.partial(_expert_kernel, tm=tm, sub=sub, nc=nc),
        grid_spec=grid_spec,
        out_shape=jax.ShapeDtypeStruct((m_pad * slab, LANES), U32),
        compiler_params=_params("arbitrary", "arbitrary"),
        name="experts",
    )(tile_expert, tile_valid, n_used, xs2d, w1, b1, w2, b2)


def _combine_kernel(pos_ref, pos_next_ref, wts_ref, x_ref, g2_ref, fg_ref, ys_hbm, ys_flat_hbm, o_ref, buf, sem,
                    *, tm, final, n_steps):
    i = pl.program_id(0)
    n_slab = x_ref.shape[1] // 2 // LANES
    slot_rows = TOP_K * tm * n_slab
    slot = i % 2
    base = pl.multiple_of(slot * slot_rows, slot_rows)

    def issue_block(p_ref, to_slot):
        to_base = to_slot * slot_rows

        def issue(g, c):
            for u in range(ISSUE_UNROLL):
                tt = g * ISSUE_UNROLL + u
                for k in range(TOP_K):
                    row0 = pl.multiple_of(to_base + (k * tm + tt) * n_slab, n_slab)
                    pltpu.make_async_copy(ys_hbm.at[p_ref[k, tt]], buf.at[pl.ds(row0, n_slab)],
                                          sem.at[to_slot]).start(priority=k % 2)
            return c

        lax.fori_loop(0, tm // ISSUE_UNROLL, issue, 0)

    @pl.when(i == 0)
    def _():
        issue_block(pos_ref, 0)

    @pl.when(i + 1 < n_steps)
    def _():
        issue_block(pos_next_ref, 1 - slot)

    pltpu.make_async_copy(ys_flat_hbm.at[pl.ds(0, slot_rows)], buf.at[pl.ds(base, slot_rows)], sem.at[slot]).wait()

    w_sq = jnp.concatenate([wts_ref[...], jnp.zeros((tm - TOP_K, tm), F32)], axis=0)
    w_t = w_sq.T
    w_k = [jnp.broadcast_to(w_t[:, k:k + 1], (tm, LANES)) for k in range(TOP_K)]
    lows, highs = [], []
    for s in range(n_slab):
        acc_lo = jnp.zeros((tm, LANES), F32)
        acc_hi = jnp.zeros((tm, LANES), F32)
        for k in range(TOP_K):
            lo, hi = _unpack_rows(buf[pl.ds(base + k * tm * n_slab + s, tm, stride=n_slab), :])
            acc_lo = acc_lo + w_k[k] * lo
            acc_hi = acc_hi + w_k[k] * hi
        lows.append(acc_lo)
        highs.append(acc_hi)
    moe = jnp.concatenate(lows + highs, axis=1)
    x2 = x_ref[...] + g2_ref[0] * moe
    if final:
        x2 = x2 * lax.rsqrt(jnp.mean(x2 * x2, axis=-1, keepdims=True) + RMS_EPS) * fg_ref[...]
    o_ref[...] = x2


def _combine(pos, wts, x1, gate2, final_g, ys_rows, tokens_per_batch, final):
    n, d = x1.shape
    tm = 256
    steps_per_batch = tokens_per_batch // tm
    n_steps = n // tm
    m_pad, slab, _ = ys_rows.shape
    return pl.pallas_call(
        functools.partial(_combine_kernel, tm=tm, final=final, n_steps=n_steps),
        grid=(n_steps,),
        in_specs=[
            pl.BlockSpec((TOP_K, tm), lambda i: (0, i), memory_space=pltpu.SMEM),
            pl.BlockSpec((TOP_K, tm), lambda i: (0, jnp.minimum(i + 1, n_steps - 1)), memory_space=pltpu.SMEM),
            pl.BlockSpec((TOP_K, tm), lambda i: (0, i)),
            pl.BlockSpec((tm, d), lambda i: (i, 0)),
            pl.BlockSpec((1, 1, d), lambda i: (i // steps_per_batch, 0, 0)),
            pl.BlockSpec((1, d), lambda i: (0, 0)),
            pl.BlockSpec(memory_space=pl.ANY),
            pl.BlockSpec(memory_space=pl.ANY),
        ],
        out_specs=pl.BlockSpec((tm, d), lambda i: (i, 0)),
        out_shape=jax.ShapeDtypeStruct((n, d), F32),
        scratch_shapes=[pltpu.VMEM((2 * TOP_K * tm * slab, LANES), U32), pltpu.SemaphoreType.DMA((2,))],
        compiler_params=_params("arbitrary"),
        name="combine",
    )(pos, pos, wts, x1, gate2, final_g, ys_rows, ys_rows.reshape(m_pad * slab, LANES))


def _routing_plan(idx, rank, counts, tm_e, n_tiles):
    experts = jnp.arange(N_EXPERTS, dtype=I32)
    cnt = counts[:, 0].astype(I32)
    tiles = (cnt + tm_e - 1) // tm_e
    tile_end = jnp.cumsum(tiles)
    tile_start = tile_end - tiles
    offsets = tile_start * tm_e
    pos = rank + jnp.sum(jnp.where(idx[..., None] == experts, offsets, 0), axis=-1)
    n_used = tile_end[-1]
    tile_ids = jnp.arange(n_tiles, dtype=I32)
    te = jnp.sum((tile_ids[:, None] >= tile_end[None, :]).astype(I32), axis=1)
    te_last = jnp.sum((n_used - 1 >= tile_end).astype(I32))
    te = jnp.where(tile_ids < n_used, te, te_last).astype(I32)
    mine = te[:, None] == experts[None, :]
    rows_left = jnp.sum(jnp.where(mine, cnt[None, :] - (tile_ids[:, None] - tile_start[None, :]) * tm_e, 0), axis=1)
    tile_valid = jnp.where(tile_ids < n_used, jnp.clip(rows_left, 0, tm_e), 0).astype(I32)
    zero_start = jnp.where(tiles > 0, (tile_end - 1) * tm_e, -1).astype(I32)
    return pos.astype(I32), te, tile_valid, n_used.reshape(1).astype(I32), zero_start


def kernel(x, c, positions, mod_w, mod_b, norm1_g, norm2_g, w_in, pool_w, pool_scale, diff_lq1, diff_lk1,
           diff_lq2, diff_lk2, diff_subln_g, conv_dw_w, conv_dw_b, conv_ln_g, conv_ln_b, conv_pw_w, conv_pw_b,
           w_out, router_w, router_b, exp_w1, exp_b1, exp_w2, exp_b2, final_g):
    b, t, d = x.shape
    depth = mod_w.shape[0]
    n = b * t
    tm_e = 1024
    sub_e = 256
    m_pad = n * TOP_K + N_EXPERTS * tm_e
    n_tiles = m_pad // tm_e
    slab = d // 2 // LANES

    def row3(a):
        return a.reshape(a.shape[0], 1, a.shape[1])

    mod = _modulation(c, mod_w, mod_b)
    tables = _rope_tables(positions)
    w_in_b = w_in.astype(BF16)
    w_out_b = w_out.astype(BF16)
    pool_w_b = pool_w.astype(BF16)
    pw_w_b = conv_pw_w.astype(BF16)
    router_wt = jnp.swapaxes(router_w, 1, 2)
    router_b3 = router_b.reshape(depth, N_EXPERTS, 1)
    b1r = exp_b1.reshape(depth, N_EXPERTS, 1, exp_b1.shape[-1])
    b2r = exp_b2.reshape(depth, N_EXPERTS, 1, d)
    final_g2 = final_g.reshape(1, d)

    for l in range(depth):
        lambda_init = 0.8 - 0.6 * math.exp(-0.3 * l)
        sh1, sc1, g1, sh2, sc2, g2 = [m.reshape(b, 1, d) for m in jnp.split(mod[l], 6, axis=-1)]
        z = _inproj(x, row3(norm1_g), sc1, sh1, w_in_b, tables, l)
        y_pool, y_conv = _poolconv(z, pool_w_b, row3(pool_scale), conv_dw_w, row3(conv_dw_b), row3(conv_ln_g),
                                   row3(conv_ln_b), pw_w_b, row3(conv_pw_b), l)
        y_diff = _diff_attention(z, row3(diff_lq1), row3(diff_lk1), row3(diff_lq2), row3(diff_lk2),
                                 row3(diff_subln_g), l, lambda_init)
        y_dil = _dilated_attention(z)
        x1, h2, idx, rank, wts, counts = _outproj_router(
            (y_pool, y_diff, y_dil, y_conv), w_out_b, x, g1, row3(norm2_g), sc2, sh2, router_wt, router_b3, l)
        pos, tile_expert, tile_valid, n_used, zero_start = _routing_plan(idx, rank, counts, tm_e, n_tiles)
        xs = _scatter_rows(zero_start, pos, h2.reshape(n, slab, LANES), m_pad, tm_e)
        ys = _experts(tile_expert, tile_valid, n_used, xs.reshape(m_pad * slab, LANES), exp_w1, b1r, exp_w2, b2r,
                      l, tm_e, sub_e, m_pad)
        x = _combine(pos, wts, x1.reshape(n, d), g2, final_g2, ys.reshape(m_pad, slab, LANES), t,
                     final=(l == depth - 1)).reshape(b, t, d)
    return x
```

```python
import functools
import math

import numpy as np
import jax
import jax.numpy as jnp
from jax import lax
from jax.experimental import pallas as pl
from jax.experimental.pallas import tpu as pltpu

F32 = jnp.float32
BF16 = jnp.bfloat16
I32 = jnp.int32
U32 = jnp.uint32
HIGHEST = lax.Precision.HIGHEST

LANES = 128
SUBLANES = 8
VMEM_LIMIT_BYTES = 60 * 1024 * 1024

POOL_WINDOWS = (2, 4, 8, 16)
DILATED_PAIRS = ((128, 1), (512, 4), (2048, 16))
DIL_BAND = 128
CONV_WIDTH = 31
HALO = 32
ISSUE_UNROLL = 8
ROW_COPY_TOKENS = 256
N_EXPERTS = 32
TOP_K = 4
SWIGLU_ALPHA = 1.702
SWIGLU_LIMIT = 7.0
ROPE_THETA = 500000.0
RMS_EPS = 1e-6
LN_EPS = 1e-5
NEG_INF = -1e30
GROUP = 512
HEAD = 128
DIFF_QK = 64
LOG2E = math.log2(math.e)


def _params(*semantics):
    return pltpu.CompilerParams(dimension_semantics=semantics, vmem_limit_bytes=VMEM_LIMIT_BYTES)


def _sigmoid(x):
    return 1.0 / (1.0 + jnp.exp(-x))


def _mod_kernel(c_ref, w_ref, b_ref, o_ref):
    c = c_ref[...]
    ca = c * _sigmoid(c)
    w = w_ref[0]
    c_hi = ca.astype(BF16)
    c_lo = (ca - c_hi.astype(F32)).astype(BF16)
    w_hi = w.astype(BF16)
    w_lo = (w - w_hi.astype(F32)).astype(BF16)
    prod = jnp.dot(c_hi, w_hi, preferred_element_type=F32)
    prod = prod + (jnp.dot(c_lo, w_hi, preferred_element_type=F32) + jnp.dot(c_hi, w_lo, preferred_element_type=F32))
    o_ref[0] = prod + b_ref[0]


def _modulation(c, mod_w, mod_b):
    depth, d, n = mod_w.shape
    b = c.shape[0]
    rows = 8
    c_pad = jnp.zeros((rows, d), F32).at[:b].set(c)
    tn = 512
    out = pl.pallas_call(
        _mod_kernel,
        grid=(depth, n // tn),
        in_specs=[
            pl.BlockSpec((rows, d), lambda l, j: (0, 0)),
            pl.BlockSpec((1, d, tn), lambda l, j: (l, 0, j)),
            pl.BlockSpec((1, 1, tn), lambda l, j: (l, 0, j)),
        ],
        out_specs=pl.BlockSpec((1, rows, tn), lambda l, j: (l, 0, j)),
        out_shape=jax.ShapeDtypeStruct((depth, rows, n), F32),
        compiler_params=_params("arbitrary", "arbitrary"),
        name="modulation",
    )(c_pad, mod_w, mod_b.reshape(depth, 1, n))
    return out[:, :b]


def _rope_table_kernel(pos_ref, invd_ref, invc_ref, cd_ref, sd_ref, cc_ref, sc_ref):
    p = pos_ref[0]
    lane = lax.broadcasted_iota(I32, (1, LANES), 1)
    for inv_ref, c_ref, s_ref, hd in ((invd_ref, cd_ref, sd_ref, DIFF_QK), (invc_ref, cc_ref, sc_ref, HEAD)):
        half = hd // 8
        lm = lane % hd
        ang = p * inv_ref[...]
        c_ref[0] = jnp.cos(ang)
        s_ref[0] = jnp.where(lm < half, -jnp.sin(ang), jnp.sin(ang))


def _lane_inv_freq(hd):
    half = hd // 8
    inv = ROPE_THETA ** (-jnp.arange(half, dtype=F32) / half)
    lm = np.arange(LANES) % hd
    rotated = jnp.asarray(lm < 2 * half)
    return jnp.where(rotated, inv[lm % half], 0.0).reshape(1, LANES).astype(F32)


def _rope_tables(positions):
    b, t = positions.shape
    tm = min(t, 1024)
    pos = positions.astype(F32).reshape(b, t, 1)
    spec_t = pl.BlockSpec((1, tm, LANES), lambda bi, i: (bi, i, 0))
    spec_inv = pl.BlockSpec((1, LANES), lambda bi, i: (0, 0))
    shp = jax.ShapeDtypeStruct((b, t, LANES), F32)
    return pl.pallas_call(
        _rope_table_kernel,
        grid=(b, t // tm),
        in_specs=[pl.BlockSpec((1, tm, 1), lambda bi, i: (bi, i, 0)), spec_inv, spec_inv],
        out_specs=[spec_t] * 4,
        out_shape=[shp] * 4,
        compiler_params=_params("arbitrary", "arbitrary"),
        name="rope_tables",
    )(pos, _lane_inv_freq(DIFF_QK), _lane_inv_freq(HEAD))


def _rope_apply(z, cos, sin, hd):
    half = hd // 8
    lane = lax.broadcasted_iota(I32, (1, LANES), 1)
    first = (lane % hd) < half
    outs = []
    for cb in range(z.shape[1] // LANES):
        zc = z[:, cb * LANES:(cb + 1) * LANES]
        partner = jnp.where(first, pltpu.roll(zc, LANES - half, 1), pltpu.roll(zc, half, 1))
        outs.append(zc * cos + partner * sin)
    return jnp.concatenate(outs, axis=1)


def _inproj_kernel(x_ref, g_ref, sc_ref, sh_ref, w_ref, cd_ref, sd_ref, cc_ref, sc2_ref, z_ref, h_scr):
    j = pl.program_id(2)

    def project():
        return jnp.dot(h_scr[...], w_ref[...], preferred_element_type=F32)

    @pl.when(j == 0)
    def _():
        x = x_ref[0]
        y = x * lax.rsqrt(jnp.mean(x * x, axis=-1, keepdims=True) + RMS_EPS) * g_ref[0]
        h = (y * (1.0 + sc_ref[0]) + sh_ref[0]).astype(BF16)
        h_scr[...] = h
        z_ref[0] = jnp.dot(h, w_ref[...], preferred_element_type=F32).astype(z_ref.dtype)

    @pl.when((j == 1) | (j == 2))
    def _():
        r = _rope_apply(project(), cd_ref[0], sd_ref[0], DIFF_QK)
        r = r * jnp.where(j == 1, DIFF_QK ** -0.5 * LOG2E, 1.0)
        z_ref[0] = r.astype(z_ref.dtype)

    @pl.when((j == 4) | (j == 5))
    def _():
        r = _rope_apply(project(), cc_ref[0], sc2_ref[0], HEAD)
        r = r * jnp.where(j == 4, HEAD ** -0.5 * LOG2E, 1.0)
        z_ref[0] = r.astype(z_ref.dtype)

    @pl.when((j == 3) | (j >= 6))
    def _():
        z_ref[0] = project().astype(z_ref.dtype)


def _inproj(x, norm_g, scale, shift, w_in_bf16, tables, layer):
    b, t, d = x.shape
    n = w_in_bf16.shape[-1]
    tm = min(t, 1024)
    tn = GROUP
    cd, sd, cc, sc = tables
    spec_tab = pl.BlockSpec((1, tm, LANES), lambda bi, i, j: (bi, i, 0))
    spec_vec = pl.BlockSpec((1, 1, d), lambda bi, i, j: (bi, 0, 0))
    return pl.pallas_call(
        _inproj_kernel,
        grid=(b, t // tm, n // tn),
        in_specs=[
            pl.BlockSpec((1, tm, d), lambda bi, i, j: (bi, i, 0)),
            pl.BlockSpec((1, 1, d), lambda bi, i, j: (layer, 0, 0)),
            spec_vec, spec_vec,
            pl.BlockSpec((None, d, tn), lambda bi, i, j: (layer, 0, j)),
            spec_tab, spec_tab, spec_tab, spec_tab,
        ],
        out_specs=pl.BlockSpec((1, tm, tn), lambda bi, i, j: (bi, i, j)),
        out_shape=jax.ShapeDtypeStruct((b, t, n), BF16),
        scratch_shapes=[pltpu.VMEM((tm, d), BF16)],
        compiler_params=_params("arbitrary", "arbitrary", "arbitrary"),
        name="inproj",
    )(x, norm_g, scale, shift, w_in_bf16, cd, sd, cc, sc)


def _poolconv_kernel(zp_ref, zph_ref, za_ref, zah_ref, zg_ref, zgh_ref,
                     pw_ref, ps_ref, dww_ref, dwb_ref, lng_ref, lnb_ref, pww_ref, pwb_ref,
                     yp_ref, yc_ref, xp_scr, u_scr, ush_scr, *, tm):
    i = pl.program_id(1)
    keep = jnp.where(i == 0, 0.0, 1.0)

    xp_scr[0:HALO, :] = zph_ref[0].astype(F32) * keep
    xp_scr[HALO:, :] = zp_ref[0].astype(F32)
    t_glob = i * tm + lax.broadcasted_iota(I32, (tm, 1), 0)
    for g, w in enumerate(POOL_WINDOWS):
        cols = slice(g * LANES, (g + 1) * LANES)
        xg = xp_scr[HALO:HALO + tm, cols]
        acc = xg
        for k in range(1, w):
            acc = acc + xp_scr[HALO - k:HALO - k + tm, cols]
        cnt = jnp.minimum(t_glob + 1, w).astype(F32)
        pooled = acc / cnt - xg
        yg = jnp.dot(pooled.astype(BF16), pw_ref[0, g], preferred_element_type=F32)
        yp_ref[0, :, cols] = (yg * ps_ref[0, :, cols]).astype(yp_ref.dtype)

    ah = zah_ref[0].astype(F32)
    gh = zgh_ref[0].astype(F32)
    u_scr[0:HALO, :] = ah * _sigmoid(gh) * keep
    a = za_ref[0].astype(F32)
    gg = zg_ref[0].astype(F32)
    u_scr[HALO:, :] = a * _sigmoid(gg)
    span = tm + HALO - SUBLANES
    for sh in range(1, SUBLANES):
        ush_scr[sh - 1] = u_scr[sh:sh + span, :]
    acc = jnp.zeros((tm, GROUP), F32) + dwb_ref[0]
    base = HALO - (CONV_WIDTH - 1)
    for k in range(CONV_WIDTH):
        off = base + k
        sh, start = off % SUBLANES, off - off % SUBLANES
        window = u_scr[start:start + tm, :] if sh == 0 else ush_scr[sh - 1, start:start + tm, :]
        acc = acc + window * dww_ref[0, k:k + 1, :]
    mu = jnp.mean(acc, axis=-1, keepdims=True)
    cen = acc - mu
    var = jnp.mean(cen * cen, axis=-1, keepdims=True)
    v = cen * lax.rsqrt(var + LN_EPS) * lng_ref[0] + lnb_ref[0]
    v = v * _sigmoid(v)
    y = jnp.dot(v.astype(BF16), pww_ref[0], preferred_element_type=F32) + pwb_ref[0]
    yc_ref[0] = y.astype(yc_ref.dtype)


def _poolconv(z, pool_w_bf16, pool_scale, dw_w, dw_b, ln_g, ln_b, pw_w_bf16, pw_b, layer):
    b, t, _ = z.shape
    tm = min(t, 512)
    r = tm // HALO
    a_blk = 7
    g_blk = 8

    def cur(col):
        return pl.BlockSpec((1, tm, GROUP), lambda bi, i: (bi, i, col))

    def halo(col):
        return pl.BlockSpec((1, HALO, GROUP), lambda bi, i: (bi, jnp.maximum(i * r - 1, 0), col))

    def vec(n):
        return pl.BlockSpec((1, 1, n), lambda bi, i: (layer, 0, 0))

    out_spec = pl.BlockSpec((1, tm, GROUP), lambda bi, i: (bi, i, 0))
    shp = jax.ShapeDtypeStruct((b, t, GROUP), BF16)
    return pl.pallas_call(
        functools.partial(_poolconv_kernel, tm=tm),
        grid=(b, t // tm),
        in_specs=[
            cur(0), halo(0), cur(a_blk), halo(a_blk), cur(g_blk), halo(g_blk),
            pl.BlockSpec((1, len(POOL_WINDOWS), LANES, LANES), lambda bi, i: (layer, 0, 0, 0)),
            vec(GROUP),
            pl.BlockSpec((1, CONV_WIDTH, GROUP), lambda bi, i: (layer, 0, 0)),
            vec(GROUP), vec(GROUP), vec(GROUP),
            pl.BlockSpec((1, GROUP, GROUP), lambda bi, i: (layer, 0, 0)),
            vec(GROUP),
        ],
        out_specs=[out_spec, out_spec],
        out_shape=[shp, shp],
        scratch_shapes=[pltpu.VMEM((tm + HALO, GROUP), F32), pltpu.VMEM((tm + HALO, GROUP), F32),
                        pltpu.VMEM((SUBLANES - 1, tm + HALO - SUBLANES, GROUP), F32)],
        compiler_params=_params("arbitrary", "arbitrary"),
        name="pool_conv",
    )(z, z, z, z, z, z, pool_w_bf16, pool_scale, dw_w, dw_b, ln_g, ln_b, pw_w_bf16, pw_b)


def _diff_kernel(q_ref, k_ref, v_ref, lq1_ref, lk1_ref, lq2_ref, lk2_ref, g_ref, o_ref, *, tq, lambda_init):
    qi = pl.program_id(2)
    q = q_ref[0]
    lane = lax.broadcasted_iota(I32, (tq, HEAD), 1)
    zero = jnp.zeros_like(q)
    q2 = jnp.concatenate([jnp.where(lane < DIFF_QK, q, zero), jnp.where(lane >= DIFF_QK, q, zero)], axis=0)
    rows = 2 * tq

    def lane_tiles(s):
        return [s[:, c * LANES:(c + 1) * LANES] for c in range(tq // LANES)]

    def fold(j, carry, masked):
        m, l_run, acc = carry
        kb = k_ref[0, pl.ds(pl.multiple_of(j * tq, tq), tq), :]
        vb = v_ref[0, pl.ds(pl.multiple_of(j * tq, tq), tq), :]
        s = lax.dot_general(q2, kb, (((1,), (1,)), ((), ())), preferred_element_type=F32)
        if masked:
            row = lax.broadcasted_iota(I32, (rows, tq), 0) % tq
            col = lax.broadcasted_iota(I32, (rows, tq), 1)
            s = jnp.where(col <= row, s, NEG_INF)
        tiles = lane_tiles(s)
        m_blk = tiles[0]
        for part in tiles[1:]:
            m_blk = jnp.maximum(m_blk, part)
        m_new = jnp.maximum(m, jnp.max(m_blk, axis=-1, keepdims=True))
        alpha = jnp.exp2(m - m_new)
        p = jnp.exp2(s - m_new)
        l_run = alpha * l_run
        for part in lane_tiles(p):
            l_run = l_run + part
        acc = alpha * acc + jnp.dot(p.astype(BF16), vb, preferred_element_type=F32)
        return m_new, l_run, acc

    init = (jnp.full((rows, 1), NEG_INF, F32), jnp.zeros((rows, LANES), F32), jnp.zeros((rows, HEAD), F32))
    carry = lax.fori_loop(0, qi, lambda j, c: fold(j, c, False), init)
    _, l_run, acc = fold(qi, carry, True)
    l = jnp.sum(l_run, axis=-1, keepdims=True)

    lam =(jnp.exp(jnp.sum(lq1_ref[0] * lk1_ref[0], axis=-1, keepdims=True))
           - jnp.exp(jnp.sum(lq2_ref[0] * lk2_ref[0], axis=-1, keepdims=True)) + lambda_init)
    o = acc / l
    od = o[:tq] - lam * o[tq:]
    y = od * lax.rsqrt(jnp.mean(od * od, axis=-1, keepdims=True) + RMS_EPS) * g_ref[0]
    o_ref[0] = (y * (1.0 - lambda_init)).astype(o_ref.dtype)


def _diff_attention(z, lq1, lk1, lq2, lk2, subln_g, layer, lambda_init):
    b, t, _ = z.shape
    heads = GROUP // HEAD
    tq = min(t, 512)
    q0, k0, v0 = 1 * heads, 2 * heads, 3 * heads

    def vec(n):
        return pl.BlockSpec((1, 1, n), lambda bi, h, i: (layer, 0, 0))

    return pl.pallas_call(
        functools.partial(_diff_kernel, tq=tq, lambda_init=lambda_init),
        grid=(b, heads, t // tq),
        in_specs=[
            pl.BlockSpec((1, tq, HEAD), lambda bi, h, i: (bi, i, q0 + h)),
            pl.BlockSpec((1, t, HEAD), lambda bi, h, i: (bi, 0, k0 + h)),
            pl.BlockSpec((1, t, HEAD), lambda bi, h, i: (bi, 0, v0 + h)),
            vec(DIFF_QK), vec(DIFF_QK), vec(DIFF_QK), vec(DIFF_QK), vec(HEAD),
        ],
        out_specs=pl.BlockSpec((1, tq, HEAD), lambda bi, h, i: (bi, i, h)),
        out_shape=jax.ShapeDtypeStruct((b, t, GROUP), BF16),
        compiler_params=_params("arbitrary", "arbitrary", "arbitrary"),
        name="diff_attention",
    )(z, z, z, lq1, lk1, lq2, lk2, subln_g)


def _dilated_kernel(q_ref, k_ref, v_ref, o_ref, qf, kf, vf, o0, o1, o2, e0, e1, e2, *, t, group):
    qf[...] = q_ref[0].astype(F32)
    kf[...] = k_ref[0].astype(F32)
    vf[...] = v_ref[0].astype(F32)
    band = DIL_BAND
    a_idx = lax.broadcasted_iota(I32, (band, 2 * band), 0)
    b_idx = lax.broadcasted_iota(I32, (band, 2 * band), 1)
    in_band = (b_idx >= a_idx) & (b_idx <= a_idx + band)
    outs = ((o0, e0), (o1, e1), (o2, e2))

    for (window, dil), (o_scr, e_scr) in zip(DILATED_PAIRS, outs):
        nb = t // (dil * band)

        def one_block(it, dil=dil, nb=nb, o_scr=o_scr, e_scr=e_scr):
            r = it // nb
            n = it % nb
            start = r + n * (band * dil)
            prev = r + jnp.maximum(n - 1, 0) * (band * dil)

            def rows(ref, s0):
                return ref[pl.ds(s0, band, stride=dil), :] if dil > 1 else ref[pl.ds(s0, band), :]

            qb = rows(qf, start).astype(BF16)
            kk = jnp.concatenate([rows(kf, prev), rows(kf, start)], axis=0).astype(BF16)
            vv = jnp.concatenate([rows(vf, prev), rows(vf, start)], axis=0).astype(BF16)
            s = lax.dot_general(qb, kk, (((1,), (1,)), ((), ())), preferred_element_type=F32)
            valid = in_band & (b_idx >= jnp.where(n > 0, 0, band))
            s = jnp.where(valid, s, NEG_INF)
            m = jnp.max(s, axis=-1, keepdims=True)
            p = jnp.exp2(s - m)
            l = jnp.sum(p, axis=-1, keepdims=True)
            o = jnp.dot(p.astype(BF16), vv, preferred_element_type=F32) / l
            idx = pl.ds(start, band, stride=dil) if dil > 1 else pl.ds(start, band)
            o_scr[idx, :] = o
            e_scr[idx, :] = jnp.broadcast_to(m + jnp.log2(l), (band, HEAD))

        def blocks(g, carry, one_block=one_block):
            for u in range(group):
                one_block(g * group + u)
            return carry

        lax.fori_loop(0, dil * nb // group, blocks, 0)

    top = jnp.maximum(jnp.maximum(e0[...], e1[...]), e2[...])
    w0 = jnp.exp2(e0[...] - top)
    w1 = jnp.exp2(e1[...] - top)
    w2 = jnp.exp2(e2[...] - top)
    mix = (w0 * o0[...] + w1 * o1[...] + w2 * o2[...]) / (w0 + w1 + w2)
    o_ref[0] = mix.astype(o_ref.dtype)


def _dilated_attention(z):
    b, t, _ = z.shape
    heads = GROUP // HEAD
    q0, k0, v0 = 4 * heads, 5 * heads, 6 * heads
    group = 16
    assert all(w // dl == DIL_BAND and t % w == 0 for w, dl in DILATED_PAIRS)
    assert (t // DIL_BAND) % group == 0

    def col(c0):
        return pl.BlockSpec((1, t, HEAD), lambda bi, h: (bi, 0, c0 + h))

    return pl.pallas_call(
        functools.partial(_dilated_kernel, t=t, group=group),
        grid=(b, heads),
        in_specs=[col(q0), col(k0), col(v0)],
        out_specs=pl.BlockSpec((1, t, HEAD), lambda bi, h: (bi, 0, h)),
        out_shape=jax.ShapeDtypeStruct((b, t, GROUP), BF16),
        scratch_shapes=[pltpu.VMEM((t, HEAD), F32)] * 9,
        compiler_params=_params("arbitrary", "arbitrary"),
        name="dilated_attention",
    )(z, z, z)


def _pack_rows(h, out_ref, row0=0):
    m, half = h.shape[0], h.shape[1] // 2
    bits = pltpu.bitcast(h.astype(BF16).astype(F32), U32)
    word = (bits[:, half:] & jnp.uint32(0xFFFF0000)) | (bits[:, :half] >> 16)
    n_slab = half // LANES
    for s in range(n_slab):
        out_ref[pl.ds(row0 * n_slab + s, m, stride=n_slab), :] = word[:, s * LANES:(s + 1) * LANES]


def _unpack_rows(word):
    return pltpu.bitcast(word << 16, F32), pltpu.bitcast(word & jnp.uint32(0xFFFF0000), F32)


def _outproj_router_kernel(yp_ref, yd_ref, yc_ref, yv_ref, wo_ref, x_ref, g1_ref, n2_ref, sc_ref, sh_ref,
                           rw_ref, rb_ref,
                           x1_ref, h2_ref, idx_ref, rank_ref, wts_ref, cnt_ref, carry_scr, *, tm):
    first = (pl.program_id(0) == 0) & (pl.program_id(1) == 0)

    @pl.when(first)
    def _():
        carry_scr[...] = jnp.zeros_like(carry_scr)

    mix = jnp.dot(yp_ref[0], wo_ref[0 * GROUP:1 * GROUP, :], preferred_element_type=F32)
    mix = mix + jnp.dot(yd_ref[0], wo_ref[1 * GROUP:2 * GROUP, :], preferred_element_type=F32)
    mix = mix + jnp.dot(yc_ref[0], wo_ref[2 * GROUP:3 * GROUP, :], preferred_element_type=F32)
    mix = mix + jnp.dot(yv_ref[0], wo_ref[3 * GROUP:4 * GROUP, :], preferred_element_type=F32)
    x1 = x_ref[0] + g1_ref[0] * mix
    x1_ref[0] = x1

    y = x1 * lax.rsqrt(jnp.mean(x1 * x1, axis=-1, keepdims=True) + RMS_EPS) * n2_ref[0]
    h2 = y * (1.0 + sc_ref[0]) + sh_ref[0]
    _pack_rows(h2, h2_ref)

    logits = lax.dot_general(rw_ref[0], h2, (((1,), (1,)), ((), ())), precision=HIGHEST,
                             preferred_element_type=F32) + rb_ref[0]
    e_idx = lax.broadcasted_iota(I32, (N_EXPERTS, tm), 0)
    work = logits
    vals, sels, hots = [], [], []
    for _ in range(TOP_K):
        mx = jnp.max(work, axis=0, keepdims=True)
        sel = jnp.min(jnp.where(work == mx, e_idx, N_EXPERTS), axis=0, keepdims=True)
        hot = e_idx == sel
        vals.append(mx)
        sels.append(sel)
        hots.append(hot)
        work = jnp.where(hot, -jnp.inf, work)
    exps = [jnp.exp(v - vals[0]) for v in vals]
    denom = exps[0] + exps[1] + exps[2] + exps[3]

    chosen = jnp.zeros((N_EXPERTS, tm), F32)
    for hot in hots:
        chosen = chosen + hot.astype(F32)
    s_idx = lax.broadcasted_iota(I32, (tm, tm), 0)
    t_idx = lax.broadcasted_iota(I32, (tm, tm), 1)
    upper = (s_idx < t_idx).astype(BF16)
    before = jnp.dot(chosen.astype(BF16), upper, preferred_element_type=F32) + carry_scr[:, 0:1]
    for k in range(TOP_K):
        idx_ref[k:k + 1, :] = sels[k]
        rank_ref[k:k + 1, :] = jnp.sum(jnp.where(hots[k], before, 0.0), axis=0, keepdims=True).astype(I32)
        wts_ref[k:k + 1, :] = exps[k] / denom
    carry_scr[...] = carry_scr[...] + jnp.sum(chosen, axis=1, keepdims=True)
    cnt_ref[...] = carry_scr[...]


def _outproj_router(ys, w_out_bf16, x, gate1, norm2_g, scale2, shift2, router_wt, router_b, layer):
    b, t, d = x.shape
    tm = min(t, 512)
    n = b * t
    nt = t // tm
    slab = d // 2 // LANES

    def ytile():
        return pl.BlockSpec((1, tm, GROUP), lambda bi, i: (bi, i, 0))

    def bvec():
        return pl.BlockSpec((1, 1, d), lambda bi, i: (bi, 0, 0))

    tok = pl.BlockSpec((TOP_K, tm), lambda bi, i: (0, bi * nt + i))
    outs = pl.pallas_call(
        functools.partial(_outproj_router_kernel, tm=tm),
        grid=(b, nt),
        in_specs=[
            ytile(), ytile(), ytile(), ytile(),
            pl.BlockSpec((None, 4 * GROUP, d), lambda bi, i: (layer, 0, 0)),
            pl.BlockSpec((1, tm, d), lambda bi, i: (bi, i, 0)),
            bvec(),
            pl.BlockSpec((1, 1, d), lambda bi, i: (layer, 0, 0)),
            bvec(), bvec(),
            pl.BlockSpec((1, N_EXPERTS, d), lambda bi, i: (layer, 0, 0)),
            pl.BlockSpec((1, N_EXPERTS, 1), lambda bi, i: (layer, 0, 0)),
        ],
        out_specs=[
            pl.BlockSpec((1, tm, d), lambda bi, i: (bi, i, 0)),
            pl.BlockSpec((tm * slab, LANES), lambda bi, i: (bi * nt + i, 0)),
            tok, tok, tok,
            pl.BlockSpec((N_EXPERTS, LANES), lambda bi, i: (0, 0)),
        ],
        out_shape=[
            jax.ShapeDtypeStruct((b, t, d), F32),
            jax.ShapeDtypeStruct((n * slab, LANES), U32),
            jax.ShapeDtypeStruct((TOP_K, n), I32),
            jax.ShapeDtypeStruct((TOP_K, n), I32),
            jax.ShapeDtypeStruct((TOP_K, n), F32),
            jax.ShapeDtypeStruct((N_EXPERTS, LANES), F32),
        ],
        scratch_shapes=[pltpu.VMEM((N_EXPERTS, LANES), F32)],
        compiler_params=_params("arbitrary", "arbitrary"),
        name="outproj_router",
    )(*ys, w_out_bf16, x, gate1, norm2_g, scale2, shift2, router_wt, router_b)
    return outs


def _scatter_kernel(zs_ref, pos_ref, h_ref, xs_hbm, zbuf, zsem, sem, *, tm, tm_e):
    zrows = zbuf.shape[0]

    @pl.when(pl.program_id(0) == 0)
    def _():
        zbuf[...] = jnp.zeros_like(zbuf)

        def zero(e, c):
            @pl.when(zs_ref[e] >= 0)
            def _():
                for q in range(tm_e // zrows):
                    pltpu.make_async_copy(zbuf, xs_hbm.at[pl.ds(zs_ref[e] + q * zrows, zrows)], zsem).start()
            return c

        lax.fori_loop(0, N_EXPERTS, zero, 0)

        def zero_done(e, c):
            @pl.when(zs_ref[e] >= 0)
            def _():
                for q in range(tm_e // zrows):
                    pltpu.make_async_copy(zbuf, xs_hbm.at[pl.ds(0, zrows)], zsem).wait()
            return c

        lax.fori_loop(0, N_EXPERTS, zero_done, 0)

    def issue(g, c):
        for u in range(ISSUE_UNROLL):
            tt = g * ISSUE_UNROLL + u
            for k in range(TOP_K):
                pltpu.make_async_copy(h_ref.at[tt], xs_hbm.at[pos_ref[k * tm + tt]], sem).start(priority=k % 2)
        return c

    lax.fori_loop(0, tm // ISSUE_UNROLL, issue, 0)
    for k in range(TOP_K):
        pltpu.make_async_copy(h_ref, xs_hbm.at[pl.ds(0, tm)], sem).wait()


def _block_major_positions(pos, tm):
    n = pos.shape[1]
    return pos.reshape(TOP_K, n // tm, tm).transpose(1, 0, 2).reshape(-1)


def _scatter_rows(zero_start, pos_flat, h_rows, m_pad, tm_e):
    n, slab, _ = h_rows.shape
    tm = ROW_COPY_TOKENS
    zrows = min(tm_e, 256)
    grid_spec = pltpu.PrefetchScalarGridSpec(
        num_scalar_prefetch=1,
        grid=(n // tm,),
        in_specs=[
            pl.BlockSpec((TOP_K * tm,), lambda i, zs: (i,), memory_space=pltpu.SMEM),
            pl.BlockSpec((tm, slab, LANES), lambda i, zs: (i, 0, 0)),
        ],
        out_specs=pl.BlockSpec(memory_space=pl.ANY),
        scratch_shapes=[pltpu.VMEM((zrows, slab, LANES), U32), pltpu.SemaphoreType.DMA(()),
                        pltpu.SemaphoreType.DMA(())],
    )
    return pl.pallas_call(
        functools.partial(_scatter_kernel, tm=tm, tm_e=tm_e),
        grid_spec=grid_spec,
        out_shape=jax.ShapeDtypeStruct((m_pad, slab, LANES), U32),
        compiler_params=_params("arbitrary"),
        name="scatter_rows",
    )(zero_start, pos_flat, h_rows)


def _deinterleave(hh):
    m, width = hh.shape
    lane = lax.broadcasted_iota(I32, (m, LANES), 1)
    low = lane < LANES // 2
    evens_then_odds = jnp.where(low, 2 * lane, 2 * lane - (LANES - 1))
    parts = [jnp.take_along_axis(hh[:, b * LANES:(b + 1) * LANES], evens_then_odds, axis=1)
             for b in range(width // LANES)]
    gates, lins = [], []
    for b in range(0, len(parts), 2):
        first, second = parts[b], parts[b + 1]
        gates.append(jnp.where(low, first, pltpu.roll(second, LANES // 2, 1)))
        lins.append(jnp.where(low, pltpu.roll(first, LANES // 2, 1), second))
    return jnp.concatenate(gates, axis=1), jnp.concatenate(lins, axis=1)


def _expert_kernel(te_ref, tv_ref, nu_ref, xs_ref, w1_ref, b1_ref, w2_ref, b2_ref, ys_ref,
                   x_scr, acc_scr, *, tm, sub, nc):
    i = pl.program_id(0)
    c = pl.program_id(1)
    d = x_scr.shape[1]
    n_slab = d // 2 // LANES
    live = i < nu_ref[0]

    def unpack_x():
        for s in range(n_slab):
            lo, hi = _unpack_rows(xs_ref[pl.ds(s, tm, stride=n_slab), :])
            x_scr[:, s * LANES:(s + 1) * LANES] = lo.astype(BF16)
            x_scr[:, d // 2 + s * LANES:d // 2 + (s + 1) * LANES] = hi.astype(BF16)

    def ffn_rows(n_rows):
        hh = jnp.dot(x_scr[0:n_rows, :], w1_ref[...].astype(BF16), preferred_element_type=F32) + b1_ref[...]
        g, lin = _deinterleave(hh)
        g = jnp.minimum(g, SWIGLU_LIMIT)
        lin = jnp.clip(lin, -SWIGLU_LIMIT, SWIGLU_LIMIT)
        act = g * _sigmoid(SWIGLU_ALPHA * g) * (lin + 1.0)
        return jnp.dot(act.astype(BF16), w2_ref[...].astype(BF16), preferred_element_type=F32)

    def pack_y(read_rows):
        for sb in range(tm // sub):
            _pack_rows(read_rows(sb * sub, (sb + 1) * sub), ys_ref, row0=sb * sub)

    @pl.when(live)
    def _():
        n_sub = tm // sub
        filled = (tv_ref[i] + sub - 1) // sub
        full = filled == n_sub
        first = c == 0
        last = c == nc - 1

        @pl.when(full & first)
        def _():
            unpack_x()
            acc_scr[...] = b2_ref[...] + ffn_rows(tm)

        @pl.when(full & last)
        def _():
            y = acc_scr[...] + ffn_rows(tm)
            pack_y(lambda r0, r1: y[r0:r1])

        @pl.when(full & jnp.logical_not(first | last))
        def _():
            acc_scr[...] += ffn_rows(tm)

        @pl.when(jnp.logical_not(full) & first)
        def _():
            unpack_x()
            acc_scr[...] = jnp.zeros_like(acc_scr) + b2_ref[...]

        for nb in range(1, n_sub):
            @pl.when(filled == nb)
            def _(nb=nb):
                acc_scr[0:nb * sub, :] += ffn_rows(nb * sub)

        @pl.when(jnp.logical_not(full) & last)
        def _():
            pack_y(lambda r0, r1: acc_scr[r0:r1, :])

    @pl.when(jnp.logical_not(live) & (c == 0))
    def _():
        ys_ref[...] = jnp.zeros_like(ys_ref)


def _experts(tile_expert, tile_valid, n_used, xs2d, w1, b1, w2, b2, layer, tm, sub, m_pad):
    _, n_exp, d, h2 = w1.shape
    hid = h2 // 2
    tc = min(hid // 2, 512)
    nc = hid // tc
    assert nc >= 2
    n_tiles = m_pad // tm
    slab = d // 2 // LANES

    def live(i, nu):
        return jnp.minimum(i, nu[0] - 1)

    def chunk(i, c, nu):
        return jnp.where(i < nu[0], c, nc - 1)

    grid_spec = pltpu.PrefetchScalarGridSpec(
        num_scalar_prefetch=3,
        grid=(n_tiles, nc),
        in_specs=[
            pl.BlockSpec((tm * slab, LANES), lambda i, c, te, tv, nu: (live(i, nu), 0)),
            pl.BlockSpec((None, None, d, 2 * tc), lambda i, c, te, tv, nu: (layer, te[i], 0, chunk(i, c, nu))),
            pl.BlockSpec((None, None, 1, 2 * tc), lambda i, c, te, tv, nu: (layer, te[i], 0, chunk(i, c, nu))),
            pl.BlockSpec((None, None, tc, d), lambda i, c, te, tv, nu: (layer, te[i], chunk(i, c, nu), 0)),
            pl.BlockSpec((None, None, 1, d), lambda i, c, te, tv, nu: (layer, te[i], 0, 0)),
        ],
        out_specs=pl.BlockSpec((tm * slab, LANES), lambda i, c, te, tv, nu: (i, 0)),
        scratch_shapes=[pltpu.VMEM((tm, d), BF16), pltpu.VMEM((tm, d), F32)],
    )
    return pl.pallas_call(
        functools.partial(_expert_kernel, tm=tm, sub=sub, nc=nc),
        grid_spec=grid_spec,
        out_shape=jax.ShapeDtypeStruct((m_pad * slab, LANES), U32),
        compiler_params=_params("arbitrary", "arbitrary"),
        name="experts",
    )(tile_expert, tile_valid, n_used, xs2d, w1, b1, w2, b2)


def _combine_kernel(pos_ref, pos_next_ref, wts_ref, x_ref, g2_ref, fg_ref, ys_hbm, ys_flat_hbm, o_ref, buf, sem,
                    *, tm, final, n_steps):
    i = pl.program_id(0)
    n_slab = x_ref.shape[1] // 2 // LANES
    slot_rows = TOP_K * tm * n_slab
    slot = i % 2
    base = pl.multiple_of(slot * slot_rows, slot_rows)

    def issue_block(p_ref, to_slot):
        to_base = to_slot * slot_rows

        def issue(g, c):
            for u in range(ISSUE_UNROLL):
                tt = g * ISSUE_UNROLL + u
                for k in range(TOP_K):
                    row0 = pl.multiple_of(to_base + (k * tm + tt) * n_slab, n_slab)
                    pltpu.make_async_copy(ys_hbm.at[p_ref[k * tm + tt]], buf.at[pl.ds(row0, n_slab)],
                                          sem.at[to_slot]).start(priority=k % 2)
            return c

        lax.fori_loop(0, tm // ISSUE_UNROLL, issue, 0)

    @pl.when(i == 0)
    def _():
        issue_block(pos_ref, 0)

    @pl.when(i + 1 < n_steps)
    def _():
        issue_block(pos_next_ref, 1 - slot)

    pltpu.make_async_copy(ys_flat_hbm.at[pl.ds(0, slot_rows)], buf.at[pl.ds(base, slot_rows)], sem.at[slot]).wait()

    w_sq = jnp.concatenate([wts_ref[...], jnp.zeros((tm - TOP_K, tm), F32)], axis=0)
    w_t = w_sq.T
    w_k = [jnp.broadcast_to(w_t[:, k:k + 1], (tm, LANES)) for k in range(TOP_K)]
    lows, highs = [], []
    for s in range(n_slab):
        acc_lo = jnp.zeros((tm, LANES), F32)
        acc_hi = jnp.zeros((tm, LANES), F32)
        for k in range(TOP_K):
            lo, hi = _unpack_rows(buf[pl.ds(base + k * tm * n_slab + s, tm, stride=n_slab), :])
            acc_lo = acc_lo + w_k[k] * lo
            acc_hi = acc_hi + w_k[k] * hi
        lows.append(acc_lo)
        highs.append(acc_hi)
    moe = jnp.concatenate(lows + highs, axis=1)
    x2 = x_ref[...] + g2_ref[0] * moe
    if final:
        x2 = x2 * lax.rsqrt(jnp.mean(x2 * x2, axis=-1, keepdims=True) + RMS_EPS) * fg_ref[...]
    o_ref[...] = x2


def _combine(pos_flat, wts, x1, gate2, final_g, ys_rows, tokens_per_batch, final):
    n, d = x1.shape
    tm = ROW_COPY_TOKENS
    steps_per_batch = tokens_per_batch // tm
    n_steps = n // tm
    m_pad, slab, _ = ys_rows.shape
    return pl.pallas_call(
        functools.partial(_combine_kernel, tm=tm, final=final, n_steps=n_steps),
        grid=(n_steps,),
        in_specs=[
            pl.BlockSpec((TOP_K * tm,), lambda i: (i,), memory_space=pltpu.SMEM),
            pl.BlockSpec((TOP_K * tm,), lambda i: (jnp.minimum(i + 1, n_steps - 1),), memory_space=pltpu.SMEM),
            pl.BlockSpec((TOP_K, tm), lambda i: (0, i)),
            pl.BlockSpec((tm, d), lambda i: (i, 0)),
            pl.BlockSpec((1, 1, d), lambda i: (i // steps_per_batch, 0, 0)),
            pl.BlockSpec((1, d), lambda i: (0, 0)),
            pl.BlockSpec(memory_space=pl.ANY),
            pl.BlockSpec(memory_space=pl.ANY),
        ],
        out_specs=pl.BlockSpec((tm, d), lambda i: (i, 0)),
        out_shape=jax.ShapeDtypeStruct((n, d), F32),
        scratch_shapes=[pltpu.VMEM((2 * TOP_K * tm * slab, LANES), U32), pltpu.SemaphoreType.DMA((2,))],
        compiler_params=_params("arbitrary"),
        name="combine",
    )(pos_flat, pos_flat, wts, x1, gate2, final_g, ys_rows, ys_rows.reshape(m_pad * slab, LANES))


def _routing_plan(idx, rank, counts, tm_e, n_tiles):
    experts = jnp.arange(N_EXPERTS, dtype=I32)
    cnt = counts[:, 0].astype(I32)
    tiles = (cnt + tm_e - 1) // tm_e
    tile_end = jnp.cumsum(tiles)
    tile_start = tile_end - tiles
    offsets = tile_start * tm_e
    pos = rank + jnp.sum(jnp.where(idx[..., None] == experts, offsets, 0), axis=-1)
    n_used = tile_end[-1]
    tile_ids = jnp.arange(n_tiles, dtype=I32)
    te = jnp.sum((tile_ids[:, None] >= tile_end[None, :]).astype(I32), axis=1)
    te_last = jnp.sum((n_used - 1 >= tile_end).astype(I32))
    te = jnp.where(tile_ids < n_used, te, te_last).astype(I32)
    mine = te[:, None] == experts[None, :]
    rows_left = jnp.sum(jnp.where(mine, cnt[None, :] - (tile_ids[:, None] - tile_start[None, :]) * tm_e, 0), axis=1)
    tile_valid = jnp.where(tile_ids < n_used, jnp.clip(rows_left, 0, tm_e), 0).astype(I32)
    zero_start = jnp.where(tiles > 0, (tile_end - 1) * tm_e, -1).astype(I32)
    return pos.astype(I32), te, tile_valid, n_used.reshape(1).astype(I32), zero_start


def kernel(x, c, positions, mod_w, mod_b, norm1_g, norm2_g, w_in, pool_w, pool_scale, diff_lq1, diff_lk1,
           diff_lq2, diff_lk2, diff_subln_g, conv_dw_w, conv_dw_b, conv_ln_g, conv_ln_b, conv_pw_w, conv_pw_b,
           w_out, router_w, router_b, exp_w1, exp_b1, exp_w2, exp_b2, final_g):
    b, t, d = x.shape
    depth = mod_w.shape[0]
    n = b * t
    tm_e = 1024
    sub_e = 256
    m_pad = n * TOP_K + N_EXPERTS * tm_e
    n_tiles = m_pad // tm_e
    slab = d // 2 // LANES

    def row3(a):
        return a.reshape(a.shape[0], 1, a.shape[1])

    mod = _modulation(c, mod_w, mod_b)
    tables = _rope_tables(positions)
    w_in_b = w_in.astype(BF16)
    w_out_b = w_out.astype(BF16)
    pool_w_b = pool_w.astype(BF16)
    pw_w_b = conv_pw_w.astype(BF16)
    router_wt = jnp.swapaxes(router_w, 1, 2)
    router_b3 = router_b.reshape(depth, N_EXPERTS, 1)
    b1r = exp_b1.reshape(depth, N_EXPERTS, 1, exp_b1.shape[-1])
    b2r = exp_b2.reshape(depth, N_EXPERTS, 1, d)
    final_g2 = final_g.reshape(1, d)

    for l in range(depth):
        lambda_init = 0.8 - 0.6 * math.exp(-0.3 * l)
        sh1, sc1, g1, sh2, sc2, g2 = [m.reshape(b, 1, d) for m in jnp.split(mod[l], 6, axis=-1)]
        z = _inproj(x, row3(norm1_g), sc1, sh1, w_in_b, tables, l)
        y_pool, y_conv = _poolconv(z, pool_w_b, row3(pool_scale), conv_dw_w, row3(conv_dw_b), row3(conv_ln_g),
                                   row3(conv_ln_b), pw_w_b, row3(conv_pw_b), l)
        y_diff = _diff_attention(z, row3(diff_lq1), row3(diff_lk1), row3(diff_lq2), row3(diff_lk2),
                                 row3(diff_subln_g), l, lambda_init)
        y_dil = _dilated_attention(z)
        x1, h2, idx, rank, wts, counts = _outproj_router(
            (y_pool, y_diff, y_dil, y_conv), w_out_b, x, g1, row3(norm2_g), sc2, sh2, router_wt, router_b3, l)
        pos, tile_expert, tile_valid, n_used, zero_start = _routing_plan(idx, rank, counts, tm_e, n_tiles)
        pos_flat = _block_major_positions(pos, ROW_COPY_TOKENS)
        xs = _scatter_rows(zero_start, pos_flat, h2.reshape(n, slab, LANES), m_pad, tm_e)
        ys = _experts(tile_expert, tile_valid, n_used, xs.reshape(m_pad * slab, LANES), exp_w1, b1r, exp_w2, b2r,
                      l, tm_e, sub_e, m_pad)
        x = _combine(pos_flat, wts, x1.reshape(n, d), g2, final_g2, ys.reshape(m_pad, slab, LANES), t,
                     final=(l == depth - 1)).reshape(b, t, d)
    return x
```

```python
import functools
import math

import numpy as np
import jax
import jax.numpy as jnp
from jax import lax
from jax.experimental import pallas as pl
from jax.experimental.pallas import tpu as pltpu

F32 = jnp.float32
BF16 = jnp.bfloat16
I32 = jnp.int32
U32 = jnp.uint32
HIGHEST = lax.Precision.HIGHEST

LANES = 128
SUBLANES = 8
VMEM_LIMIT_BYTES = 60 * 1024 * 1024

POOL_WINDOWS = (2, 4, 8, 16)
DILATED_PAIRS = ((128, 1), (512, 4), (2048, 16))
DIL_BAND = 128
CONV_WIDTH = 31
HALO = 32
ISSUE_UNROLL = 8
ROW_COPY_TOKENS = 256
N_EXPERTS = 32
TOP_K = 4
SWIGLU_ALPHA = 1.702
SWIGLU_LIMIT = 7.0
ROPE_THETA = 500000.0
RMS_EPS = 1e-6
LN_EPS = 1e-5
NEG_INF = -1e30
GROUP = 512
HEAD = 128
DIFF_QK = 64
LOG2E = math.log2(math.e)


def _params(*semantics):
    return pltpu.CompilerParams(dimension_semantics=semantics, vmem_limit_bytes=VMEM_LIMIT_BYTES)


def _sigmoid(x):
    return 1.0 / (1.0 + jnp.exp(-x))


def _mod_kernel(c_ref, w_ref, b_ref, o_ref):
    c = c_ref[...]
    ca = c * _sigmoid(c)
    w = w_ref[0]
    c_hi = ca.astype(BF16)
    c_lo = (ca - c_hi.astype(F32)).astype(BF16)
    w_hi = w.astype(BF16)
    w_lo = (w - w_hi.astype(F32)).astype(BF16)
    prod = jnp.dot(c_hi, w_hi, preferred_element_type=F32)
    prod = prod + (jnp.dot(c_lo, w_hi, preferred_element_type=F32) + jnp.dot(c_hi, w_lo, preferred_element_type=F32))
    o_ref[0] = prod + b_ref[0]


def _modulation(c, mod_w, mod_b):
    depth, d, n = mod_w.shape
    b = c.shape[0]
    rows = 8
    c_pad = jnp.zeros((rows, d), F32).at[:b].set(c)
    tn = 512
    out = pl.pallas_call(
        _mod_kernel,
        grid=(depth, n // tn),
        in_specs=[
            pl.BlockSpec((rows, d), lambda l, j: (0, 0)),
            pl.BlockSpec((1, d, tn), lambda l, j: (l, 0, j)),
            pl.BlockSpec((1, 1, tn), lambda l, j: (l, 0, j)),
        ],
        out_specs=pl.BlockSpec((1, rows, tn), lambda l, j: (l, 0, j)),
        out_shape=jax.ShapeDtypeStruct((depth, rows, n), F32),
        compiler_params=_params("arbitrary", "arbitrary"),
        name="modulation",
    )(c_pad, mod_w, mod_b.reshape(depth, 1, n))
    return out[:, :b]


def _rope_table_kernel(pos_ref, invd_ref, invc_ref, cd_ref, sd_ref, cc_ref, sc_ref):
    p = pos_ref[0]
    lane = lax.broadcasted_iota(I32, (1, LANES), 1)
    for inv_ref, c_ref, s_ref, hd in ((invd_ref, cd_ref, sd_ref, DIFF_QK), (invc_ref, cc_ref, sc_ref, HEAD)):
        half = hd // 8
        lm = lane % hd
        ang = p * inv_ref[...]
        c_ref[0] = jnp.cos(ang)
        s_ref[0] = jnp.where(lm < half, -jnp.sin(ang), jnp.sin(ang))


def _lane_inv_freq(hd):
    half = hd // 8
    inv = ROPE_THETA ** (-jnp.arange(half, dtype=F32) / half)
    lm = np.arange(LANES) % hd
    rotated = jnp.asarray(lm < 2 * half)
    return jnp.where(rotated, inv[lm % half], 0.0).reshape(1, LANES).astype(F32)


def _rope_tables(positions):
    b, t = positions.shape
    tm = min(t, 1024)
    pos = positions.astype(F32).reshape(b, t, 1)
    spec_t = pl.BlockSpec((1, tm, LANES), lambda bi, i: (bi, i, 0))
    spec_inv = pl.BlockSpec((1, LANES), lambda bi, i: (0, 0))
    shp = jax.ShapeDtypeStruct((b, t, LANES), F32)
    return pl.pallas_call(
        _rope_table_kernel,
        grid=(b, t // tm),
        in_specs=[pl.BlockSpec((1, tm, 1), lambda bi, i: (bi, i, 0)), spec_inv, spec_inv],
        out_specs=[spec_t] * 4,
        out_shape=[shp] * 4,
        compiler_params=_params("arbitrary", "arbitrary"),
        name="rope_tables",
    )(pos, _lane_inv_freq(DIFF_QK), _lane_inv_freq(HEAD))


def _rope_apply(z, cos, sin, hd):
    half = hd // 8
    lane = lax.broadcasted_iota(I32, (1, LANES), 1)
    first = (lane % hd) < half
    outs = []
    for cb in range(z.shape[1] // LANES):
        zc = z[:, cb * LANES:(cb + 1) * LANES]
        partner = jnp.where(first, pltpu.roll(zc, LANES - half, 1), pltpu.roll(zc, half, 1))
        outs.append(zc * cos + partner * sin)
    return jnp.concatenate(outs, axis=1)


def _inproj_kernel(x_ref, g_ref, sc_ref, sh_ref, w_ref, cd_ref, sd_ref, cc_ref, sc2_ref, z_ref, h_scr):
    j = pl.program_id(2)

    def project():
        return jnp.dot(h_scr[...], w_ref[...], preferred_element_type=F32)

    @pl.when(j == 0)
    def _():
        x = x_ref[0]
        y = x * lax.rsqrt(jnp.mean(x * x, axis=-1, keepdims=True) + RMS_EPS) * g_ref[0]
        h = (y * (1.0 + sc_ref[0]) + sh_ref[0]).astype(BF16)
        h_scr[...] = h
        z_ref[0] = jnp.dot(h, w_ref[...], preferred_element_type=F32).astype(z_ref.dtype)

    @pl.when((j == 1) | (j == 2))
    def _():
        r = _rope_apply(project(), cd_ref[0], sd_ref[0], DIFF_QK)
        r = r * jnp.where(j == 1, DIFF_QK ** -0.5 * LOG2E, 1.0)
        z_ref[0] = r.astype(z_ref.dtype)

    @pl.when((j == 4) | (j == 5))
    def _():
        r = _rope_apply(project(), cc_ref[0], sc2_ref[0], HEAD)
        r = r * jnp.where(j == 4, HEAD ** -0.5 * LOG2E, 1.0)
        z_ref[0] = r.astype(z_ref.dtype)

    @pl.when((j == 3) | (j >= 6))
    def _():
        z_ref[0] = project().astype(z_ref.dtype)


def _inproj(x, norm_g, scale, shift, w_in_bf16, tables, layer):
    b, t, d = x.shape
    n = w_in_bf16.shape[-1]
    tm = min(t, 1024)
    tn = GROUP
    cd, sd, cc, sc = tables
    spec_tab = pl.BlockSpec((1, tm, LANES), lambda bi, i, j: (bi, i, 0))
    spec_vec = pl.BlockSpec((1, 1, d), lambda bi, i, j: (bi, 0, 0))
    return pl.pallas_call(
        _inproj_kernel,
        grid=(b, t // tm, n // tn),
        in_specs=[
            pl.BlockSpec((1, tm, d), lambda bi, i, j: (bi, i, 0)),
            pl.BlockSpec((1, 1, d), lambda bi, i, j: (layer, 0, 0)),
            spec_vec, spec_vec,
            pl.BlockSpec((None, d, tn), lambda bi, i, j: (layer, 0, j)),
            spec_tab, spec_tab, spec_tab, spec_tab,
        ],
        out_specs=pl.BlockSpec((1, tm, tn), lambda bi, i, j: (bi, i, j)),
        out_shape=jax.ShapeDtypeStruct((b, t, n), BF16),
        scratch_shapes=[pltpu.VMEM((tm, d), BF16)],
        compiler_params=_params("arbitrary", "arbitrary", "arbitrary"),
        name="inproj",
    )(x, norm_g, scale, shift, w_in_bf16, cd, sd, cc, sc)


def _poolconv_kernel(zp_ref, zph_ref, za_ref, zah_ref, zg_ref, zgh_ref,
                     pw_ref, ps_ref, dww_ref, dwb_ref, lng_ref, lnb_ref, pww_ref, pwb_ref,
                     yp_ref, yc_ref, xp_scr, u_scr, ush_scr, *, tm):
    i = pl.program_id(1)
    keep = jnp.where(i == 0, 0.0, 1.0)

    xp_scr[0:HALO, :] = zph_ref[0].astype(F32) * keep
    xp_scr[HALO:, :] = zp_ref[0].astype(F32)
    t_glob = i * tm + lax.broadcasted_iota(I32, (tm, 1), 0)
    for g, w in enumerate(POOL_WINDOWS):
        cols = slice(g * LANES, (g + 1) * LANES)
        xg = xp_scr[HALO:HALO + tm, cols]
        acc = xg
        for k in range(1, w):
            acc = acc + xp_scr[HALO - k:HALO - k + tm, cols]
        cnt = jnp.minimum(t_glob + 1, w).astype(F32)
        pooled = acc / cnt - xg
        yg = jnp.dot(pooled.astype(BF16), pw_ref[0, g], preferred_element_type=F32)
        yp_ref[0, :, cols] = (yg * ps_ref[0, :, cols]).astype(yp_ref.dtype)

    ah = zah_ref[0].astype(F32)
    gh = zgh_ref[0].astype(F32)
    u_scr[0:HALO, :] = ah * _sigmoid(gh) * keep
    a = za_ref[0].astype(F32)
    gg = zg_ref[0].astype(F32)
    u_scr[HALO:, :] = a * _sigmoid(gg)
    span = tm + HALO - SUBLANES
    for sh in range(1, SUBLANES):
        ush_scr[sh - 1] = u_scr[sh:sh + span, :]
    acc = jnp.zeros((tm, GROUP), F32) + dwb_ref[0]
    base = HALO - (CONV_WIDTH - 1)
    for k in range(CONV_WIDTH):
        off = base + k
        sh, start = off % SUBLANES, off - off % SUBLANES
        window = u_scr[start:start + tm, :] if sh == 0 else ush_scr[sh - 1, start:start + tm, :]
        acc = acc + window * dww_ref[0, k:k + 1, :]
    mu = jnp.mean(acc, axis=-1, keepdims=True)
    cen = acc - mu
    var = jnp.mean(cen * cen, axis=-1, keepdims=True)
    v = cen * lax.rsqrt(var + LN_EPS) * lng_ref[0] + lnb_ref[0]
    v = v * _sigmoid(v)
    y = jnp.dot(v.astype(BF16), pww_ref[0], preferred_element_type=F32) + pwb_ref[0]
    yc_ref[0] = y.astype(yc_ref.dtype)


def _poolconv(z, pool_w_bf16, pool_scale, dw_w, dw_b, ln_g, ln_b, pw_w_bf16, pw_b, layer):
    b, t, _ = z.shape
    tm = min(t, 512)
    r = tm // HALO
    a_blk = 7
    g_blk = 8

    def cur(col):
        return pl.BlockSpec((1, tm, GROUP), lambda bi, i: (bi, i, col))

    def halo(col):
        return pl.BlockSpec((1, HALO, GROUP), lambda bi, i: (bi, jnp.maximum(i * r - 1, 0), col))

    def vec(n):
        return pl.BlockSpec((1, 1, n), lambda bi, i: (layer, 0, 0))

    out_spec = pl.BlockSpec((1, tm, GROUP), lambda bi, i: (bi, i, 0))
    shp = jax.ShapeDtypeStruct((b, t, GROUP), BF16)
    return pl.pallas_call(
        functools.partial(_poolconv_kernel, tm=tm),
        grid=(b, t // tm),
        in_specs=[
            cur(0), halo(0), cur(a_blk), halo(a_blk), cur(g_blk), halo(g_blk),
            pl.BlockSpec((1, len(POOL_WINDOWS), LANES, LANES), lambda bi, i: (layer, 0, 0, 0)),
            vec(GROUP),
            pl.BlockSpec((1, CONV_WIDTH, GROUP), lambda bi, i: (layer, 0, 0)),
            vec(GROUP), vec(GROUP), vec(GROUP),
            pl.BlockSpec((1, GROUP, GROUP), lambda bi, i: (layer, 0, 0)),
            vec(GROUP),
        ],
        out_specs=[out_spec, out_spec],
        out_shape=[shp, shp],
        scratch_shapes=[pltpu.VMEM((tm + HALO, GROUP), F32), pltpu.VMEM((tm + HALO, GROUP), F32),
                        pltpu.VMEM((SUBLANES - 1, tm + HALO - SUBLANES, GROUP), F32)],
        compiler_params=_params("arbitrary", "arbitrary"),
        name="pool_conv",
    )(z, z, z, z, z, z, pool_w_bf16, pool_scale, dw_w, dw_b, ln_g, ln_b, pw_w_bf16, pw_b)


def _diff_kernel(q_ref, k_ref, v_ref, lq1_ref, lk1_ref, lq2_ref, lk2_ref, g_ref, o_ref, *, tq, lambda_init):
    qi = pl.program_id(2)
    q = q_ref[0]
    lane = lax.broadcasted_iota(I32, (tq, HEAD), 1)
    zero = jnp.zeros_like(q)
    q2 = jnp.concatenate([jnp.where(lane < DIFF_QK, q, zero), jnp.where(lane >= DIFF_QK, q, zero)], axis=0)
    rows = 2 * tq

    def lane_tiles(s):
        return [s[:, c * LANES:(c + 1) * LANES] for c in range(tq // LANES)]

    def fold(j, carry, masked):
        m, l_run, acc = carry
        kb = k_ref[0, pl.ds(pl.multiple_of(j * tq, tq), tq), :]
        vb = v_ref[0, pl.ds(pl.multiple_of(j * tq, tq), tq), :]
        s = lax.dot_general(q2, kb, (((1,), (1,)), ((), ())), preferred_element_type=F32)
        if masked:
            row = lax.broadcasted_iota(I32, (rows, tq), 0) % tq
            col = lax.broadcasted_iota(I32, (rows, tq), 1)
            s = jnp.where(col <= row, s, NEG_INF)
        tiles = lane_tiles(s)
        m_blk = tiles[0]
        for part in tiles[1:]:
            m_blk = jnp.maximum(m_blk, part)
        m_new = jnp.maximum(m, jnp.max(m_blk, axis=-1, keepdims=True))
        alpha = jnp.exp2(m - m_new)
        p = jnp.exp2(s - m_new)
        l_run = alpha * l_run
        for part in lane_tiles(p):
            l_run = l_run + part
        acc = alpha * acc + jnp.dot(p.astype(BF16), vb, preferred_element_type=F32)
        return m_new, l_run, acc

    init = (jnp.full((rows, 1), NEG_INF, F32), jnp.zeros((rows, LANES), F32), jnp.zeros((rows, HEAD), F32))
    carry = lax.fori_loop(0, qi, lambda j, c: fold(j, c, False), init)
    _, l_run, acc = fold(qi, carry, True)
    l = jnp.sum(l_run, axis=-1, keepdims=True)

    lam =(jnp.exp(jnp.sum(lq1_ref[0] * lk1_ref[0], axis=-1, keepdims=True))
           - jnp.exp(jnp.sum(lq2_ref[0] * lk2_ref[0], axis=-1, keepdims=True)) + lambda_init)
    o = acc / l
    od = o[:tq] - lam * o[tq:]
    y = od * lax.rsqrt(jnp.mean(od * od, axis=-1, keepdims=True) + RMS_EPS) * g_ref[0]
    o_ref[0] = (y * (1.0 - lambda_init)).astype(o_ref.dtype)


def _diff_attention(z, lq1, lk1, lq2, lk2, subln_g, layer, lambda_init):
    b, t, _ = z.shape
    heads = GROUP // HEAD
    tq = min(t, 512)
    q0, k0, v0 = 1 * heads, 2 * heads, 3 * heads

    def vec(n):
        return pl.BlockSpec((1, 1, n), lambda bi, h, i: (layer, 0, 0))

    return pl.pallas_call(
        functools.partial(_diff_kernel, tq=tq, lambda_init=lambda_init),
        grid=(b, heads, t // tq),
        in_specs=[
            pl.BlockSpec((1, tq, HEAD), lambda bi, h, i: (bi, i, q0 + h)),
            pl.BlockSpec((1, t, HEAD), lambda bi, h, i: (bi, 0, k0 + h)),
            pl.BlockSpec((1, t, HEAD), lambda bi, h, i: (bi, 0, v0 + h)),
            vec(DIFF_QK), vec(DIFF_QK), vec(DIFF_QK), vec(DIFF_QK), vec(HEAD),
        ],
        out_specs=pl.BlockSpec((1, tq, HEAD), lambda bi, h, i: (bi, i, h)),
        out_shape=jax.ShapeDtypeStruct((b, t, GROUP), BF16),
        compiler_params=_params("arbitrary", "arbitrary", "arbitrary"),
        name="diff_attention",
    )(z, z, z, lq1, lk1, lq2, lk2, subln_g)


def _dilated_kernel(q_ref, k_ref, v_ref, o_ref, qf, kf, vf, o0, o1, o2, e0, e1, e2, *, t, group):
    qf[...] = q_ref[0].astype(F32)
    kf[...] = k_ref[0].astype(F32)
    vf[...] = v_ref[0].astype(F32)
    band = DIL_BAND
    a_idx = lax.broadcasted_iota(I32, (band, 2 * band), 0)
    b_idx = lax.broadcasted_iota(I32, (band, 2 * band), 1)
    in_band = (b_idx >= a_idx) & (b_idx <= a_idx + band)
    outs = ((o0, e0), (o1, e1), (o2, e2))

    for (window, dil), (o_scr, e_scr) in zip(DILATED_PAIRS, outs):
        nb = t // (dil * band)

        def one_block(it, dil=dil, nb=nb, o_scr=o_scr, e_scr=e_scr):
            r = it // nb
            n = it % nb
            start = r + n * (band * dil)
            prev = r + jnp.maximum(n - 1, 0) * (band * dil)

            def rows(ref, s0):
                return ref[pl.ds(s0, band, stride=dil), :] if dil > 1 else ref[pl.ds(s0, band), :]

            qb = rows(qf, start).astype(BF16)
            kk = jnp.concatenate([rows(kf, prev), rows(kf, start)], axis=0).astype(BF16)
            vv = jnp.concatenate([rows(vf, prev), rows(vf, start)], axis=0).astype(BF16)
            s = lax.dot_general(qb, kk, (((1,), (1,)), ((), ())), preferred_element_type=F32)
            valid = in_band & (b_idx >= jnp.where(n > 0, 0, band))
            s = jnp.where(valid, s, NEG_INF)
            m = jnp.max(s, axis=-1, keepdims=True)
            p = jnp.exp2(s - m)
            l = jnp.sum(p, axis=-1, keepdims=True)
            o = jnp.dot(p.astype(BF16), vv, preferred_element_type=F32) / l
            idx = pl.ds(start, band, stride=dil) if dil > 1 else pl.ds(start, band)
            o_scr[idx, :] = o
            e_scr[idx, :] = jnp.broadcast_to(m + jnp.log2(l), (band, HEAD))

        def blocks(g, carry, one_block=one_block):
            for u in range(group):
                one_block(g * group + u)
            return carry

        lax.fori_loop(0, dil * nb // group, blocks, 0)

    top = jnp.maximum(jnp.maximum(e0[...], e1[...]), e2[...])
    w0 = jnp.exp2(e0[...] - top)
    w1 = jnp.exp2(e1[...] - top)
    w2 = jnp.exp2(e2[...] - top)
    mix = (w0 * o0[...] + w1 * o1[...] + w2 * o2[...]) / (w0 + w1 + w2)
    o_ref[0] = mix.astype(o_ref.dtype)


def _dilated_attention(z):
    b, t, _ = z.shape
    heads = GROUP // HEAD
    q0, k0, v0 = 4 * heads, 5 * heads, 6 * heads
    group = 16
    assert all(w // dl == DIL_BAND and t % w == 0 for w, dl in DILATED_PAIRS)
    assert (t // DIL_BAND) % group == 0

    def col(c0):
        return pl.BlockSpec((1, t, HEAD), lambda bi, h: (bi, 0, c0 + h))

    return pl.pallas_call(
        functools.partial(_dilated_kernel, t=t, group=group),
        grid=(b, heads),
        in_specs=[col(q0), col(k0), col(v0)],
        out_specs=pl.BlockSpec((1, t, HEAD), lambda bi, h: (bi, 0, h)),
        out_shape=jax.ShapeDtypeStruct((b, t, GROUP), BF16),
        scratch_shapes=[pltpu.VMEM((t, HEAD), F32)] * 9,
        compiler_params=_params("arbitrary", "arbitrary"),
        name="dilated_attention",
    )(z, z, z)


def _pack_rows(h, out_ref, row0=0):
    m, half = h.shape[0], h.shape[1] // 2
    bits = pltpu.bitcast(h.astype(BF16).astype(F32), U32)
    word = (bits[:, half:] & jnp.uint32(0xFFFF0000)) | (bits[:, :half] >> 16)
    n_slab = half // LANES
    for s in range(n_slab):
        out_ref[pl.ds(row0 * n_slab + s, m, stride=n_slab), :] = word[:, s * LANES:(s + 1) * LANES]


def _unpack_rows(word):
    return pltpu.bitcast(word << 16, F32), pltpu.bitcast(word & jnp.uint32(0xFFFF0000), F32)


def _outproj_router_kernel(yp_ref, yd_ref, yc_ref, yv_ref, wo_ref, x_ref, g1_ref, n2_ref, sc_ref, sh_ref,
                           rw_ref, rb_ref,
                           x1_ref, h2_ref, idx_ref, rank_ref, wts_ref, cnt_ref, carry_scr, *, tm):
    first = (pl.program_id(0) == 0) & (pl.program_id(1) == 0)

    @pl.when(first)
    def _():
        carry_scr[...] = jnp.zeros_like(carry_scr)

    mix = jnp.dot(yp_ref[0], wo_ref[0 * GROUP:1 * GROUP, :], preferred_element_type=F32)
    mix = mix + jnp.dot(yd_ref[0], wo_ref[1 * GROUP:2 * GROUP, :], preferred_element_type=F32)
    mix = mix + jnp.dot(yc_ref[0], wo_ref[2 * GROUP:3 * GROUP, :], preferred_element_type=F32)
    mix = mix + jnp.dot(yv_ref[0], wo_ref[3 * GROUP:4 * GROUP, :], preferred_element_type=F32)
    x1 = x_ref[0] + g1_ref[0] * mix
    x1_ref[0] = x1

    y = x1 * lax.rsqrt(jnp.mean(x1 * x1, axis=-1, keepdims=True) + RMS_EPS) * n2_ref[0]
    h2 = y * (1.0 + sc_ref[0]) + sh_ref[0]
    _pack_rows(h2, h2_ref)

    logits = lax.dot_general(rw_ref[0], h2, (((1,), (1,)), ((), ())), precision=HIGHEST,
                             preferred_element_type=F32) + rb_ref[0]
    e_idx = lax.broadcasted_iota(I32, (N_EXPERTS, tm), 0)
    work = logits
    vals, sels, hots = [], [], []
    for _ in range(TOP_K):
        mx = jnp.max(work, axis=0, keepdims=True)
        sel = jnp.min(jnp.where(work == mx, e_idx, N_EXPERTS), axis=0, keepdims=True)
        hot = e_idx == sel
        vals.append(mx)
        sels.append(sel)
        hots.append(hot)
        work = jnp.where(hot, -jnp.inf, work)
    exps = [jnp.exp(v - vals[0]) for v in vals]
    denom = exps[0] + exps[1] + exps[2] + exps[3]

    chosen = jnp.zeros((N_EXPERTS, tm), F32)
    for hot in hots:
        chosen = chosen + hot.astype(F32)
    s_idx = lax.broadcasted_iota(I32, (tm, tm), 0)
    t_idx = lax.broadcasted_iota(I32, (tm, tm), 1)
    upper = (s_idx < t_idx).astype(BF16)
    before = jnp.dot(chosen.astype(BF16), upper, preferred_element_type=F32) + carry_scr[:, 0:1]
    for k in range(TOP_K):
        idx_ref[k:k + 1, :] = sels[k]
        rank_ref[k:k + 1, :] = jnp.sum(jnp.where(hots[k], before, 0.0), axis=0, keepdims=True).astype(I32)
        wts_ref[k:k + 1, :] = exps[k] / denom
    carry_scr[...] = carry_scr[...] + jnp.sum(chosen, axis=1, keepdims=True)
    cnt_ref[...] = carry_scr[...]


def _outproj_router(ys, w_out_bf16, x, gate1, norm2_g, scale2, shift2, router_wt, router_b, layer):
    b, t, d = x.shape
    tm = min(t, 512)
    n = b * t
    nt = t // tm
    slab = d // 2 // LANES

    def ytile():
        return pl.BlockSpec((1, tm, GROUP), lambda bi, i: (bi, i, 0))

    def bvec():
        return pl.BlockSpec((1, 1, d), lambda bi, i: (bi, 0, 0))

    tok = pl.BlockSpec((TOP_K, tm), lambda bi, i: (0, bi * nt + i))
    outs = pl.pallas_call(
        functools.partial(_outproj_router_kernel, tm=tm),
        grid=(b, nt),
        in_specs=[
            ytile(), ytile(), ytile(), ytile(),
            pl.BlockSpec((None, 4 * GROUP, d), lambda bi, i: (layer, 0, 0)),
            pl.BlockSpec((1, tm, d), lambda bi, i: (bi, i, 0)),
            bvec(),
            pl.BlockSpec((1, 1, d), lambda bi, i: (layer, 0, 0)),
            bvec(), bvec(),
            pl.BlockSpec((1, N_EXPERTS, d), lambda bi, i: (layer, 0, 0)),
            pl.BlockSpec((1, N_EXPERTS, 1), lambda bi, i: (layer, 0, 0)),
        ],
        out_specs=[
            pl.BlockSpec((1, tm, d), lambda bi, i: (bi, i, 0)),
            pl.BlockSpec((tm * slab, LANES), lambda bi, i: (bi * nt + i, 0)),
            tok, tok, tok,
            pl.BlockSpec((N_EXPERTS, LANES), lambda bi, i: (0, 0)),
        ],
        out_shape=[
            jax.ShapeDtypeStruct((b, t, d), F32),
            jax.ShapeDtypeStruct((n * slab, LANES), U32),
            jax.ShapeDtypeStruct((TOP_K, n), I32),
            jax.ShapeDtypeStruct((TOP_K, n), I32),
            jax.ShapeDtypeStruct((TOP_K, n), F32),
            jax.ShapeDtypeStruct((N_EXPERTS, LANES), F32),
        ],
        scratch_shapes=[pltpu.VMEM((N_EXPERTS, LANES), F32)],
        compiler_params=_params("arbitrary", "arbitrary"),
        name="outproj_router",
    )(*ys, w_out_bf16, x, gate1, norm2_g, scale2, shift2, router_wt, router_b)
    return outs


def _scatter_kernel(zs_ref, pos_ref, h_ref, xs_hbm, zbuf, zsem, sem, *, tm, tm_e):
    zrows = zbuf.shape[0]

    @pl.when(pl.program_id(0) == 0)
    def _():
        zbuf[...] = jnp.zeros_like(zbuf)

        def zero(e, c):
            @pl.when(zs_ref[e] >= 0)
            def _():
                for q in range(tm_e // zrows):
                    pltpu.make_async_copy(zbuf, xs_hbm.at[pl.ds(zs_ref[e] + q * zrows, zrows)], zsem).start()
            return c

        lax.fori_loop(0, N_EXPERTS, zero, 0)

        def zero_done(e, c):
            @pl.when(zs_ref[e] >= 0)
            def _():
                for q in range(tm_e // zrows):
                    pltpu.make_async_copy(zbuf, xs_hbm.at[pl.ds(0, zrows)], zsem).wait()
            return c

        lax.fori_loop(0, N_EXPERTS, zero_done, 0)

    def issue(g, c):
        for u in range(ISSUE_UNROLL):
            tt = g * ISSUE_UNROLL + u
            for k in range(TOP_K):
                pltpu.make_async_copy(h_ref.at[tt], xs_hbm.at[pos_ref[k * tm + tt]], sem).start(priority=k % 2)
        return c

    lax.fori_loop(0, tm // ISSUE_UNROLL, issue, 0)
    for k in range(TOP_K):
        pltpu.make_async_copy(h_ref, xs_hbm.at[pl.ds(0, tm)], sem).wait()


def _block_major_positions(pos, tm):
    n = pos.shape[1]
    return pos.reshape(TOP_K, n // tm, tm).transpose(1, 0, 2).reshape(-1)


def _scatter_rows(zero_start, pos_flat, h_rows, m_pad, tm_e):
    n, slab, _ = h_rows.shape
    tm = ROW_COPY_TOKENS
    zrows = min(tm_e, 256)
    grid_spec = pltpu.PrefetchScalarGridSpec(
        num_scalar_prefetch=1,
        grid=(n // tm,),
        in_specs=[
            pl.BlockSpec((TOP_K * tm,), lambda i, zs: (i,), memory_space=pltpu.SMEM),
            pl.BlockSpec((tm, slab, LANES), lambda i, zs: (i, 0, 0)),
        ],
        out_specs=pl.BlockSpec(memory_space=pl.ANY),
        scratch_shapes=[pltpu.VMEM((zrows, slab, LANES), U32), pltpu.SemaphoreType.DMA(()),
                        pltpu.SemaphoreType.DMA(())],
    )
    return pl.pallas_call(
        functools.partial(_scatter_kernel, tm=tm, tm_e=tm_e),
        grid_spec=grid_spec,
        out_shape=jax.ShapeDtypeStruct((m_pad, slab, LANES), U32),
        compiler_params=_params("arbitrary"),
        name="scatter_rows",
    )(zero_start, pos_flat, h_rows)


def _deinterleave(hh):
    m, width = hh.shape
    lane = lax.broadcasted_iota(I32, (m, LANES), 1)
    low = lane < LANES // 2
    evens_then_odds = jnp.where(low, 2 * lane, 2 * lane - (LANES - 1))
    parts = [jnp.take_along_axis(hh[:, b * LANES:(b + 1) * LANES], evens_then_odds, axis=1)
             for b in range(width // LANES)]
    gates, lins = [], []
    for b in range(0, len(parts), 2):
        first, second = parts[b], parts[b + 1]
        gates.append(jnp.where(low, first, pltpu.roll(second, LANES // 2, 1)))
        lins.append(jnp.where(low, pltpu.roll(first, LANES // 2, 1), second))
    return jnp.concatenate(gates, axis=1), jnp.concatenate(lins, axis=1)


def _expert_kernel(te_ref, tv_ref, nu_ref, xs_ref, w1_ref, b1_ref, w2_ref, b2_ref, ys_ref,
                   x_scr, acc_scr, *, tm, sub, nc):
    i = pl.program_id(0)
    c = pl.program_id(1)
    d = x_scr.shape[1]
    n_slab = d // 2 // LANES
    live = i < nu_ref[0]

    def unpack_x():
        for s in range(n_slab):
            lo, hi = _unpack_rows(xs_ref[pl.ds(s, tm, stride=n_slab), :])
            x_scr[:, s * LANES:(s + 1) * LANES] = lo.astype(BF16)
            x_scr[:, d // 2 + s * LANES:d // 2 + (s + 1) * LANES] = hi.astype(BF16)

    def ffn_rows(n_rows):
        hh = jnp.dot(x_scr[0:n_rows, :], w1_ref[...].astype(BF16), preferred_element_type=F32) + b1_ref[...]
        g, lin = _deinterleave(hh)
        g = jnp.minimum(g, SWIGLU_LIMIT)
        lin = jnp.clip(lin, -SWIGLU_LIMIT, SWIGLU_LIMIT)
        act = g * _sigmoid(SWIGLU_ALPHA * g) * (lin + 1.0)
        return jnp.dot(act.astype(BF16), w2_ref[...].astype(BF16), preferred_element_type=F32)

    def pack_y(read_rows):
        for sb in range(tm // sub):
            _pack_rows(read_rows(sb * sub, (sb + 1) * sub), ys_ref, row0=sb * sub)

    @pl.when(live)
    def _():
        n_sub = tm // sub
        filled = (tv_ref[i] + sub - 1) // sub
        full = filled == n_sub
        first = c == 0
        last = c == nc - 1

        @pl.when(full & first)
        def _():
            unpack_x()
            acc_scr[...] = b2_ref[...] + ffn_rows(tm)

        @pl.when(full & last)
        def _():
            y = acc_scr[...] + ffn_rows(tm)
            pack_y(lambda r0, r1: y[r0:r1])

        @pl.when(full & jnp.logical_not(first | last))
        def _():
            acc_scr[...] += ffn_rows(tm)

        @pl.when(jnp.logical_not(full) & first)
        def _():
            unpack_x()
            acc_scr[...] = jnp.zeros_like(acc_scr) + b2_ref[...]

        for nb in range(1, n_sub):
            @pl.when(filled == nb)
            def _(nb=nb):
                acc_scr[0:nb * sub, :] += ffn_rows(nb * sub)

        @pl.when(jnp.logical_not(full) & last)
        def _():
            pack_y(lambda r0, r1: acc_scr[r0:r1, :])

    @pl.when(jnp.logical_not(live) & (c == 0))
    def _():
        ys_ref[...] = jnp.zeros_like(ys_ref)


def _experts(tile_expert, tile_valid, n_used, xs2d, w1, b1, w2, b2, layer, tm, sub, m_pad):
    _, n_exp, d, h2 = w1.shape
    hid = h2 // 2
    tc = min(hid // 2, 512)
    nc = hid // tc
    assert nc >= 2
    n_tiles = m_pad // tm
    slab = d // 2 // LANES

    def live(i, nu):
        return jnp.minimum(i, nu[0] - 1)

    def chunk(i, c, nu):
        return jnp.where(i < nu[0], c, nc - 1)

    grid_spec = pltpu.PrefetchScalarGridSpec(
        num_scalar_prefetch=3,
        grid=(n_tiles, nc),
        in_specs=[
            pl.BlockSpec((tm * slab, LANES), lambda i, c, te, tv, nu: (live(i, nu), 0)),
            pl.BlockSpec((None, None, d, 2 * tc), lambda i, c, te, tv, nu: (layer, te[i], 0, chunk(i, c, nu))),
            pl.BlockSpec((None, None, 1, 2 * tc), lambda i, c, te, tv, nu: (layer, te[i], 0, chunk(i, c, nu))),
            pl.BlockSpec((None, None, tc, d), lambda i, c, te, tv, nu: (layer, te[i], chunk(i, c, nu), 0)),
            pl.BlockSpec((None, None, 1, d), lambda i, c, te, tv, nu: (layer, te[i], 0, 0)),
        ],
        out_specs=pl.BlockSpec((tm * slab, LANES), lambda i, c, te, tv, nu: (i, 0)),
        scratch_shapes=[pltpu.VMEM((tm, d), BF16), pltpu.VMEM((tm, d), F32)],
    )
    return pl.pallas_call(
        functools.partial(_expert_kernel, tm=tm, sub=sub, nc=nc),
        grid_spec=grid_spec,
        out_shape=jax.ShapeDtypeStruct((m_pad * slab, LANES), U32),
        compiler_params=_params("arbitrary", "arbitrary"),
        name="experts",
    )(tile_expert, tile_valid, n_used, xs2d, w1, b1, w2, b2)


def _combine_kernel(pos_ref, pos_next_ref, wts_ref, x_ref, g2_ref, fg_ref, ys_hbm, ys_flat_hbm, o_ref, buf, sem,
                    *, tm, final, n_steps):
    i = pl.program_id(0)
    n_slab = x_ref.shape[1] // 2 // LANES
    slot_rows = TOP_K * tm * n_slab
    slot = i % 2
    base = pl.multiple_of(slot * slot_rows, slot_rows)

    def issue_block(p_ref, to_slot):
        to_base = to_slot * slot_rows

        def issue(g, c):
            for u in range(ISSUE_UNROLL):
                tt = g * ISSUE_UNROLL + u
                for k in range(TOP_K):
                    row0 = pl.multiple_of(to_base + (k * tm + tt) * n_slab, n_slab)
                    pltpu.make_async_copy(ys_hbm.at[p_ref[k * tm + tt]], buf.at[pl.ds(row0, n_slab)],
                                          sem.at[to_slot]).start(priority=k % 2)
            return c

        lax.fori_loop(0, tm // ISSUE_UNROLL, issue, 0)

    @pl.when(i == 0)
    def _():
        issue_block(pos_ref, 0)

    @pl.when(i + 1 < n_steps)
    def _():
        issue_block(pos_next_ref, 1 - slot)

    pltpu.make_async_copy(ys_flat_hbm.at[pl.ds(0, slot_rows)], buf.at[pl.ds(base, slot_rows)], sem.at[slot]).wait()

    w_sq = jnp.concatenate([wts_ref[...], jnp.zeros((tm - TOP_K, tm), F32)], axis=0)
    w_t = w_sq.T
    w_k = [jnp.broadcast_to(w_t[:, k:k + 1], (tm, LANES)) for k in range(TOP_K)]
    lows, highs = [], []
    for s in range(n_slab):
        acc_lo = jnp.zeros((tm, LANES), F32)
        acc_hi = jnp.zeros((tm, LANES), F32)
        for k in range(TOP_K):
            lo, hi = _unpack_rows(buf[pl.ds(base + k * tm * n_slab + s, tm, stride=n_slab), :])
            acc_lo = acc_lo + w_k[k] * lo
            acc_hi = acc_hi + w_k[k] * hi
        lows.append(acc_lo)
        highs.append(acc_hi)
    moe = jnp.concatenate(lows + highs, axis=1)
    x2 = x_ref[...] + g2_ref[0] * moe
    if final:
        x2 = x2 * lax.rsqrt(jnp.mean(x2 * x2, axis=-1, keepdims=True) + RMS_EPS) * fg_ref[...]
    o_ref[...] = x2


def _combine(pos_flat, wts, x1, gate2, final_g, ys_rows, tokens_per_batch, final):
    n, d = x1.shape
    tm = ROW_COPY_TOKENS
    steps_per_batch = tokens_per_batch // tm
    n_steps = n // tm
    m_pad, slab, _ = ys_rows.shape
    return pl.pallas_call(
        functools.partial(_combine_kernel, tm=tm, final=final, n_steps=n_steps),
        grid=(n_steps,),
        in_specs=[
            pl.BlockSpec((TOP_K * tm,), lambda i: (i,), memory_space=pltpu.SMEM),
            pl.BlockSpec((TOP_K * tm,), lambda i: (jnp.minimum(i + 1, n_steps - 1),), memory_space=pltpu.SMEM),
            pl.BlockSpec((TOP_K, tm), lambda i: (0, i)),
            pl.BlockSpec((tm, d), lambda i: (i, 0)),
            pl.BlockSpec((1, 1, d), lambda i: (i // steps_per_batch, 0, 0)),
            pl.BlockSpec((1, d), lambda i: (0, 0)),
            pl.BlockSpec(memory_space=pl.ANY),
            pl.BlockSpec(memory_space=pl.ANY),
        ],
        out_specs=pl.BlockSpec((tm, d), lambda i: (i, 0)),
        out_shape=jax.ShapeDtypeStruct((n, d), F32),
        scratch_shapes=[pltpu.VMEM((2 * TOP_K * tm * slab, LANES), U32), pltpu.SemaphoreType.DMA((2,))],
        compiler_params=_params("arbitrary"),
        name="combine",
    )(pos_flat, pos_flat, wts, x1, gate2, final_g, ys_rows, ys_rows.reshape(m_pad * slab, LANES))


def _routing_plan(idx, rank, counts, tm_e, n_tiles):
    experts = jnp.arange(N_EXPERTS, dtype=I32)
    cnt = counts[:, 0].astype(I32)
    tiles = (cnt + tm_e - 1) // tm_e
    tile_end = jnp.cumsum(tiles)
    tile_start = tile_end - tiles
    offsets = tile_start * tm_e
    pos = rank + jnp.sum(jnp.where(idx[..., None] == experts, offsets, 0), axis=-1)
    n_used = tile_end[-1]
    tile_ids = jnp.arange(n_tiles, dtype=I32)
    te = jnp.sum((tile_ids[:, None] >= tile_end[None, :]).astype(I32), axis=1)
    te_last = jnp.sum((n_used - 1 >= tile_end).astype(I32))
    te = jnp.where(tile_ids < n_used, te, te_last).astype(I32)
    mine = te[:, None] == experts[None, :]
    rows_left = jnp.sum(jnp.where(mine, cnt[None, :] - (tile_ids[:, None] - tile_start[None, :]) * tm_e, 0), axis=1)
    tile_valid = jnp.where(tile_ids < n_used, jnp.clip(rows_left, 0, tm_e), 0).astype(I32)
    zero_start = jnp.where(tiles > 0, (tile_end - 1) * tm_e, -1).astype(I32)
    return pos.astype(I32), te, tile_valid, n_used.reshape(1).astype(I32), zero_start


def kernel(x, c, positions, mod_w, mod_b, norm1_g, norm2_g, w_in, pool_w, pool_scale, diff_lq1, diff_lk1,
           diff_lq2, diff_lk2, diff_subln_g, conv_dw_w, conv_dw_b, conv_ln_g, conv_ln_b, conv_pw_w, conv_pw_b,
           w_out, router_w, router_b, exp_w1, exp_b1, exp_w2, exp_b2, final_g):
    b, t, d = x.shape
    depth = mod_w.shape[0]
    n = b * t
    tm_e = 1024
    sub_e = 128
    m_pad = n * TOP_K + N_EXPERTS * tm_e
    n_tiles = m_pad // tm_e
    slab = d // 2 // LANES

    def row3(a):
        return a.reshape(a.shape[0], 1, a.shape[1])

    mod = _modulation(c, mod_w, mod_b)
    tables = _rope_tables(positions)
    w_in_b = w_in.astype(BF16)
    w_out_b = w_out.astype(BF16)
    pool_w_b = pool_w.astype(BF16)
    pw_w_b = conv_pw_w.astype(BF16)
    router_wt = jnp.swapaxes(router_w, 1, 2)
    router_b3 = router_b.reshape(depth, N_EXPERTS, 1)
    b1r = exp_b1.reshape(depth, N_EXPERTS, 1, exp_b1.shape[-1])
    b2r = exp_b2.reshape(depth, N_EXPERTS, 1, d)
    final_g2 = final_g.reshape(1, d)

    for l in range(depth):
        lambda_init = 0.8 - 0.6 * math.exp(-0.3 * l)
        sh1, sc1, g1, sh2, sc2, g2 = [m.reshape(b, 1, d) for m in jnp.split(mod[l], 6, axis=-1)]
        z = _inproj(x, row3(norm1_g), sc1, sh1, w_in_b, tables, l)
        y_pool, y_conv = _poolconv(z, pool_w_b, row3(pool_scale), conv_dw_w, row3(conv_dw_b), row3(conv_ln_g),
                                   row3(conv_ln_b), pw_w_b, row3(conv_pw_b), l)
        y_diff = _diff_attention(z, row3(diff_lq1), row3(diff_lk1), row3(diff_lq2), row3(diff_lk2),
                                 row3(diff_subln_g), l, lambda_init)
        y_dil = _dilated_attention(z)
        x1, h2, idx, rank, wts, counts = _outproj_router(
            (y_pool, y_diff, y_dil, y_conv), w_out_b, x, g1, row3(norm2_g), sc2, sh2, router_wt, router_b3, l)
        pos, tile_expert, tile_valid, n_used, zero_start = _routing_plan(idx, rank, counts, tm_e, n_tiles)
        pos_flat = _block_major_positions(pos, ROW_COPY_TOKENS)
        xs = _scatter_rows(zero_start, pos_flat, h2.reshape(n, slab, LANES), m_pad, tm_e)
        ys = _experts(tile_expert, tile_valid, n_used, xs.reshape(m_pad * slab, LANES), exp_w1, b1r, exp_w2, b2r,
                      l, tm_e, sub_e, m_pad)
        x = _combine(pos_flat, wts, x1.reshape(n, d), g2, final_g2, ys.reshape(m_pad, slab, LANES), t,
                     final=(l == depth - 1)).reshape(b, t, d)
    return x
```

```python
import functools
import math

import numpy as np
import jax
import jax.numpy as jnp
from jax import lax
from jax.experimental import pallas as pl
from jax.experimental.pallas import tpu as pltpu

F32 = jnp.float32
BF16 = jnp.bfloat16
I32 = jnp.int32
U32 = jnp.uint32
HIGHEST = lax.Precision.HIGHEST

LANES = 128
SUBLANES = 8
VMEM_LIMIT_BYTES = 60 * 1024 * 1024

POOL_WINDOWS = (2, 4, 8, 16)
DILATED_PAIRS = ((128, 1), (512, 4), (2048, 16))
DIL_BAND = 128
CONV_WIDTH = 31
HALO = 32
ISSUE_UNROLL = 8
ROW_COPY_TOKENS = 512
N_EXPERTS = 32
TOP_K = 4
SWIGLU_ALPHA = 1.702
SWIGLU_LIMIT = 7.0
ROPE_THETA = 500000.0
RMS_EPS = 1e-6
LN_EPS = 1e-5
NEG_INF = -1e30
GROUP = 512
HEAD = 128
DIFF_QK = 64
LOG2E = math.log2(math.e)


def _params(*semantics):
    return pltpu.CompilerParams(dimension_semantics=semantics, vmem_limit_bytes=VMEM_LIMIT_BYTES)


def _sigmoid(x):
    return 1.0 / (1.0 + jnp.exp(-x))


def _mod_kernel(c_ref, w_ref, b_ref, o_ref):
    c = c_ref[...]
    ca = c * _sigmoid(c)
    w = w_ref[0]
    c_hi = ca.astype(BF16)
    c_lo = (ca - c_hi.astype(F32)).astype(BF16)
    w_hi = w.astype(BF16)
    w_lo = (w - w_hi.astype(F32)).astype(BF16)
    prod = jnp.dot(c_hi, w_hi, preferred_element_type=F32)
    prod = prod + (jnp.dot(c_lo, w_hi, preferred_element_type=F32) + jnp.dot(c_hi, w_lo, preferred_element_type=F32))
    o_ref[0] = prod + b_ref[0]


def _modulation(c, mod_w, mod_b):
    depth, d, n = mod_w.shape
    b = c.shape[0]
    rows = 8
    c_pad = jnp.zeros((rows, d), F32).at[:b].set(c)
    tn = 512
    out = pl.pallas_call(
        _mod_kernel,
        grid=(depth, n // tn),
        in_specs=[
            pl.BlockSpec((rows, d), lambda l, j: (0, 0)),
            pl.BlockSpec((1, d, tn), lambda l, j: (l, 0, j)),
            pl.BlockSpec((1, 1, tn), lambda l, j: (l, 0, j)),
        ],
        out_specs=pl.BlockSpec((1, rows, tn), lambda l, j: (l, 0, j)),
        out_shape=jax.ShapeDtypeStruct((depth, rows, n), F32),
        compiler_params=_params("arbitrary", "arbitrary"),
        name="modulation",
    )(c_pad, mod_w, mod_b.reshape(depth, 1, n))
    return out[:, :b]


def _rope_table_kernel(pos_ref, invd_ref, invc_ref, cd_ref, sd_ref, cc_ref, sc_ref):
    p = pos_ref[0]
    lane = lax.broadcasted_iota(I32, (1, LANES), 1)
    for inv_ref, c_ref, s_ref, hd in ((invd_ref, cd_ref, sd_ref, DIFF_QK), (invc_ref, cc_ref, sc_ref, HEAD)):
        half = hd // 8
        lm = lane % hd
        ang = p * inv_ref[...]
        c_ref[0] = jnp.cos(ang)
        s_ref[0] = jnp.where(lm < half, -jnp.sin(ang), jnp.sin(ang))


def _lane_inv_freq(hd):
    half = hd // 8
    inv = ROPE_THETA ** (-jnp.arange(half, dtype=F32) / half)
    lm = np.arange(LANES) % hd
    rotated = jnp.asarray(lm < 2 * half)
    return jnp.where(rotated, inv[lm % half], 0.0).reshape(1, LANES).astype(F32)


def _rope_tables(positions):
    b, t = positions.shape
    tm = min(t, 1024)
    pos = positions.astype(F32).reshape(b, t, 1)
    spec_t = pl.BlockSpec((1, tm, LANES), lambda bi, i: (bi, i, 0))
    spec_inv = pl.BlockSpec((1, LANES), lambda bi, i: (0, 0))
    shp = jax.ShapeDtypeStruct((b, t, LANES), F32)
    return pl.pallas_call(
        _rope_table_kernel,
        grid=(b, t // tm),
        in_specs=[pl.BlockSpec((1, tm, 1), lambda bi, i: (bi, i, 0)), spec_inv, spec_inv],
        out_specs=[spec_t] * 4,
        out_shape=[shp] * 4,
        compiler_params=_params("arbitrary", "arbitrary"),
        name="rope_tables",
    )(pos, _lane_inv_freq(DIFF_QK), _lane_inv_freq(HEAD))


def _rope_apply(z, cos, sin, hd):
    half = hd // 8
    lane = lax.broadcasted_iota(I32, (1, LANES), 1)
    first = (lane % hd) < half
    outs = []
    for cb in range(z.shape[1] // LANES):
        zc = z[:, cb * LANES:(cb + 1) * LANES]
        partner = jnp.where(first, pltpu.roll(zc, LANES - half, 1), pltpu.roll(zc, half, 1))
        outs.append(zc * cos + partner * sin)
    return jnp.concatenate(outs, axis=1)


def _inproj_kernel(x_ref, g_ref, sc_ref, sh_ref, w_ref, cd_ref, sd_ref, cc_ref, sc2_ref, z_ref, h_scr):
    j = pl.program_id(2)

    def project():
        return jnp.dot(h_scr[...], w_ref[...], preferred_element_type=F32)

    @pl.when(j == 0)
    def _():
        x = x_ref[0]
        y = x * lax.rsqrt(jnp.mean(x * x, axis=-1, keepdims=True) + RMS_EPS) * g_ref[0]
        h = (y * (1.0 + sc_ref[0]) + sh_ref[0]).astype(BF16)
        h_scr[...] = h
        z_ref[0] = jnp.dot(h, w_ref[...], preferred_element_type=F32).astype(z_ref.dtype)

    @pl.when((j == 1) | (j == 2))
    def _():
        r = _rope_apply(project(), cd_ref[0], sd_ref[0], DIFF_QK)
        r = r * jnp.where(j == 1, DIFF_QK ** -0.5 * LOG2E, 1.0)
        z_ref[0] = r.astype(z_ref.dtype)

    @pl.when((j == 4) | (j == 5))
    def _():
        r = _rope_apply(project(), cc_ref[0], sc2_ref[0], HEAD)
        r = r * jnp.where(j == 4, HEAD ** -0.5 * LOG2E, 1.0)
        z_ref[0] = r.astype(z_ref.dtype)

    @pl.when((j == 3) | (j >= 6))
    def _():
        z_ref[0] = project().astype(z_ref.dtype)


def _inproj(x, norm_g, scale, shift, w_in_bf16, tables, layer):
    b, t, d = x.shape
    n = w_in_bf16.shape[-1]
    tm = min(t, 1024)
    tn = GROUP
    cd, sd, cc, sc = tables
    spec_tab = pl.BlockSpec((1, tm, LANES), lambda bi, i, j: (bi, i, 0))
    spec_vec = pl.BlockSpec((1, 1, d), lambda bi, i, j: (bi, 0, 0))
    return pl.pallas_call(
        _inproj_kernel,
        grid=(b, t // tm, n // tn),
        in_specs=[
            pl.BlockSpec((1, tm, d), lambda bi, i, j: (bi, i, 0)),
            pl.BlockSpec((1, 1, d), lambda bi, i, j: (layer, 0, 0)),
            spec_vec, spec_vec,
            pl.BlockSpec((None, d, tn), lambda bi, i, j: (layer, 0, j)),
            spec_tab, spec_tab, spec_tab, spec_tab,
        ],
        out_specs=pl.BlockSpec((1, tm, tn), lambda bi, i, j: (bi, i, j)),
        out_shape=jax.ShapeDtypeStruct((b, t, n), BF16),
        scratch_shapes=[pltpu.VMEM((tm, d), BF16)],
        compiler_params=_params("arbitrary", "arbitrary", "arbitrary"),
        name="inproj",
    )(x, norm_g, scale, shift, w_in_bf16, cd, sd, cc, sc)


def _poolconv_kernel(zp_ref, zph_ref, za_ref, zah_ref, zg_ref, zgh_ref,
                     pw_ref, ps_ref, dww_ref, dwb_ref, lng_ref, lnb_ref, pww_ref, pwb_ref,
                     yp_ref, yc_ref, xp_scr, u_scr, ush_scr, *, tm):
    i = pl.program_id(1)
    keep = jnp.where(i == 0, 0.0, 1.0)

    xp_scr[0:HALO, :] = zph_ref[0].astype(F32) * keep
    xp_scr[HALO:, :] = zp_ref[0].astype(F32)
    t_glob = i * tm + lax.broadcasted_iota(I32, (tm, 1), 0)
    for g, w in enumerate(POOL_WINDOWS):
        cols = slice(g * LANES, (g + 1) * LANES)
        xg = xp_scr[HALO:HALO + tm, cols]
        acc = xg
        for k in range(1, w):
            acc = acc + xp_scr[HALO - k:HALO - k + tm, cols]
        cnt = jnp.minimum(t_glob + 1, w).astype(F32)
        pooled = acc / cnt - xg
        yg = jnp.dot(pooled.astype(BF16), pw_ref[0, g], preferred_element_type=F32)
        yp_ref[0, :, cols] = (yg * ps_ref[0, :, cols]).astype(yp_ref.dtype)

    ah = zah_ref[0].astype(F32)
    gh = zgh_ref[0].astype(F32)
    u_scr[0:HALO, :] = ah * _sigmoid(gh) * keep
    a = za_ref[0].astype(F32)
    gg = zg_ref[0].astype(F32)
    u_scr[HALO:, :] = a * _sigmoid(gg)
    span = tm + HALO - SUBLANES
    for sh in range(1, SUBLANES):
        ush_scr[sh - 1] = u_scr[sh:sh + span, :]
    acc = jnp.zeros((tm, GROUP), F32) + dwb_ref[0]
    base = HALO - (CONV_WIDTH - 1)
    for k in range(CONV_WIDTH):
        off = base + k
        sh, start = off % SUBLANES, off - off % SUBLANES
        window = u_scr[start:start + tm, :] if sh == 0 else ush_scr[sh - 1, start:start + tm, :]
        acc = acc + window * dww_ref[0, k:k + 1, :]
    mu = jnp.mean(acc, axis=-1, keepdims=True)
    cen = acc - mu
    var = jnp.mean(cen * cen, axis=-1, keepdims=True)
    v = cen * lax.rsqrt(var + LN_EPS) * lng_ref[0] + lnb_ref[0]
    v = v * _sigmoid(v)
    y = jnp.dot(v.astype(BF16), pww_ref[0], preferred_element_type=F32) + pwb_ref[0]
    yc_ref[0] = y.astype(yc_ref.dtype)


def _poolconv(z, pool_w_bf16, pool_scale, dw_w, dw_b, ln_g, ln_b, pw_w_bf16, pw_b, layer):
    b, t, _ = z.shape
    tm = min(t, 512)
    r = tm // HALO
    a_blk = 7
    g_blk = 8

    def cur(col):
        return pl.BlockSpec((1, tm, GROUP), lambda bi, i: (bi, i, col))

    def halo(col):
        return pl.BlockSpec((1, HALO, GROUP), lambda bi, i: (bi, jnp.maximum(i * r - 1, 0), col))

    def vec(n):
        return pl.BlockSpec((1, 1, n), lambda bi, i: (layer, 0, 0))

    out_spec = pl.BlockSpec((1, tm, GROUP), lambda bi, i: (bi, i, 0))
    shp = jax.ShapeDtypeStruct((b, t, GROUP), BF16)
    return pl.pallas_call(
        functools.partial(_poolconv_kernel, tm=tm),
        grid=(b, t // tm),
        in_specs=[
            cur(0), halo(0), cur(a_blk), halo(a_blk), cur(g_blk), halo(g_blk),
            pl.BlockSpec((1, len(POOL_WINDOWS), LANES, LANES), lambda bi, i: (layer, 0, 0, 0)),
            vec(GROUP),
            pl.BlockSpec((1, CONV_WIDTH, GROUP), lambda bi, i: (layer, 0, 0)),
            vec(GROUP), vec(GROUP), vec(GROUP),
            pl.BlockSpec((1, GROUP, GROUP), lambda bi, i: (layer, 0, 0)),
            vec(GROUP),
        ],
        out_specs=[out_spec, out_spec],
        out_shape=[shp, shp],
        scratch_shapes=[pltpu.VMEM((tm + HALO, GROUP), F32), pltpu.VMEM((tm + HALO, GROUP), F32),
                        pltpu.VMEM((SUBLANES - 1, tm + HALO - SUBLANES, GROUP), F32)],
        compiler_params=_params("arbitrary", "arbitrary"),
        name="pool_conv",
    )(z, z, z, z, z, z, pool_w_bf16, pool_scale, dw_w, dw_b, ln_g, ln_b, pw_w_bf16, pw_b)


def _diff_kernel(q_ref, k_ref, v_ref, lq1_ref, lk1_ref, lq2_ref, lk2_ref, g_ref, o_ref, *, tq, lambda_init):
    qi = pl.program_id(2)
    q = q_ref[0]
    lane = lax.broadcasted_iota(I32, (tq, HEAD), 1)
    zero = jnp.zeros_like(q)
    q2 = jnp.concatenate([jnp.where(lane < DIFF_QK, q, zero), jnp.where(lane >= DIFF_QK, q, zero)], axis=0)
    rows = 2 * tq

    def lane_tiles(s):
        return [s[:, c * LANES:(c + 1) * LANES] for c in range(tq // LANES)]

    def fold(j, carry, masked):
        m, l_run, acc = carry
        kb = k_ref[0, pl.ds(pl.multiple_of(j * tq, tq), tq), :]
        vb = v_ref[0, pl.ds(pl.multiple_of(j * tq, tq), tq), :]
        s = lax.dot_general(q2, kb, (((1,), (1,)), ((), ())), preferred_element_type=F32)
        if masked:
            row = lax.broadcasted_iota(I32, (rows, tq), 0) % tq
            col = lax.broadcasted_iota(I32, (rows, tq), 1)
            s = jnp.where(col <= row, s, NEG_INF)
        tiles = lane_tiles(s)
        m_blk = tiles[0]
        for part in tiles[1:]:
            m_blk = jnp.maximum(m_blk, part)
        m_new = jnp.maximum(m, jnp.max(m_blk, axis=-1, keepdims=True))
        alpha = jnp.exp2(m - m_new)
        p = jnp.exp2(s - m_new)
        l_run = alpha * l_run
        for part in lane_tiles(p):
            l_run = l_run + part
        acc = alpha * acc + jnp.dot(p.astype(BF16), vb, preferred_element_type=F32)
        return m_new, l_run, acc

    init = (jnp.full((rows, 1), NEG_INF, F32), jnp.zeros((rows, LANES), F32), jnp.zeros((rows, HEAD), F32))
    carry = lax.fori_loop(0, qi, lambda j, c: fold(j, c, False), init)
    _, l_run, acc = fold(qi, carry, True)
    l = jnp.sum(l_run, axis=-1, keepdims=True)

    lam =(jnp.exp(jnp.sum(lq1_ref[0] * lk1_ref[0], axis=-1, keepdims=True))
           - jnp.exp(jnp.sum(lq2_ref[0] * lk2_ref[0], axis=-1, keepdims=True)) + lambda_init)
    o = acc / l
    od = o[:tq] - lam * o[tq:]
    y = od * lax.rsqrt(jnp.mean(od * od, axis=-1, keepdims=True) + RMS_EPS) * g_ref[0]
    o_ref[0] = (y * (1.0 - lambda_init)).astype(o_ref.dtype)


def _diff_attention(z, lq1, lk1, lq2, lk2, subln_g, layer, lambda_init):
    b, t, _ = z.shape
    heads = GROUP // HEAD
    tq = min(t, 512)
    q0, k0, v0 = 1 * heads, 2 * heads, 3 * heads

    def vec(n):
        return pl.BlockSpec((1, 1, n), lambda bi, h, i: (layer, 0, 0))

    return pl.pallas_call(
        functools.partial(_diff_kernel, tq=tq, lambda_init=lambda_init),
        grid=(b, heads, t // tq),
        in_specs=[
            pl.BlockSpec((1, tq, HEAD), lambda bi, h, i: (bi, i, q0 + h)),
            pl.BlockSpec((1, t, HEAD), lambda bi, h, i: (bi, 0, k0 + h)),
            pl.BlockSpec((1, t, HEAD), lambda bi, h, i: (bi, 0, v0 + h)),
            vec(DIFF_QK), vec(DIFF_QK), vec(DIFF_QK), vec(DIFF_QK), vec(HEAD),
        ],
        out_specs=pl.BlockSpec((1, tq, HEAD), lambda bi, h, i: (bi, i, h)),
        out_shape=jax.ShapeDtypeStruct((b, t, GROUP), BF16),
        compiler_params=_params("arbitrary", "arbitrary", "arbitrary"),
        name="diff_attention",
    )(z, z, z, lq1, lk1, lq2, lk2, subln_g)


def _dilated_kernel(q_ref, k_ref, v_ref, o_ref, qf, kf, vf, o0, o1, o2, e0, e1, e2, *, t, group):
    qf[...] = q_ref[0].astype(F32)
    kf[...] = k_ref[0].astype(F32)
    vf[...] = v_ref[0].astype(F32)
    band = DIL_BAND
    a_idx = lax.broadcasted_iota(I32, (band, 2 * band), 0)
    b_idx = lax.broadcasted_iota(I32, (band, 2 * band), 1)
    in_band = (b_idx >= a_idx) & (b_idx <= a_idx + band)
    outs = ((o0, e0), (o1, e1), (o2, e2))

    for (window, dil), (o_scr, e_scr) in zip(DILATED_PAIRS, outs):
        nb = t // (dil * band)

        def one_block(it, dil=dil, nb=nb, o_scr=o_scr, e_scr=e_scr):
            r = it // nb
            n = it % nb
            start = r + n * (band * dil)
            prev = r + jnp.maximum(n - 1, 0) * (band * dil)

            def rows(ref, s0):
                return ref[pl.ds(s0, band, stride=dil), :] if dil > 1 else ref[pl.ds(s0, band), :]

            qb = rows(qf, start).astype(BF16)
            kk = jnp.concatenate([rows(kf, prev), rows(kf, start)], axis=0).astype(BF16)
            vv = jnp.concatenate([rows(vf, prev), rows(vf, start)], axis=0).astype(BF16)
            s = lax.dot_general(qb, kk, (((1,), (1,)), ((), ())), preferred_element_type=F32)
            valid = in_band & (b_idx >= jnp.where(n > 0, 0, band))
            s = jnp.where(valid, s, NEG_INF)
            m = jnp.max(s, axis=-1, keepdims=True)
            p = jnp.exp2(s - m)
            l = jnp.sum(p, axis=-1, keepdims=True)
            o = jnp.dot(p.astype(BF16), vv, preferred_element_type=F32) / l
            idx = pl.ds(start, band, stride=dil) if dil > 1 else pl.ds(start, band)
            o_scr[idx, :] = o
            e_scr[idx, :] = jnp.broadcast_to(m + jnp.log2(l), (band, HEAD))

        def blocks(g, carry, one_block=one_block):
            for u in range(group):
                one_block(g * group + u)
            return carry

        lax.fori_loop(0, dil * nb // group, blocks, 0)

    top = jnp.maximum(jnp.maximum(e0[...], e1[...]), e2[...])
    w0 = jnp.exp2(e0[...] - top)
    w1 = jnp.exp2(e1[...] - top)
    w2 = jnp.exp2(e2[...] - top)
    mix = (w0 * o0[...] + w1 * o1[...] + w2 * o2[...]) / (w0 + w1 + w2)
    o_ref[0] = mix.astype(o_ref.dtype)


def _dilated_attention(z):
    b, t, _ = z.shape
    heads = GROUP // HEAD
    q0, k0, v0 = 4 * heads, 5 * heads, 6 * heads
    group = 16
    assert all(w // dl == DIL_BAND and t % w == 0 for w, dl in DILATED_PAIRS)
    assert (t // DIL_BAND) % group == 0

    def col(c0):
        return pl.BlockSpec((1, t, HEAD), lambda bi, h: (bi, 0, c0 + h))

    return pl.pallas_call(
        functools.partial(_dilated_kernel, t=t, group=group),
        grid=(b, heads),
        in_specs=[col(q0), col(k0), col(v0)],
        out_specs=pl.BlockSpec((1, t, HEAD), lambda bi, h: (bi, 0, h)),
        out_shape=jax.ShapeDtypeStruct((b, t, GROUP), BF16),
        scratch_shapes=[pltpu.VMEM((t, HEAD), F32)] * 9,
        compiler_params=_params("arbitrary", "arbitrary"),
        name="dilated_attention",
    )(z, z, z)


def _pack_rows(h, out_ref, row0=0):
    m, half = h.shape[0], h.shape[1] // 2
    bits = pltpu.bitcast(h.astype(BF16).astype(F32), U32)
    word = (bits[:, half:] & jnp.uint32(0xFFFF0000)) | (bits[:, :half] >> 16)
    n_slab = half // LANES
    for s in range(n_slab):
        out_ref[pl.ds(row0 * n_slab + s, m, stride=n_slab), :] = word[:, s * LANES:(s + 1) * LANES]


def _unpack_rows(word):
    return pltpu.bitcast(word << 16, F32), pltpu.bitcast(word & jnp.uint32(0xFFFF0000), F32)


def _outproj_router_kernel(yp_ref, yd_ref, yc_ref, yv_ref, wo_ref, x_ref, g1_ref, n2_ref, sc_ref, sh_ref,
                           rw_ref, rb_ref,
                           x1_ref, h2_ref, idx_ref, rank_ref, wts_ref, cnt_ref, carry_scr, *, tm):
    first = (pl.program_id(0) == 0) & (pl.program_id(1) == 0)

    @pl.when(first)
    def _():
        carry_scr[...] = jnp.zeros_like(carry_scr)

    mix = jnp.dot(yp_ref[0], wo_ref[0 * GROUP:1 * GROUP, :], preferred_element_type=F32)
    mix = mix + jnp.dot(yd_ref[0], wo_ref[1 * GROUP:2 * GROUP, :], preferred_element_type=F32)
    mix = mix + jnp.dot(yc_ref[0], wo_ref[2 * GROUP:3 * GROUP, :], preferred_element_type=F32)
    mix = mix + jnp.dot(yv_ref[0], wo_ref[3 * GROUP:4 * GROUP, :], preferred_element_type=F32)
    x1 = x_ref[0] + g1_ref[0] * mix
    x1_ref[0] = x1

    y = x1 * lax.rsqrt(jnp.mean(x1 * x1, axis=-1, keepdims=True) + RMS_EPS) * n2_ref[0]
    h2 = y * (1.0 + sc_ref[0]) + sh_ref[0]
    _pack_rows(h2, h2_ref)

    logits = lax.dot_general(rw_ref[0], h2, (((1,), (1,)), ((), ())), precision=HIGHEST,
                             preferred_element_type=F32) + rb_ref[0]
    e_idx = lax.broadcasted_iota(I32, (N_EXPERTS, tm), 0)
    work = logits
    vals, sels, hots = [], [], []
    for _ in range(TOP_K):
        mx = jnp.max(work, axis=0, keepdims=True)
        sel = jnp.min(jnp.where(work == mx, e_idx, N_EXPERTS), axis=0, keepdims=True)
        hot = e_idx == sel
        vals.append(mx)
        sels.append(sel)
        hots.append(hot)
        work = jnp.where(hot, -jnp.inf, work)
    exps = [jnp.exp(v - vals[0]) for v in vals]
    denom = exps[0] + exps[1] + exps[2] + exps[3]

    chosen = jnp.zeros((N_EXPERTS, tm), F32)
    for hot in hots:
        chosen = chosen + hot.astype(F32)
    s_idx = lax.broadcasted_iota(I32, (tm, tm), 0)
    t_idx = lax.broadcasted_iota(I32, (tm, tm), 1)
    upper = (s_idx < t_idx).astype(BF16)
    before = jnp.dot(chosen.astype(BF16), upper, preferred_element_type=F32) + carry_scr[:, 0:1]
    for k in range(TOP_K):
        idx_ref[k:k + 1, :] = sels[k]
        rank_ref[k:k + 1, :] = jnp.sum(jnp.where(hots[k], before, 0.0), axis=0, keepdims=True).astype(I32)
        wts_ref[k:k + 1, :] = exps[k] / denom
    carry_scr[...] = carry_scr[...] + jnp.sum(chosen, axis=1, keepdims=True)
    cnt_ref[...] = carry_scr[...]


def _outproj_router(ys, w_out_bf16, x, gate1, norm2_g, scale2, shift2, router_wt, router_b, layer):
    b, t, d = x.shape
    tm = min(t, 512)
    n = b * t
    nt = t // tm
    slab = d // 2 // LANES

    def ytile():
        return pl.BlockSpec((1, tm, GROUP), lambda bi, i: (bi, i, 0))

    def bvec():
        return pl.BlockSpec((1, 1, d), lambda bi, i: (bi, 0, 0))

    tok = pl.BlockSpec((TOP_K, tm), lambda bi, i: (0, bi * nt + i))
    outs = pl.pallas_call(
        functools.partial(_outproj_router_kernel, tm=tm),
        grid=(b, nt),
        in_specs=[
            ytile(), ytile(), ytile(), ytile(),
            pl.BlockSpec((None, 4 * GROUP, d), lambda bi, i: (layer, 0, 0)),
            pl.BlockSpec((1, tm, d), lambda bi, i: (bi, i, 0)),
            bvec(),
            pl.BlockSpec((1, 1, d), lambda bi, i: (layer, 0, 0)),
            bvec(), bvec(),
            pl.BlockSpec((1, N_EXPERTS, d), lambda bi, i: (layer, 0, 0)),
            pl.BlockSpec((1, N_EXPERTS, 1), lambda bi, i: (layer, 0, 0)),
        ],
        out_specs=[
            pl.BlockSpec((1, tm, d), lambda bi, i: (bi, i, 0)),
            pl.BlockSpec((tm * slab, LANES), lambda bi, i: (bi * nt + i, 0)),
            tok, tok, tok,
            pl.BlockSpec((N_EXPERTS, LANES), lambda bi, i: (0, 0)),
        ],
        out_shape=[
            jax.ShapeDtypeStruct((b, t, d), F32),
            jax.ShapeDtypeStruct((n * slab, LANES), U32),
            jax.ShapeDtypeStruct((TOP_K, n), I32),
            jax.ShapeDtypeStruct((TOP_K, n), I32),
            jax.ShapeDtypeStruct((TOP_K, n), F32),
            jax.ShapeDtypeStruct((N_EXPERTS, LANES), F32),
        ],
        scratch_shapes=[pltpu.VMEM((N_EXPERTS, LANES), F32)],
        compiler_params=_params("arbitrary", "arbitrary"),
        name="outproj_router",
    )(*ys, w_out_bf16, x, gate1, norm2_g, scale2, shift2, router_wt, router_b)
    return outs


def _scatter_kernel(zs_ref, pos_ref, h_ref, xs_hbm, zbuf, zsem, sem, *, tm, tm_e):
    zrows = zbuf.shape[0]

    @pl.when(pl.program_id(0) == 0)
    def _():
        zbuf[...] = jnp.zeros_like(zbuf)

        def zero(e, c):
            @pl.when(zs_ref[e] >= 0)
            def _():
                for q in range(tm_e // zrows):
                    pltpu.make_async_copy(zbuf, xs_hbm.at[pl.ds(zs_ref[e] + q * zrows, zrows)], zsem).start()
            return c

        lax.fori_loop(0, N_EXPERTS, zero, 0)

        def zero_done(e, c):
            @pl.when(zs_ref[e] >= 0)
            def _():
                for q in range(tm_e // zrows):
                    pltpu.make_async_copy(zbuf, xs_hbm.at[pl.ds(0, zrows)], zsem).wait()
            return c

        lax.fori_loop(0, N_EXPERTS, zero_done, 0)

    def issue(g, c):
        for u in range(ISSUE_UNROLL):
            tt = g * ISSUE_UNROLL + u
            for k in range(TOP_K):
                pltpu.make_async_copy(h_ref.at[tt], xs_hbm.at[pos_ref[k * tm + tt]], sem).start(priority=k % 2)
        return c

    lax.fori_loop(0, tm // ISSUE_UNROLL, issue, 0)
    for k in range(TOP_K):
        pltpu.make_async_copy(h_ref, xs_hbm.at[pl.ds(0, tm)], sem).wait()


def _block_major_positions(pos, tm):
    n = pos.shape[1]
    return pos.reshape(TOP_K, n // tm, tm).transpose(1, 0, 2).reshape(-1)


def _scatter_rows(zero_start, pos_flat, h_rows, m_pad, tm_e):
    n, slab, _ = h_rows.shape
    tm = ROW_COPY_TOKENS
    zrows = min(tm_e, 256)
    grid_spec = pltpu.PrefetchScalarGridSpec(
        num_scalar_prefetch=1,
        grid=(n // tm,),
        in_specs=[
            pl.BlockSpec((TOP_K * tm,), lambda i, zs: (i,), memory_space=pltpu.SMEM),
            pl.BlockSpec((tm, slab, LANES), lambda i, zs: (i, 0, 0)),
        ],
        out_specs=pl.BlockSpec(memory_space=pl.ANY),
        scratch_shapes=[pltpu.VMEM((zrows, slab, LANES), U32), pltpu.SemaphoreType.DMA(()),
                        pltpu.SemaphoreType.DMA(())],
    )
    return pl.pallas_call(
        functools.partial(_scatter_kernel, tm=tm, tm_e=tm_e),
        grid_spec=grid_spec,
        out_shape=jax.ShapeDtypeStruct((m_pad, slab, LANES), U32),
        compiler_params=_params("arbitrary"),
        name="scatter_rows",
    )(zero_start, pos_flat, h_rows)


def _deinterleave(hh):
    m, width = hh.shape
    lane = lax.broadcasted_iota(I32, (m, LANES), 1)
    low = lane < LANES // 2
    evens_then_odds = jnp.where(low, 2 * lane, 2 * lane - (LANES - 1))
    parts = [jnp.take_along_axis(hh[:, b * LANES:(b + 1) * LANES], evens_then_odds, axis=1)
             for b in range(width // LANES)]
    gates, lins = [], []
    for b in range(0, len(parts), 2):
        first, second = parts[b], parts[b + 1]
        gates.append(jnp.where(low, first, pltpu.roll(second, LANES // 2, 1)))
        lins.append(jnp.where(low, pltpu.roll(first, LANES // 2, 1), second))
    return jnp.concatenate(gates, axis=1), jnp.concatenate(lins, axis=1)


def _expert_kernel(te_ref, tv_ref, nu_ref, xs_ref, w1_ref, b1_ref, w2_ref, b2_ref, ys_ref,
                   x_scr, acc_scr, *, tm, sub, nc):
    i = pl.program_id(0)
    c = pl.program_id(1)
    d = x_scr.shape[1]
    n_slab = d // 2 // LANES
    live = i < nu_ref[0]

    def unpack_x():
        for s in range(n_slab):
            lo, hi = _unpack_rows(xs_ref[pl.ds(s, tm, stride=n_slab), :])
            x_scr[:, s * LANES:(s + 1) * LANES] = lo.astype(BF16)
            x_scr[:, d // 2 + s * LANES:d // 2 + (s + 1) * LANES] = hi.astype(BF16)

    def ffn_rows(n_rows):
        hh = jnp.dot(x_scr[0:n_rows, :], w1_ref[...].astype(BF16), preferred_element_type=F32) + b1_ref[...]
        g, lin = _deinterleave(hh)
        g = jnp.minimum(g, SWIGLU_LIMIT)
        lin = jnp.clip(lin, -SWIGLU_LIMIT, SWIGLU_LIMIT)
        act = g * _sigmoid(SWIGLU_ALPHA * g) * (lin + 1.0)
        return jnp.dot(act.astype(BF16), w2_ref[...].astype(BF16), preferred_element_type=F32)

    def pack_y(read_rows):
        for sb in range(tm // sub):
            _pack_rows(read_rows(sb * sub, (sb + 1) * sub), ys_ref, row0=sb * sub)

    @pl.when(live)
    def _():
        n_sub = tm // sub
        filled = (tv_ref[i] + sub - 1) // sub
        full = filled == n_sub
        first = c == 0
        last = c == nc - 1

        @pl.when(full & first)
        def _():
            unpack_x()
            acc_scr[...] = b2_ref[...] + ffn_rows(tm)

        @pl.when(full & last)
        def _():
            y = acc_scr[...] + ffn_rows(tm)
            pack_y(lambda r0, r1: y[r0:r1])

        @pl.when(full & jnp.logical_not(first | last))
        def _():
            acc_scr[...] += ffn_rows(tm)

        @pl.when(jnp.logical_not(full) & first)
        def _():
            unpack_x()
            acc_scr[...] = jnp.zeros_like(acc_scr) + b2_ref[...]

        for nb in range(1, n_sub):
            @pl.when(filled == nb)
            def _(nb=nb):
                acc_scr[0:nb * sub, :] += ffn_rows(nb * sub)

        @pl.when(jnp.logical_not(full) & last)
        def _():
            pack_y(lambda r0, r1: acc_scr[r0:r1, :])

    @pl.when(jnp.logical_not(live) & (c == 0))
    def _():
        ys_ref[...] = jnp.zeros_like(ys_ref)


def _experts(tile_expert, tile_valid, n_used, xs2d, w1, b1, w2, b2, layer, tm, sub, m_pad):
    _, n_exp, d, h2 = w1.shape
    hid = h2 // 2
    tc = min(hid // 2, 512)
    nc = hid // tc
    assert nc >= 2
    n_tiles = m_pad // tm
    slab = d // 2 // LANES

    def live(i, nu):
        return jnp.minimum(i, nu[0] - 1)

    def chunk(i, c, nu):
        return jnp.where(i < nu[0], c, nc - 1)

    grid_spec = pltpu.PrefetchScalarGridSpec(
        num_scalar_prefetch=3,
        grid=(n_tiles, nc),
        in_specs=[
            pl.BlockSpec((tm * slab, LANES), lambda i, c, te, tv, nu: (live(i, nu), 0)),
            pl.BlockSpec((None, None, d, 2 * tc), lambda i, c, te, tv, nu: (layer, te[i], 0, chunk(i, c, nu))),
            pl.BlockSpec((None, None, 1, 2 * tc), lambda i, c, te, tv, nu: (layer, te[i], 0, chunk(i, c, nu))),
            pl.BlockSpec((None, None, tc, d), lambda i, c, te, tv, nu: (layer, te[i], chunk(i, c, nu), 0)),
            pl.BlockSpec((None, None, 1, d), lambda i, c, te, tv, nu: (layer, te[i], 0, 0)),
        ],
        out_specs=pl.BlockSpec((tm * slab, LANES), lambda i, c, te, tv, nu: (i, 0)),
        scratch_shapes=[pltpu.VMEM((tm, d), BF16), pltpu.VMEM((tm, d), F32)],
    )
    return pl.pallas_call(
        functools.partial(_expert_kernel, tm=tm, sub=sub, nc=nc),
        grid_spec=grid_spec,
        out_shape=jax.ShapeDtypeStruct((m_pad * slab, LANES), U32),
        compiler_params=_params("arbitrary", "arbitrary"),
        name="experts",
    )(tile_expert, tile_valid, n_used, xs2d, w1, b1, w2, b2)


def _combine_kernel(pos_ref, pos_next_ref, wts_ref, x_ref, g2_ref, fg_ref, ys_hbm, ys_flat_hbm, o_ref, buf, sem,
                    *, tm, final, n_steps):
    i = pl.program_id(0)
    n_slab = x_ref.shape[1] // 2 // LANES
    slot_rows = TOP_K * tm * n_slab
    slot = i % 2
    base = pl.multiple_of(slot * slot_rows, slot_rows)

    def issue_block(p_ref, to_slot):
        to_base = to_slot * slot_rows

        def issue(g, c):
            for u in range(ISSUE_UNROLL):
                tt = g * ISSUE_UNROLL + u
                for k in range(TOP_K):
                    row0 = pl.multiple_of(to_base + (k * tm + tt) * n_slab, n_slab)
                    pltpu.make_async_copy(ys_hbm.at[p_ref[k * tm + tt]], buf.at[pl.ds(row0, n_slab)],
                                          sem.at[to_slot]).start(priority=k % 2)
            return c

        lax.fori_loop(0, tm // ISSUE_UNROLL, issue, 0)

    @pl.when(i == 0)
    def _():
        issue_block(pos_ref, 0)

    @pl.when(i + 1 < n_steps)
    def _():
        issue_block(pos_next_ref, 1 - slot)

    pltpu.make_async_copy(ys_flat_hbm.at[pl.ds(0, slot_rows)], buf.at[pl.ds(base, slot_rows)], sem.at[slot]).wait()

    w_sq = jnp.concatenate([wts_ref[...], jnp.zeros((tm - TOP_K, tm), F32)], axis=0)
    w_t = w_sq.T
    w_k = [jnp.broadcast_to(w_t[:, k:k + 1], (tm, LANES)) for k in range(TOP_K)]
    lows, highs = [], []
    for s in range(n_slab):
        acc_lo = jnp.zeros((tm, LANES), F32)
        acc_hi = jnp.zeros((tm, LANES), F32)
        for k in range(TOP_K):
            lo, hi = _unpack_rows(buf[pl.ds(base + k * tm * n_slab + s, tm, stride=n_slab), :])
            acc_lo = acc_lo + w_k[k] * lo
            acc_hi = acc_hi + w_k[k] * hi
        lows.append(acc_lo)
        highs.append(acc_hi)
    moe = jnp.concatenate(lows + highs, axis=1)
    x2 = x_ref[...] + g2_ref[0] * moe
    if final:
        x2 = x2 * lax.rsqrt(jnp.mean(x2 * x2, axis=-1, keepdims=True) + RMS_EPS) * fg_ref[...]
    o_ref[...] = x2


def _combine(pos_flat, wts, x1, gate2, final_g, ys_rows, tokens_per_batch, final):
    n, d = x1.shape
    tm = ROW_COPY_TOKENS
    steps_per_batch = tokens_per_batch // tm
    n_steps = n // tm
    m_pad, slab, _ = ys_rows.shape
    return pl.pallas_call(
        functools.partial(_combine_kernel, tm=tm, final=final, n_steps=n_steps),
        grid=(n_steps,),
        in_specs=[
            pl.BlockSpec((TOP_K * tm,), lambda i: (i,), memory_space=pltpu.SMEM),
            pl.BlockSpec((TOP_K * tm,), lambda i: (jnp.minimum(i + 1, n_steps - 1),), memory_space=pltpu.SMEM),
            pl.BlockSpec((TOP_K, tm), lambda i: (0, i)),
            pl.BlockSpec((tm, d), lambda i: (i, 0)),
            pl.BlockSpec((1, 1, d), lambda i: (i // steps_per_batch, 0, 0)),
            pl.BlockSpec((1, d), lambda i: (0, 0)),
            pl.BlockSpec(memory_space=pl.ANY),
            pl.BlockSpec(memory_space=pl.ANY),
        ],
        out_specs=pl.BlockSpec((tm, d), lambda i: (i, 0)),
        out_shape=jax.ShapeDtypeStruct((n, d), F32),
        scratch_shapes=[pltpu.VMEM((2 * TOP_K * tm * slab, LANES), U32), pltpu.SemaphoreType.DMA((2,))],
        compiler_params=_params("arbitrary"),
        name="combine",
    )(pos_flat, pos_flat, wts, x1, gate2, final_g, ys_rows, ys_rows.reshape(m_pad * slab, LANES))


def _routing_plan(idx, rank, counts, tm_e, n_tiles):
    experts = jnp.arange(N_EXPERTS, dtype=I32)
    cnt = counts[:, 0].astype(I32)
    tiles = (cnt + tm_e - 1) // tm_e
    tile_end = jnp.cumsum(tiles)
    tile_start = tile_end - tiles
    offsets = tile_start * tm_e
    pos = rank + jnp.sum(jnp.where(idx[..., None] == experts, offsets, 0), axis=-1)
    n_used = tile_end[-1]
    tile_ids = jnp.arange(n_tiles, dtype=I32)
    te = jnp.sum((tile_ids[:, None] >= tile_end[None, :]).astype(I32), axis=1)
    te_last = jnp.sum((n_used - 1 >= tile_end).astype(I32))
    te = jnp.where(tile_ids < n_used, te, te_last).astype(I32)
    mine = te[:, None] == experts[None, :]
    rows_left = jnp.sum(jnp.where(mine, cnt[None, :] - (tile_ids[:, None] - tile_start[None, :]) * tm_e, 0), axis=1)
    tile_valid = jnp.where(tile_ids < n_used, jnp.clip(rows_left, 0, tm_e), 0).astype(I32)
    zero_start = jnp.where(tiles > 0, (tile_end - 1) * tm_e, -1).astype(I32)
    return pos.astype(I32), te, tile_valid, n_used.reshape(1).astype(I32), zero_start


def kernel(x, c, positions, mod_w, mod_b, norm1_g, norm2_g, w_in, pool_w, pool_scale, diff_lq1, diff_lk1,
           diff_lq2, diff_lk2, diff_subln_g, conv_dw_w, conv_dw_b, conv_ln_g, conv_ln_b, conv_pw_w, conv_pw_b,
           w_out, router_w, router_b, exp_w1, exp_b1, exp_w2, exp_b2, final_g):
    b, t, d = x.shape
    depth = mod_w.shape[0]
    n = b * t
    tm_e = 1024
    sub_e = 256
    m_pad = n * TOP_K + N_EXPERTS * tm_e
    n_tiles = m_pad // tm_e
    slab = d // 2 // LANES

    def row3(a):
        return a.reshape(a.shape[0], 1, a.shape[1])

    mod = _modulation(c, mod_w, mod_b)
    tables = _rope_tables(positions)
    w_in_b = w_in.astype(BF16)
    w_out_b = w_out.astype(BF16)
    pool_w_b = pool_w.astype(BF16)
    pw_w_b = conv_pw_w.astype(BF16)
    router_wt = jnp.swapaxes(router_w, 1, 2)
    router_b3 = router_b.reshape(depth, N_EXPERTS, 1)
    b1r = exp_b1.reshape(depth, N_EXPERTS, 1, exp_b1.shape[-1])
    b2r = exp_b2.reshape(depth, N_EXPERTS, 1, d)
    final_g2 = final_g.reshape(1, d)

    for l in range(depth):
        lambda_init = 0.8 - 0.6 * math.exp(-0.3 * l)
        sh1, sc1, g1, sh2, sc2, g2 = [m.reshape(b, 1, d) for m in jnp.split(mod[l], 6, axis=-1)]
        z = _inproj(x, row3(norm1_g), sc1, sh1, w_in_b, tables, l)
        y_pool, y_conv = _poolconv(z, pool_w_b, row3(pool_scale), conv_dw_w, row3(conv_dw_b), row3(conv_ln_g),
                                   row3(conv_ln_b), pw_w_b, row3(conv_pw_b), l)
        y_diff = _diff_attention(z, row3(diff_lq1), row3(diff_lk1), row3(diff_lq2), row3(diff_lk2),
                                 row3(diff_subln_g), l, lambda_init)
        y_dil = _dilated_attention(z)
        x1, h2, idx, rank, wts, counts = _outproj_router(
            (y_pool, y_diff, y_dil, y_conv), w_out_b, x, g1, row3(norm2_g), sc2, sh2, router_wt, router_b3, l)
        pos, tile_expert, tile_valid, n_used, zero_start = _routing_plan(idx, rank, counts, tm_e, n_tiles)
        pos_flat = _block_major_positions(pos, ROW_COPY_TOKENS)
        xs = _scatter_rows(zero_start, pos_flat, h2.reshape(n, slab, LANES), m_pad, tm_e)
        ys = _experts(tile_expert, tile_valid, n_used, xs.reshape(m_pad * slab, LANES), exp_w1, b1r, exp_w2, b2r,
                      l, tm_e, sub_e, m_pad)
        x = _combine(pos_flat, wts, x1.reshape(n, d), g2, final_g2, ys.reshape(m_pad, slab, LANES), t,
                     final=(l == depth - 1)).reshape(b, t, d)
    return x
```

```python
import functools
import math

import numpy as np
import jax
import jax.numpy as jnp
from jax import lax
from jax.experimental import pallas as pl
from jax.experimental.pallas import tpu as pltpu

F32 = jnp.float32
BF16 = jnp.bfloat16
I32 = jnp.int32
U32 = jnp.uint32
HIGHEST = lax.Precision.HIGHEST

LANES = 128
SUBLANES = 8
VMEM_LIMIT_BYTES = 60 * 1024 * 1024

POOL_WINDOWS = (2, 4, 8, 16)
DILATED_PAIRS = ((128, 1), (512, 4), (2048, 16))
DIL_BAND = 128
CONV_WIDTH = 31
HALO = 32
ISSUE_UNROLL = 8
SCATTER_TOKENS = 512
GATHER_TOKENS = 256
N_EXPERTS = 32
TOP_K = 4
SWIGLU_ALPHA = 1.702
SWIGLU_LIMIT = 7.0
ROPE_THETA = 500000.0
RMS_EPS = 1e-6
LN_EPS = 1e-5
NEG_INF = -1e30
GROUP = 512
HEAD = 128
DIFF_QK = 64
LOG2E = math.log2(math.e)


def _params(*semantics):
    return pltpu.CompilerParams(dimension_semantics=semantics, vmem_limit_bytes=VMEM_LIMIT_BYTES)


def _sigmoid(x):
    return 1.0 / (1.0 + jnp.exp(-x))


def _mod_kernel(c_ref, w_ref, b_ref, o_ref):
    c = c_ref[...]
    ca = c * _sigmoid(c)
    w = w_ref[0]
    c_hi = ca.astype(BF16)
    c_lo = (ca - c_hi.astype(F32)).astype(BF16)
    w_hi = w.astype(BF16)
    w_lo = (w - w_hi.astype(F32)).astype(BF16)
    prod = jnp.dot(c_hi, w_hi, preferred_element_type=F32)
    prod = prod + (jnp.dot(c_lo, w_hi, preferred_element_type=F32) + jnp.dot(c_hi, w_lo, preferred_element_type=F32))
    o_ref[0] = prod + b_ref[0]


def _modulation(c, mod_w, mod_b):
    depth, d, n = mod_w.shape
    b = c.shape[0]
    rows = 8
    c_pad = jnp.zeros((rows, d), F32).at[:b].set(c)
    tn = 512
    out = pl.pallas_call(
        _mod_kernel,
        grid=(depth, n // tn),
        in_specs=[
            pl.BlockSpec((rows, d), lambda l, j: (0, 0)),
            pl.BlockSpec((1, d, tn), lambda l, j: (l, 0, j)),
            pl.BlockSpec((1, 1, tn), lambda l, j: (l, 0, j)),
        ],
        out_specs=pl.BlockSpec((1, rows, tn), lambda l, j: (l, 0, j)),
        out_shape=jax.ShapeDtypeStruct((depth, rows, n), F32),
        compiler_params=_params("arbitrary", "arbitrary"),
        name="modulation",
    )(c_pad, mod_w, mod_b.reshape(depth, 1, n))
    return out[:, :b]


def _rope_table_kernel(pos_ref, invd_ref, invc_ref, cd_ref, sd_ref, cc_ref, sc_ref):
    p = pos_ref[0]
    lane = lax.broadcasted_iota(I32, (1, LANES), 1)
    for inv_ref, c_ref, s_ref, hd in ((invd_ref, cd_ref, sd_ref, DIFF_QK), (invc_ref, cc_ref, sc_ref, HEAD)):
        half = hd // 8
        lm = lane % hd
        ang = p * inv_ref[...]
        c_ref[0] = jnp.cos(ang)
        s_ref[0] = jnp.where(lm < half, -jnp.sin(ang), jnp.sin(ang))


def _lane_inv_freq(hd):
    half = hd // 8
    inv = ROPE_THETA ** (-jnp.arange(half, dtype=F32) / half)
    lm = np.arange(LANES) % hd
    rotated = jnp.asarray(lm < 2 * half)
    return jnp.where(rotated, inv[lm % half], 0.0).reshape(1, LANES).astype(F32)


def _rope_tables(positions):
    b, t = positions.shape
    tm = min(t, 1024)
    pos = positions.astype(F32).reshape(b, t, 1)
    spec_t = pl.BlockSpec((1, tm, LANES), lambda bi, i: (bi, i, 0))
    spec_inv = pl.BlockSpec((1, LANES), lambda bi, i: (0, 0))
    shp = jax.ShapeDtypeStruct((b, t, LANES), F32)
    return pl.pallas_call(
        _rope_table_kernel,
        grid=(b, t // tm),
        in_specs=[pl.BlockSpec((1, tm, 1), lambda bi, i: (bi, i, 0)), spec_inv, spec_inv],
        out_specs=[spec_t] * 4,
        out_shape=[shp] * 4,
        compiler_params=_params("arbitrary", "arbitrary"),
        name="rope_tables",
    )(pos, _lane_inv_freq(DIFF_QK), _lane_inv_freq(HEAD))


def _rope_apply(z, cos, sin, hd):
    half = hd // 8
    lane = lax.broadcasted_iota(I32, (1, LANES), 1)
    first = (lane % hd) < half
    outs = []
    for cb in range(z.shape[1] // LANES):
        zc = z[:, cb * LANES:(cb + 1) * LANES]
        partner = jnp.where(first, pltpu.roll(zc, LANES - half, 1), pltpu.roll(zc, half, 1))
        outs.append(zc * cos + partner * sin)
    return jnp.concatenate(outs, axis=1)


def _inproj_kernel(x_ref, g_ref, sc_ref, sh_ref, w_ref, cd_ref, sd_ref, cc_ref, sc2_ref, z_ref, h_scr):
    j = pl.program_id(2)

    def project():
        return jnp.dot(h_scr[...], w_ref[...], preferred_element_type=F32)

    @pl.when(j == 0)
    def _():
        x = x_ref[0]
        y = x * lax.rsqrt(jnp.mean(x * x, axis=-1, keepdims=True) + RMS_EPS) * g_ref[0]
        h = (y * (1.0 + sc_ref[0]) + sh_ref[0]).astype(BF16)
        h_scr[...] = h
        z_ref[0] = jnp.dot(h, w_ref[...], preferred_element_type=F32).astype(z_ref.dtype)

    @pl.when((j == 1) | (j == 2))
    def _():
        r = _rope_apply(project(), cd_ref[0], sd_ref[0], DIFF_QK)
        r = r * jnp.where(j == 1, DIFF_QK ** -0.5 * LOG2E, 1.0)
        z_ref[0] = r.astype(z_ref.dtype)

    @pl.when((j == 4) | (j == 5))
    def _():
        r = _rope_apply(project(), cc_ref[0], sc2_ref[0], HEAD)
        r = r * jnp.where(j == 4, HEAD ** -0.5 * LOG2E, 1.0)
        z_ref[0] = r.astype(z_ref.dtype)

    @pl.when((j == 3) | (j >= 6))
    def _():
        z_ref[0] = project().astype(z_ref.dtype)


def _inproj(x, norm_g, scale, shift, w_in_bf16, tables, layer):
    b, t, d = x.shape
    n = w_in_bf16.shape[-1]
    tm = min(t, 1024)
    tn = GROUP
    cd, sd, cc, sc = tables
    spec_tab = pl.BlockSpec((1, tm, LANES), lambda bi, i, j: (bi, i, 0))
    spec_vec = pl.BlockSpec((1, 1, d), lambda bi, i, j: (bi, 0, 0))
    return pl.pallas_call(
        _inproj_kernel,
        grid=(b, t // tm, n // tn),
        in_specs=[
            pl.BlockSpec((1, tm, d), lambda bi, i, j: (bi, i, 0)),
            pl.BlockSpec((1, 1, d), lambda bi, i, j: (layer, 0, 0)),
            spec_vec, spec_vec,
            pl.BlockSpec((None, d, tn), lambda bi, i, j: (layer, 0, j)),
            spec_tab, spec_tab, spec_tab, spec_tab,
        ],
        out_specs=pl.BlockSpec((1, tm, tn), lambda bi, i, j: (bi, i, j)),
        out_shape=jax.ShapeDtypeStruct((b, t, n), BF16),
        scratch_shapes=[pltpu.VMEM((tm, d), BF16)],
        compiler_params=_params("arbitrary", "arbitrary", "arbitrary"),
        name="inproj",
    )(x, norm_g, scale, shift, w_in_bf16, cd, sd, cc, sc)


def _poolconv_kernel(zp_ref, zph_ref, za_ref, zah_ref, zg_ref, zgh_ref,
                     pw_ref, ps_ref, dww_ref, dwb_ref, lng_ref, lnb_ref, pww_ref, pwb_ref,
                     yp_ref, yc_ref, xp_scr, u_scr, ush_scr, *, tm):
    i = pl.program_id(1)
    keep = jnp.where(i == 0, 0.0, 1.0)

    xp_scr[0:HALO, :] = zph_ref[0].astype(F32) * keep
    xp_scr[HALO:, :] = zp_ref[0].astype(F32)
    t_glob = i * tm + lax.broadcasted_iota(I32, (tm, 1), 0)
    for g, w in enumerate(POOL_WINDOWS):
        cols = slice(g * LANES, (g + 1) * LANES)
        xg = xp_scr[HALO:HALO + tm, cols]
        acc = xg
        for k in range(1, w):
            acc = acc + xp_scr[HALO - k:HALO - k + tm, cols]
        cnt = jnp.minimum(t_glob + 1, w).astype(F32)
        pooled = acc / cnt - xg
        yg = jnp.dot(pooled.astype(BF16), pw_ref[0, g], preferred_element_type=F32)
        yp_ref[0, :, cols] = (yg * ps_ref[0, :, cols]).astype(yp_ref.dtype)

    ah = zah_ref[0].astype(F32)
    gh = zgh_ref[0].astype(F32)
    u_scr[0:HALO, :] = ah * _sigmoid(gh) * keep
    a = za_ref[0].astype(F32)
    gg = zg_ref[0].astype(F32)
    u_scr[HALO:, :] = a * _sigmoid(gg)
    span = tm + HALO - SUBLANES
    for sh in range(1, SUBLANES):
        ush_scr[sh - 1] = u_scr[sh:sh + span, :]
    acc = jnp.zeros((tm, GROUP), F32) + dwb_ref[0]
    base = HALO - (CONV_WIDTH - 1)
    for k in range(CONV_WIDTH):
        off = base + k
        sh, start = off % SUBLANES, off - off % SUBLANES
        window = u_scr[start:start + tm, :] if sh == 0 else ush_scr[sh - 1, start:start + tm, :]
        acc = acc + window * dww_ref[0, k:k + 1, :]
    mu = jnp.mean(acc, axis=-1, keepdims=True)
    cen = acc - mu
    var = jnp.mean(cen * cen, axis=-1, keepdims=True)
    v = cen * lax.rsqrt(var + LN_EPS) * lng_ref[0] + lnb_ref[0]
    v = v * _sigmoid(v)
    y = jnp.dot(v.astype(BF16), pww_ref[0], preferred_element_type=F32) + pwb_ref[0]
    yc_ref[0] = y.astype(yc_ref.dtype)


def _poolconv(z, pool_w_bf16, pool_scale, dw_w, dw_b, ln_g, ln_b, pw_w_bf16, pw_b, layer):
    b, t, _ = z.shape
    tm = min(t, 512)
    r = tm // HALO
    a_blk = 7
    g_blk = 8

    def cur(col):
        return pl.BlockSpec((1, tm, GROUP), lambda bi, i: (bi, i, col))

    def halo(col):
        return pl.BlockSpec((1, HALO, GROUP), lambda bi, i: (bi, jnp.maximum(i * r - 1, 0), col))

    def vec(n):
        return pl.BlockSpec((1, 1, n), lambda bi, i: (layer, 0, 0))

    out_spec = pl.BlockSpec((1, tm, GROUP), lambda bi, i: (bi, i, 0))
    shp = jax.ShapeDtypeStruct((b, t, GROUP), BF16)
    return pl.pallas_call(
        functools.partial(_poolconv_kernel, tm=tm),
        grid=(b, t // tm),
        in_specs=[
            cur(0), halo(0), cur(a_blk), halo(a_blk), cur(g_blk), halo(g_blk),
            pl.BlockSpec((1, len(POOL_WINDOWS), LANES, LANES), lambda bi, i: (layer, 0, 0, 0)),
            vec(GROUP),
            pl.BlockSpec((1, CONV_WIDTH, GROUP), lambda bi, i: (layer, 0, 0)),
            vec(GROUP), vec(GROUP), vec(GROUP),
            pl.BlockSpec((1, GROUP, GROUP), lambda bi, i: (layer, 0, 0)),
            vec(GROUP),
        ],
        out_specs=[out_spec, out_spec],
        out_shape=[shp, shp],
        scratch_shapes=[pltpu.VMEM((tm + HALO, GROUP), F32), pltpu.VMEM((tm + HALO, GROUP), F32),
                        pltpu.VMEM((SUBLANES - 1, tm + HALO - SUBLANES, GROUP), F32)],
        compiler_params=_params("arbitrary", "arbitrary"),
        name="pool_conv",
    )(z, z, z, z, z, z, pool_w_bf16, pool_scale, dw_w, dw_b, ln_g, ln_b, pw_w_bf16, pw_b)


def _diff_kernel(q_ref, k_ref, v_ref, lq1_ref, lk1_ref, lq2_ref, lk2_ref, g_ref, o_ref, *, tq, lambda_init):
    qi = pl.program_id(2)
    q = q_ref[0]
    lane = lax.broadcasted_iota(I32, (tq, HEAD), 1)
    zero = jnp.zeros_like(q)
    q2 = jnp.concatenate([jnp.where(lane < DIFF_QK, q, zero), jnp.where(lane >= DIFF_QK, q, zero)], axis=0)
    rows = 2 * tq

    def lane_tiles(s):
        return [s[:, c * LANES:(c + 1) * LANES] for c in range(tq // LANES)]

    def fold(j, carry, masked):
        m, l_run, acc = carry
        kb = k_ref[0, pl.ds(pl.multiple_of(j * tq, tq), tq), :]
        vb = v_ref[0, pl.ds(pl.multiple_of(j * tq, tq), tq), :]
        s = lax.dot_general(q2, kb, (((1,), (1,)), ((), ())), preferred_element_type=F32)
        if masked:
            row = lax.broadcasted_iota(I32, (rows, tq), 0) % tq
            col = lax.broadcasted_iota(I32, (rows, tq), 1)
            s = jnp.where(col <= row, s, NEG_INF)
        tiles = lane_tiles(s)
        m_blk = tiles[0]
        for part in tiles[1:]:
            m_blk = jnp.maximum(m_blk, part)
        m_new = jnp.maximum(m, jnp.max(m_blk, axis=-1, keepdims=True))
        alpha = jnp.exp2(m - m_new)
        p = jnp.exp2(s - m_new)
        l_run = alpha * l_run
        for part in lane_tiles(p):
            l_run = l_run + part
        acc = alpha * acc + jnp.dot(p.astype(BF16), vb, preferred_element_type=F32)
        return m_new, l_run, acc

    init = (jnp.full((rows, 1), NEG_INF, F32), jnp.zeros((rows, LANES), F32), jnp.zeros((rows, HEAD), F32))
    carry = lax.fori_loop(0, qi, lambda j, c: fold(j, c, False), init)
    _, l_run, acc = fold(qi, carry, True)
    l = jnp.sum(l_run, axis=-1, keepdims=True)

    lam =(jnp.exp(jnp.sum(lq1_ref[0] * lk1_ref[0], axis=-1, keepdims=True))
           - jnp.exp(jnp.sum(lq2_ref[0] * lk2_ref[0], axis=-1, keepdims=True)) + lambda_init)
    o = acc / l
    od = o[:tq] - lam * o[tq:]
    y = od * lax.rsqrt(jnp.mean(od * od, axis=-1, keepdims=True) + RMS_EPS) * g_ref[0]
    o_ref[0] = (y * (1.0 - lambda_init)).astype(o_ref.dtype)


def _diff_attention(z, lq1, lk1, lq2, lk2, subln_g, layer, lambda_init):
    b, t, _ = z.shape
    heads = GROUP // HEAD
    tq = min(t, 512)
    q0, k0, v0 = 1 * heads, 2 * heads, 3 * heads

    def vec(n):
        return pl.BlockSpec((1, 1, n), lambda bi, h, i: (layer, 0, 0))

    return pl.pallas_call(
        functools.partial(_diff_kernel, tq=tq, lambda_init=lambda_init),
        grid=(b, heads, t // tq),
        in_specs=[
            pl.BlockSpec((1, tq, HEAD), lambda bi, h, i: (bi, i, q0 + h)),
            pl.BlockSpec((1, t, HEAD), lambda bi, h, i: (bi, 0, k0 + h)),
            pl.BlockSpec((1, t, HEAD), lambda bi, h, i: (bi, 0, v0 + h)),
            vec(DIFF_QK), vec(DIFF_QK), vec(DIFF_QK), vec(DIFF_QK), vec(HEAD),
        ],
        out_specs=pl.BlockSpec((1, tq, HEAD), lambda bi, h, i: (bi, i, h)),
        out_shape=jax.ShapeDtypeStruct((b, t, GROUP), BF16),
        compiler_params=_params("arbitrary", "arbitrary", "arbitrary"),
        name="diff_attention",
    )(z, z, z, lq1, lk1, lq2, lk2, subln_g)


def _dilated_kernel(q_ref, k_ref, v_ref, o_ref, qf, kf, vf, o0, o1, o2, e0, e1, e2, *, t, group):
    qf[...] = q_ref[0].astype(F32)
    kf[...] = k_ref[0].astype(F32)
    vf[...] = v_ref[0].astype(F32)
    band = DIL_BAND
    a_idx = lax.broadcasted_iota(I32, (band, 2 * band), 0)
    b_idx = lax.broadcasted_iota(I32, (band, 2 * band), 1)
    in_band = (b_idx >= a_idx) & (b_idx <= a_idx + band)
    outs = ((o0, e0), (o1, e1), (o2, e2))

    for (window, dil), (o_scr, e_scr) in zip(DILATED_PAIRS, outs):
        nb = t // (dil * band)

        def one_block(it, dil=dil, nb=nb, o_scr=o_scr, e_scr=e_scr):
            r = it // nb
            n = it % nb
            start = r + n * (band * dil)
            prev = r + jnp.maximum(n - 1, 0) * (band * dil)

            def rows(ref, s0):
                return ref[pl.ds(s0, band, stride=dil), :] if dil > 1 else ref[pl.ds(s0, band), :]

            qb = rows(qf, start).astype(BF16)
            kk = jnp.concatenate([rows(kf, prev), rows(kf, start)], axis=0).astype(BF16)
            vv = jnp.concatenate([rows(vf, prev), rows(vf, start)], axis=0).astype(BF16)
            s = lax.dot_general(qb, kk, (((1,), (1,)), ((), ())), preferred_element_type=F32)
            valid = in_band & (b_idx >= jnp.where(n > 0, 0, band))
            s = jnp.where(valid, s, NEG_INF)
            m = jnp.max(s, axis=-1, keepdims=True)
            p = jnp.exp2(s - m)
            l = jnp.sum(p, axis=-1, keepdims=True)
            o = jnp.dot(p.astype(BF16), vv, preferred_element_type=F32) / l
            idx = pl.ds(start, band, stride=dil) if dil > 1 else pl.ds(start, band)
            o_scr[idx, :] = o
            e_scr[idx, :] = jnp.broadcast_to(m + jnp.log2(l), (band, HEAD))

        def blocks(g, carry, one_block=one_block):
            for u in range(group):
                one_block(g * group + u)
            return carry

        lax.fori_loop(0, dil * nb // group, blocks, 0)

    top = jnp.maximum(jnp.maximum(e0[...], e1[...]), e2[...])
    w0 = jnp.exp2(e0[...] - top)
    w1 = jnp.exp2(e1[...] - top)
    w2 = jnp.exp2(e2[...] - top)
    mix = (w0 * o0[...] + w1 * o1[...] + w2 * o2[...]) / (w0 + w1 + w2)
    o_ref[0] = mix.astype(o_ref.dtype)


def _dilated_attention(z):
    b, t, _ = z.shape
    heads = GROUP // HEAD
    q0, k0, v0 = 4 * heads, 5 * heads, 6 * heads
    group = 16
    assert all(w // dl == DIL_BAND and t % w == 0 for w, dl in DILATED_PAIRS)
    assert (t // DIL_BAND) % group == 0

    def col(c0):
        return pl.BlockSpec((1, t, HEAD), lambda bi, h: (bi, 0, c0 + h))

    return pl.pallas_call(
        functools.partial(_dilated_kernel, t=t, group=group),
        grid=(b, heads),
        in_specs=[col(q0), col(k0), col(v0)],
        out_specs=pl.BlockSpec((1, t, HEAD), lambda bi, h: (bi, 0, h)),
        out_shape=jax.ShapeDtypeStruct((b, t, GROUP), BF16),
        scratch_shapes=[pltpu.VMEM((t, HEAD), F32)] * 9,
        compiler_params=_params("arbitrary", "arbitrary"),
        name="dilated_attention",
    )(z, z, z)


def _pack_rows(h, out_ref, row0=0):
    m, half = h.shape[0], h.shape[1] // 2
    bits = pltpu.bitcast(h.astype(BF16).astype(F32), U32)
    word = (bits[:, half:] & jnp.uint32(0xFFFF0000)) | (bits[:, :half] >> 16)
    n_slab = half // LANES
    for s in range(n_slab):
        out_ref[pl.ds(row0 * n_slab + s, m, stride=n_slab), :] = word[:, s * LANES:(s + 1) * LANES]


def _unpack_rows(word):
    return pltpu.bitcast(word << 16, F32), pltpu.bitcast(word & jnp.uint32(0xFFFF0000), F32)


def _outproj_router_kernel(yp_ref, yd_ref, yc_ref, yv_ref, wo_ref, x_ref, g1_ref, n2_ref, sc_ref, sh_ref,
                           rw_ref, rb_ref,
                           x1_ref, h2_ref, idx_ref, rank_ref, wts_ref, cnt_ref, carry_scr, *, tm):
    first = (pl.program_id(0) == 0) & (pl.program_id(1) == 0)

    @pl.when(first)
    def _():
        carry_scr[...] = jnp.zeros_like(carry_scr)

    mix = jnp.dot(yp_ref[0], wo_ref[0 * GROUP:1 * GROUP, :], preferred_element_type=F32)
    mix = mix + jnp.dot(yd_ref[0], wo_ref[1 * GROUP:2 * GROUP, :], preferred_element_type=F32)
    mix = mix + jnp.dot(yc_ref[0], wo_ref[2 * GROUP:3 * GROUP, :], preferred_element_type=F32)
    mix = mix + jnp.dot(yv_ref[0], wo_ref[3 * GROUP:4 * GROUP, :], preferred_element_type=F32)
    x1 = x_ref[0] + g1_ref[0] * mix
    x1_ref[0] = x1

    y = x1 * lax.rsqrt(jnp.mean(x1 * x1, axis=-1, keepdims=True) + RMS_EPS) * n2_ref[0]
    h2 = y * (1.0 + sc_ref[0]) + sh_ref[0]
    _pack_rows(h2, h2_ref)

    logits = lax.dot_general(rw_ref[0], h2, (((1,), (1,)), ((), ())), precision=HIGHEST,
                             preferred_element_type=F32) + rb_ref[0]
    e_idx = lax.broadcasted_iota(I32, (N_EXPERTS, tm), 0)
    work = logits
    vals, sels, hots = [], [], []
    for _ in range(TOP_K):
        mx = jnp.max(work, axis=0, keepdims=True)
        sel = jnp.min(jnp.where(work == mx, e_idx, N_EXPERTS), axis=0, keepdims=True)
        hot = e_idx == sel
        vals.append(mx)
        sels.append(sel)
        hots.append(hot)
        work = jnp.where(hot, -jnp.inf, work)
    exps = [jnp.exp(v - vals[0]) for v in vals]
    denom = exps[0] + exps[1] + exps[2] + exps[3]

    chosen = jnp.zeros((N_EXPERTS, tm), F32)
    for hot in hots:
        chosen = chosen + hot.astype(F32)
    s_idx = lax.broadcasted_iota(I32, (tm, tm), 0)
    t_idx = lax.broadcasted_iota(I32, (tm, tm), 1)
    upper = (s_idx < t_idx).astype(BF16)
    before = jnp.dot(chosen.astype(BF16), upper, preferred_element_type=F32) + carry_scr[:, 0:1]
    for k in range(TOP_K):
        idx_ref[k:k + 1, :] = sels[k]
        rank_ref[k:k + 1, :] = jnp.sum(jnp.where(hots[k], before, 0.0), axis=0, keepdims=True).astype(I32)
        wts_ref[k:k + 1, :] = exps[k] / denom
    carry_scr[...] = carry_scr[...] + jnp.sum(chosen, axis=1, keepdims=True)
    cnt_ref[...] = carry_scr[...]


def _outproj_router(ys, w_out_bf16, x, gate1, norm2_g, scale2, shift2, router_wt, router_b, layer):
    b, t, d = x.shape
    tm = min(t, 512)
    n = b * t
    nt = t // tm
    slab = d // 2 // LANES

    def ytile():
        return pl.BlockSpec((1, tm, GROUP), lambda bi, i: (bi, i, 0))

    def bvec():
        return pl.BlockSpec((1, 1, d), lambda bi, i: (bi, 0, 0))

    tok = pl.BlockSpec((TOP_K, tm), lambda bi, i: (0, bi * nt + i))
    outs = pl.pallas_call(
        functools.partial(_outproj_router_kernel, tm=tm),
        grid=(b, nt),
        in_specs=[
            ytile(), ytile(), ytile(), ytile(),
            pl.BlockSpec((None, 4 * GROUP, d), lambda bi, i: (layer, 0, 0)),
            pl.BlockSpec((1, tm, d), lambda bi, i: (bi, i, 0)),
            bvec(),
            pl.BlockSpec((1, 1, d), lambda bi, i: (layer, 0, 0)),
            bvec(), bvec(),
            pl.BlockSpec((1, N_EXPERTS, d), lambda bi, i: (layer, 0, 0)),
            pl.BlockSpec((1, N_EXPERTS, 1), lambda bi, i: (layer, 0, 0)),
        ],
        out_specs=[
            pl.BlockSpec((1, tm, d), lambda bi, i: (bi, i, 0)),
            pl.BlockSpec((tm * slab, LANES), lambda bi, i: (bi * nt + i, 0)),
            tok, tok, tok,
            pl.BlockSpec((N_EXPERTS, LANES), lambda bi, i: (0, 0)),
        ],
        out_shape=[
            jax.ShapeDtypeStruct((b, t, d), F32),
            jax.ShapeDtypeStruct((n * slab, LANES), U32),
            jax.ShapeDtypeStruct((TOP_K, n), I32),
            jax.ShapeDtypeStruct((TOP_K, n), I32),
            jax.ShapeDtypeStruct((TOP_K, n), F32),
            jax.ShapeDtypeStruct((N_EXPERTS, LANES), F32),
        ],
        scratch_shapes=[pltpu.VMEM((N_EXPERTS, LANES), F32)],
        compiler_params=_params("arbitrary", "arbitrary"),
        name="outproj_router",
    )(*ys, w_out_bf16, x, gate1, norm2_g, scale2, shift2, router_wt, router_b)
    return outs


def _scatter_kernel(zs_ref, pos_ref, h_ref, xs_hbm, zbuf, zsem, sem, *, tm, tm_e):
    zrows = zbuf.shape[0]

    @pl.when(pl.program_id(0) == 0)
    def _():
        zbuf[...] = jnp.zeros_like(zbuf)

        def zero(e, c):
            @pl.when(zs_ref[e] >= 0)
            def _():
                for q in range(tm_e // zrows):
                    pltpu.make_async_copy(zbuf, xs_hbm.at[pl.ds(zs_ref[e] + q * zrows, zrows)], zsem).start()
            return c

        lax.fori_loop(0, N_EXPERTS, zero, 0)

        def zero_done(e, c):
            @pl.when(zs_ref[e] >= 0)
            def _():
                for q in range(tm_e // zrows):
                    pltpu.make_async_copy(zbuf, xs_hbm.at[pl.ds(0, zrows)], zsem).wait()
            return c

        lax.fori_loop(0, N_EXPERTS, zero_done, 0)

    def issue(g, c):
        for u in range(ISSUE_UNROLL):
            tt = g * ISSUE_UNROLL + u
            for k in range(TOP_K):
                pltpu.make_async_copy(h_ref.at[tt], xs_hbm.at[pos_ref[k * tm + tt]], sem).start(priority=k % 2)
        return c

    lax.fori_loop(0, tm // ISSUE_UNROLL, issue, 0)
    for k in range(TOP_K):
        pltpu.make_async_copy(h_ref, xs_hbm.at[pl.ds(0, tm)], sem).wait()


def _block_major_positions(pos, tm):
    n = pos.shape[1]
    return pos.reshape(TOP_K, n // tm, tm).transpose(1, 0, 2).reshape(-1)


def _scatter_rows(zero_start, pos_flat, h_rows, m_pad, tm_e):
    n, slab, _ = h_rows.shape
    tm = SCATTER_TOKENS
    zrows = min(tm_e, 256)
    grid_spec = pltpu.PrefetchScalarGridSpec(
        num_scalar_prefetch=1,
        grid=(n // tm,),
        in_specs=[
            pl.BlockSpec((TOP_K * tm,), lambda i, zs: (i,), memory_space=pltpu.SMEM),
            pl.BlockSpec((tm, slab, LANES), lambda i, zs: (i, 0, 0)),
        ],
        out_specs=pl.BlockSpec(memory_space=pl.ANY),
        scratch_shapes=[pltpu.VMEM((zrows, slab, LANES), U32), pltpu.SemaphoreType.DMA(()),
                        pltpu.SemaphoreType.DMA(())],
    )
    return pl.pallas_call(
        functools.partial(_scatter_kernel, tm=tm, tm_e=tm_e),
        grid_spec=grid_spec,
        out_shape=jax.ShapeDtypeStruct((m_pad, slab, LANES), U32),
        compiler_params=_params("arbitrary"),
        name="scatter_rows",
    )(zero_start, pos_flat, h_rows)


def _deinterleave(hh):
    m, width = hh.shape
    lane = lax.broadcasted_iota(I32, (m, LANES), 1)
    low = lane < LANES // 2
    evens_then_odds = jnp.where(low, 2 * lane, 2 * lane - (LANES - 1))
    parts = [jnp.take_along_axis(hh[:, b * LANES:(b + 1) * LANES], evens_then_odds, axis=1)
             for b in range(width // LANES)]
    gates, lins = [], []
    for b in range(0, len(parts), 2):
        first, second = parts[b], parts[b + 1]
        gates.append(jnp.where(low, first, pltpu.roll(second, LANES // 2, 1)))
        lins.append(jnp.where(low, pltpu.roll(first, LANES // 2, 1), second))
    return jnp.concatenate(gates, axis=1), jnp.concatenate(lins, axis=1)


def _expert_kernel(te_ref, tv_ref, nu_ref, xs_ref, w1_ref, b1_ref, w2_ref, b2_ref, ys_ref,
                   x_scr, acc_scr, *, tm, sub, nc):
    i = pl.program_id(0)
    c = pl.program_id(1)
    d = x_scr.shape[1]
    n_slab = d // 2 // LANES
    live = i < nu_ref[0]

    def unpack_x():
        for s in range(n_slab):
            lo, hi = _unpack_rows(xs_ref[pl.ds(s, tm, stride=n_slab), :])
            x_scr[:, s * LANES:(s + 1) * LANES] = lo.astype(BF16)
            x_scr[:, d // 2 + s * LANES:d // 2 + (s + 1) * LANES] = hi.astype(BF16)

    def ffn_rows(n_rows):
        hh = jnp.dot(x_scr[0:n_rows, :], w1_ref[...].astype(BF16), preferred_element_type=F32) + b1_ref[...]
        g, lin = _deinterleave(hh)
        g = jnp.minimum(g, SWIGLU_LIMIT)
        lin = jnp.clip(lin, -SWIGLU_LIMIT, SWIGLU_LIMIT)
        act = g * _sigmoid(SWIGLU_ALPHA * g) * (lin + 1.0)
        return jnp.dot(act.astype(BF16), w2_ref[...].astype(BF16), preferred_element_type=F32)

    def pack_y(read_rows):
        for sb in range(tm // sub):
            _pack_rows(read_rows(sb * sub, (sb + 1) * sub), ys_ref, row0=sb * sub)

    @pl.when(live)
    def _():
        n_sub = tm // sub
        filled = (tv_ref[i] + sub - 1) // sub
        full = filled == n_sub
        first = c == 0
        last = c == nc - 1

        @pl.when(full & first)
        def _():
            unpack_x()
            acc_scr[...] = b2_ref[...] + ffn_rows(tm)

        @pl.when(full & last)
        def _():
            y = acc_scr[...] + ffn_rows(tm)
            pack_y(lambda r0, r1: y[r0:r1])

        @pl.when(full & jnp.logical_not(first | last))
        def _():
            acc_scr[...] += ffn_rows(tm)

        @pl.when(jnp.logical_not(full) & first)
        def _():
            unpack_x()
            acc_scr[...] = jnp.zeros_like(acc_scr) + b2_ref[...]

        for nb in range(1, n_sub):
            @pl.when(filled == nb)
            def _(nb=nb):
                acc_scr[0:nb * sub, :] += ffn_rows(nb * sub)

        @pl.when(jnp.logical_not(full) & last)
        def _():
            pack_y(lambda r0, r1: acc_scr[r0:r1, :])

    @pl.when(jnp.logical_not(live) & (c == 0))
    def _():
        ys_ref[...] = jnp.zeros_like(ys_ref)


def _experts(tile_expert, tile_valid, n_used, xs2d, w1, b1, w2, b2, layer, tm, sub, m_pad):
    _, n_exp, d, h2 = w1.shape
    hid = h2 // 2
    tc = min(hid // 2, 512)
    nc = hid // tc
    assert nc >= 2
    n_tiles = m_pad // tm
    slab = d // 2 // LANES

    def live(i, nu):
        return jnp.minimum(i, nu[0] - 1)

    def chunk(i, c, nu):
        return jnp.where(i < nu[0], c, nc - 1)

    grid_spec = pltpu.PrefetchScalarGridSpec(
        num_scalar_prefetch=3,
        grid=(n_tiles, nc),
        in_specs=[
            pl.BlockSpec((tm * slab, LANES), lambda i, c, te, tv, nu: (live(i, nu), 0)),
            pl.BlockSpec((None, None, d, 2 * tc), lambda i, c, te, tv, nu: (layer, te[i], 0, chunk(i, c, nu))),
            pl.BlockSpec((None, None, 1, 2 * tc), lambda i, c, te, tv, nu: (layer, te[i], 0, chunk(i, c, nu))),
            pl.BlockSpec((None, None, tc, d), lambda i, c, te, tv, nu: (layer, te[i], chunk(i, c, nu), 0)),
            pl.BlockSpec((None, None, 1, d), lambda i, c, te, tv, nu: (layer, te[i], 0, 0)),
        ],
        out_specs=pl.BlockSpec((tm * slab, LANES), lambda i, c, te, tv, nu: (i, 0)),
        scratch_shapes=[pltpu.VMEM((tm, d), BF16), pltpu.VMEM((tm, d), F32)],
    )
    return pl.pallas_call(
        functools.partial(_expert_kernel, tm=tm, sub=sub, nc=nc),
        grid_spec=grid_spec,
        out_shape=jax.ShapeDtypeStruct((m_pad * slab, LANES), U32),
        compiler_params=_params("arbitrary", "arbitrary"),
        name="experts",
    )(tile_expert, tile_valid, n_used, xs2d, w1, b1, w2, b2)


def _combine_kernel(pos_ref, pos_next_ref, wts_ref, x_ref, g2_ref, fg_ref, ys_hbm, ys_flat_hbm, o_ref, buf, sem,
                    *, tm, final, n_steps):
    i = pl.program_id(0)
    n_slab = x_ref.shape[1] // 2 // LANES
    slot_rows = TOP_K * tm * n_slab
    slot = i % 2
    base = pl.multiple_of(slot * slot_rows, slot_rows)

    def issue_block(p_ref, to_slot):
        to_base = to_slot * slot_rows

        def issue(g, c):
            for u in range(ISSUE_UNROLL):
                tt = g * ISSUE_UNROLL + u
                for k in range(TOP_K):
                    row0 = pl.multiple_of(to_base + (k * tm + tt) * n_slab, n_slab)
                    pltpu.make_async_copy(ys_hbm.at[p_ref[k * tm + tt]], buf.at[pl.ds(row0, n_slab)],
                                          sem.at[to_slot]).start(priority=k % 2)
            return c

        lax.fori_loop(0, tm // ISSUE_UNROLL, issue, 0)

    @pl.when(i == 0)
    def _():
        issue_block(pos_ref, 0)

    @pl.when(i + 1 < n_steps)
    def _():
        issue_block(pos_next_ref, 1 - slot)

    pltpu.make_async_copy(ys_flat_hbm.at[pl.ds(0, slot_rows)], buf.at[pl.ds(base, slot_rows)], sem.at[slot]).wait()

    w_sq = jnp.concatenate([wts_ref[...], jnp.zeros((tm - TOP_K, tm), F32)], axis=0)
    w_t = w_sq.T
    w_k = [jnp.broadcast_to(w_t[:, k:k + 1], (tm, LANES)) for k in range(TOP_K)]
    lows, highs = [], []
    for s in range(n_slab):
        acc_lo = jnp.zeros((tm, LANES), F32)
        acc_hi = jnp.zeros((tm, LANES), F32)
        for k in range(TOP_K):
            lo, hi = _unpack_rows(buf[pl.ds(base + k * tm * n_slab + s, tm, stride=n_slab), :])
            acc_lo = acc_lo + w_k[k] * lo
            acc_hi = acc_hi + w_k[k] * hi
        lows.append(acc_lo)
        highs.append(acc_hi)
    moe = jnp.concatenate(lows + highs, axis=1)
    x2 = x_ref[...] + g2_ref[0] * moe
    if final:
        x2 = x2 * lax.rsqrt(jnp.mean(x2 * x2, axis=-1, keepdims=True) + RMS_EPS) * fg_ref[...]
    o_ref[...] = x2


def _combine(pos_flat, wts, x1, gate2, final_g, ys_rows, tokens_per_batch, final):
    n, d = x1.shape
    tm = GATHER_TOKENS
    steps_per_batch = tokens_per_batch // tm
    n_steps = n // tm
    m_pad, slab, _ = ys_rows.shape
    return pl.pallas_call(
        functools.partial(_combine_kernel, tm=tm, final=final, n_steps=n_steps),
        grid=(n_steps,),
        in_specs=[
            pl.BlockSpec((TOP_K * tm,), lambda i: (i,), memory_space=pltpu.SMEM),
            pl.BlockSpec((TOP_K * tm,), lambda i: (jnp.minimum(i + 1, n_steps - 1),), memory_space=pltpu.SMEM),
            pl.BlockSpec((TOP_K, tm), lambda i: (0, i)),
            pl.BlockSpec((tm, d), lambda i: (i, 0)),
            pl.BlockSpec((1, 1, d), lambda i: (i // steps_per_batch, 0, 0)),
            pl.BlockSpec((1, d), lambda i: (0, 0)),
            pl.BlockSpec(memory_space=pl.ANY),
            pl.BlockSpec(memory_space=pl.ANY),
        ],
        out_specs=pl.BlockSpec((tm, d), lambda i: (i, 0)),
        out_shape=jax.ShapeDtypeStruct((n, d), F32),
        scratch_shapes=[pltpu.VMEM((2 * TOP_K * tm * slab, LANES), U32), pltpu.SemaphoreType.DMA((2,))],
        compiler_params=_params("arbitrary"),
        name="combine",
    )(pos_flat, pos_flat, wts, x1, gate2, final_g, ys_rows, ys_rows.reshape(m_pad * slab, LANES))


def _routing_plan(idx, rank, counts, tm_e, n_tiles):
    experts = jnp.arange(N_EXPERTS, dtype=I32)
    cnt = counts[:, 0].astype(I32)
    tiles = (cnt + tm_e - 1) // tm_e
    tile_end = jnp.cumsum(tiles)
    tile_start = tile_end - tiles
    offsets = tile_start * tm_e
    pos = rank + jnp.sum(jnp.where(idx[..., None] == experts, offsets, 0), axis=-1)
    n_used = tile_end[-1]
    tile_ids = jnp.arange(n_tiles, dtype=I32)
    te = jnp.sum((tile_ids[:, None] >= tile_end[None, :]).astype(I32), axis=1)
    te_last = jnp.sum((n_used - 1 >= tile_end).astype(I32))
    te = jnp.where(tile_ids < n_used, te, te_last).astype(I32)
    mine = te[:, None] == experts[None, :]
    rows_left = jnp.sum(jnp.where(mine, cnt[None, :] - (tile_ids[:, None] - tile_start[None, :]) * tm_e, 0), axis=1)
    tile_valid = jnp.where(tile_ids < n_used, jnp.clip(rows_left, 0, tm_e), 0).astype(I32)
    zero_start = jnp.where(tiles > 0, (tile_end - 1) * tm_e, -1).astype(I32)
    return pos.astype(I32), te, tile_valid, n_used.reshape(1).astype(I32), zero_start


def kernel(x, c, positions, mod_w, mod_b, norm1_g, norm2_g, w_in, pool_w, pool_scale, diff_lq1, diff_lk1,
           diff_lq2, diff_lk2, diff_subln_g, conv_dw_w, conv_dw_b, conv_ln_g, conv_ln_b, conv_pw_w, conv_pw_b,
           w_out, router_w, router_b, exp_w1, exp_b1, exp_w2, exp_b2, final_g):
    b, t, d = x.shape
    depth = mod_w.shape[0]
    n = b * t
    tm_e = 1024
    sub_e = 256
    m_pad = n * TOP_K + N_EXPERTS * tm_e
    n_tiles = m_pad // tm_e
    slab = d // 2 // LANES

    def row3(a):
        return a.reshape(a.shape[0], 1, a.shape[1])

    mod = _modulation(c, mod_w, mod_b)
    tables = _rope_tables(positions)
    w_in_b = w_in.astype(BF16)
    w_out_b = w_out.astype(BF16)
    pool_w_b = pool_w.astype(BF16)
    pw_w_b = conv_pw_w.astype(BF16)
    router_wt = jnp.swapaxes(router_w, 1, 2)
    router_b3 = router_b.reshape(depth, N_EXPERTS, 1)
    b1r = exp_b1.reshape(depth, N_EXPERTS, 1, exp_b1.shape[-1])
    b2r = exp_b2.reshape(depth, N_EXPERTS, 1, d)
    final_g2 = final_g.reshape(1, d)

    for l in range(depth):
        lambda_init = 0.8 - 0.6 * math.exp(-0.3 * l)
        sh1, sc1, g1, sh2, sc2, g2 = [m.reshape(b, 1, d) for m in jnp.split(mod[l], 6, axis=-1)]
        z = _inproj(x, row3(norm1_g), sc1, sh1, w_in_b, tables, l)
        y_pool, y_conv = _poolconv(z, pool_w_b, row3(pool_scale), conv_dw_w, row3(conv_dw_b), row3(conv_ln_g),
                                   row3(conv_ln_b), pw_w_b, row3(conv_pw_b), l)
        y_diff = _diff_attention(z, row3(diff_lq1), row3(diff_lk1), row3(diff_lq2), row3(diff_lk2),
                                 row3(diff_subln_g), l, lambda_init)
        y_dil = _dilated_attention(z)
        x1, h2, idx, rank, wts, counts = _outproj_router(
            (y_pool, y_diff, y_dil, y_conv), w_out_b, x, g1, row3(norm2_g), sc2, sh2, router_wt, router_b3, l)
        pos, tile_expert, tile_valid, n_used, zero_start = _routing_plan(idx, rank, counts, tm_e, n_tiles)
        xs = _scatter_rows(zero_start, _block_major_positions(pos, SCATTER_TOKENS), h2.reshape(n, slab, LANES),
                           m_pad, tm_e)
        ys = _experts(tile_expert, tile_valid, n_used, xs.reshape(m_pad * slab, LANES), exp_w1, b1r, exp_w2, b2r,
                      l, tm_e, sub_e, m_pad)
        x = _combine(_block_major_positions(pos, GATHER_TOKENS), wts, x1.reshape(n, d), g2, final_g2,
                     ys.reshape(m_pad, slab, LANES), t, final=(l == depth - 1)).reshape(b, t, d)
    return x
```

```python
import functools
import math

import numpy as np
import jax
import jax.numpy as jnp
from jax import lax
from jax.experimental import pallas as pl
from jax.experimental.pallas import tpu as pltpu

F32 = jnp.float32
BF16 = jnp.bfloat16
I32 = jnp.int32
U32 = jnp.uint32
HIGHEST = lax.Precision.HIGHEST

LANES = 128
SUBLANES = 8
VMEM_LIMIT_BYTES = 60 * 1024 * 1024

POOL_WINDOWS = (2, 4, 8, 16)
DILATED_PAIRS = ((128, 1), (512, 4), (2048, 16))
DIL_BAND = 128
CONV_WIDTH = 31
HALO = 32
ISSUE_UNROLL = 8
SCATTER_TOKENS = 512
GATHER_TOKENS = 256
N_EXPERTS = 32
TOP_K = 4
SWIGLU_ALPHA = 1.702
SWIGLU_LIMIT = 7.0
ROPE_THETA = 500000.0
RMS_EPS = 1e-6
LN_EPS = 1e-5
NEG_INF = -1e30
GROUP = 512
HEAD = 128
DIFF_QK = 64
LOG2E = math.log2(math.e)


def _params(*semantics):
    return pltpu.CompilerParams(dimension_semantics=semantics, vmem_limit_bytes=VMEM_LIMIT_BYTES)


def _sigmoid(x):
    return 1.0 / (1.0 + jnp.exp(-x))


def _mod_kernel(c_ref, w_ref, b_ref, o_ref):
    c = c_ref[...]
    ca = c * _sigmoid(c)
    w = w_ref[0]
    c_hi = ca.astype(BF16)
    c_lo = (ca - c_hi.astype(F32)).astype(BF16)
    w_hi = w.astype(BF16)
    w_lo = (w - w_hi.astype(F32)).astype(BF16)
    prod = jnp.dot(c_hi, w_hi, preferred_element_type=F32)
    prod = prod + (jnp.dot(c_lo, w_hi, preferred_element_type=F32) + jnp.dot(c_hi, w_lo, preferred_element_type=F32))
    o_ref[0] = prod + b_ref[0]


def _modulation(c, mod_w, mod_b):
    depth, d, n = mod_w.shape
    b = c.shape[0]
    rows = 8
    c_pad = jnp.zeros((rows, d), F32).at[:b].set(c)
    tn = 512
    out = pl.pallas_call(
        _mod_kernel,
        grid=(depth, n // tn),
        in_specs=[
            pl.BlockSpec((rows, d), lambda l, j: (0, 0)),
            pl.BlockSpec((1, d, tn), lambda l, j: (l, 0, j)),
            pl.BlockSpec((1, 1, tn), lambda l, j: (l, 0, j)),
        ],
        out_specs=pl.BlockSpec((1, rows, tn), lambda l, j: (l, 0, j)),
        out_shape=jax.ShapeDtypeStruct((depth, rows, n), F32),
        compiler_params=_params("arbitrary", "arbitrary"),
        name="modulation",
    )(c_pad, mod_w, mod_b.reshape(depth, 1, n))
    return out[:, :b]


def _rope_table_kernel(pos_ref, invd_ref, invc_ref, cd_ref, sd_ref, cc_ref, sc_ref):
    p = pos_ref[0]
    lane = lax.broadcasted_iota(I32, (1, LANES), 1)
    for inv_ref, c_ref, s_ref, hd in ((invd_ref, cd_ref, sd_ref, DIFF_QK), (invc_ref, cc_ref, sc_ref, HEAD)):
        half = hd // 8
        lm = lane % hd
        ang = p * inv_ref[...]
        c_ref[0] = jnp.cos(ang)
        s_ref[0] = jnp.where(lm < half, -jnp.sin(ang), jnp.sin(ang))


def _lane_inv_freq(hd):
    half = hd // 8
    inv = ROPE_THETA ** (-jnp.arange(half, dtype=F32) / half)
    lm = np.arange(LANES) % hd
    rotated = jnp.asarray(lm < 2 * half)
    return jnp.where(rotated, inv[lm % half], 0.0).reshape(1, LANES).astype(F32)


def _rope_tables(positions):
    b, t = positions.shape
    tm = min(t, 1024)
    pos = positions.astype(F32).reshape(b, t, 1)
    spec_t = pl.BlockSpec((1, tm, LANES), lambda bi, i: (bi, i, 0))
    spec_inv = pl.BlockSpec((1, LANES), lambda bi, i: (0, 0))
    shp = jax.ShapeDtypeStruct((b, t, LANES), F32)
    return pl.pallas_call(
        _rope_table_kernel,
        grid=(b, t // tm),
        in_specs=[pl.BlockSpec((1, tm, 1), lambda bi, i: (bi, i, 0)), spec_inv, spec_inv],
        out_specs=[spec_t] * 4,
        out_shape=[shp] * 4,
        compiler_params=_params("arbitrary", "arbitrary"),
        name="rope_tables",
    )(pos, _lane_inv_freq(DIFF_QK), _lane_inv_freq(HEAD))


def _rope_apply(z, cos, sin, hd):
    half = hd // 8
    lane = lax.broadcasted_iota(I32, (1, LANES), 1)
    first = (lane % hd) < half
    outs = []
    for cb in range(z.shape[1] // LANES):
        zc = z[:, cb * LANES:(cb + 1) * LANES]
        partner = jnp.where(first, pltpu.roll(zc, LANES - half, 1), pltpu.roll(zc, half, 1))
        outs.append(zc * cos + partner * sin)
    return jnp.concatenate(outs, axis=1)


def _inproj_kernel(x_ref, g_ref, sc_ref, sh_ref, w_ref, cd_ref, sd_ref, cc_ref, sc2_ref, z_ref, h_scr):
    j = pl.program_id(2)

    def project():
        return jnp.dot(h_scr[...], w_ref[...].astype(BF16), preferred_element_type=F32)

    @pl.when(j == 0)
    def _():
        x = x_ref[0]
        y = x * lax.rsqrt(jnp.mean(x * x, axis=-1, keepdims=True) + RMS_EPS) * g_ref[0]
        h = (y * (1.0 + sc_ref[0]) + sh_ref[0]).astype(BF16)
        h_scr[...] = h
        z_ref[0] = jnp.dot(h, w_ref[...].astype(BF16), preferred_element_type=F32).astype(z_ref.dtype)

    @pl.when((j == 1) | (j == 2))
    def _():
        r = _rope_apply(project(), cd_ref[0], sd_ref[0], DIFF_QK)
        r = r * jnp.where(j == 1, DIFF_QK ** -0.5 * LOG2E, 1.0)
        z_ref[0] = r.astype(z_ref.dtype)

    @pl.when((j == 4) | (j == 5))
    def _():
        r = _rope_apply(project(), cc_ref[0], sc2_ref[0], HEAD)
        r = r * jnp.where(j == 4, HEAD ** -0.5 * LOG2E, 1.0)
        z_ref[0] = r.astype(z_ref.dtype)

    @pl.when((j == 3) | (j >= 6))
    def _():
        z_ref[0] = project().astype(z_ref.dtype)


def _inproj(x, norm_g, scale, shift, w_in_bf16, tables, layer):
    b, t, d = x.shape
    n = w_in_bf16.shape[-1]
    tm = min(t, 1024)
    tn = GROUP
    cd, sd, cc, sc = tables
    spec_tab = pl.BlockSpec((1, tm, LANES), lambda bi, i, j: (bi, i, 0))
    spec_vec = pl.BlockSpec((1, 1, d), lambda bi, i, j: (bi, 0, 0))
    return pl.pallas_call(
        _inproj_kernel,
        grid=(b, t // tm, n // tn),
        in_specs=[
            pl.BlockSpec((1, tm, d), lambda bi, i, j: (bi, i, 0)),
            pl.BlockSpec((1, 1, d), lambda bi, i, j: (layer, 0, 0)),
            spec_vec, spec_vec,
            pl.BlockSpec((None, d, tn), lambda bi, i, j: (layer, 0, j)),
            spec_tab, spec_tab, spec_tab, spec_tab,
        ],
        out_specs=pl.BlockSpec((1, tm, tn), lambda bi, i, j: (bi, i, j)),
        out_shape=jax.ShapeDtypeStruct((b, t, n), BF16),
        scratch_shapes=[pltpu.VMEM((tm, d), BF16)],
        compiler_params=_params("arbitrary", "arbitrary", "arbitrary"),
        name="inproj",
    )(x, norm_g, scale, shift, w_in_bf16, cd, sd, cc, sc)


def _poolconv_kernel(zp_ref, zph_ref, za_ref, zah_ref, zg_ref, zgh_ref,
                     pw_ref, ps_ref, dww_ref, dwb_ref, lng_ref, lnb_ref, pww_ref, pwb_ref,
                     yp_ref, yc_ref, xp_scr, u_scr, ush_scr, *, tm):
    i = pl.program_id(1)
    keep = jnp.where(i == 0, 0.0, 1.0)

    xp_scr[0:HALO, :] = zph_ref[0].astype(F32) * keep
    xp_scr[HALO:, :] = zp_ref[0].astype(F32)
    t_glob = i * tm + lax.broadcasted_iota(I32, (tm, 1), 0)
    for g, w in enumerate(POOL_WINDOWS):
        cols = slice(g * LANES, (g + 1) * LANES)
        xg = xp_scr[HALO:HALO + tm, cols]
        acc = xg
        for k in range(1, w):
            acc = acc + xp_scr[HALO - k:HALO - k + tm, cols]
        cnt = jnp.minimum(t_glob + 1, w).astype(F32)
        pooled = acc / cnt - xg
        yg = jnp.dot(pooled.astype(BF16), pw_ref[0, g], preferred_element_type=F32)
        yp_ref[0, :, cols] = (yg * ps_ref[0, :, cols]).astype(yp_ref.dtype)

    ah = zah_ref[0].astype(F32)
    gh = zgh_ref[0].astype(F32)
    u_scr[0:HALO, :] = ah * _sigmoid(gh) * keep
    a = za_ref[0].astype(F32)
    gg = zg_ref[0].astype(F32)
    u_scr[HALO:, :] = a * _sigmoid(gg)
    span = tm + HALO - SUBLANES
    for sh in range(1, SUBLANES):
        ush_scr[sh - 1] = u_scr[sh:sh + span, :]
    acc = jnp.zeros((tm, GROUP), F32) + dwb_ref[0]
    base = HALO - (CONV_WIDTH - 1)
    for k in range(CONV_WIDTH):
        off = base + k
        sh, start = off % SUBLANES, off - off % SUBLANES
        window = u_scr[start:start + tm, :] if sh == 0 else ush_scr[sh - 1, start:start + tm, :]
        acc = acc + window * dww_ref[0, k:k + 1, :]
    mu = jnp.mean(acc, axis=-1, keepdims=True)
    cen = acc - mu
    var = jnp.mean(cen * cen, axis=-1, keepdims=True)
    v = cen * lax.rsqrt(var + LN_EPS) * lng_ref[0] + lnb_ref[0]
    v = v * _sigmoid(v)
    y = jnp.dot(v.astype(BF16), pww_ref[0], preferred_element_type=F32) + pwb_ref[0]
    yc_ref[0] = y.astype(yc_ref.dtype)


def _poolconv(z, pool_w_bf16, pool_scale, dw_w, dw_b, ln_g, ln_b, pw_w_bf16, pw_b, layer):
    b, t, _ = z.shape
    tm = min(t, 512)
    r = tm // HALO
    a_blk = 7
    g_blk = 8

    def cur(col):
        return pl.BlockSpec((1, tm, GROUP), lambda bi, i: (bi, i, col))

    def halo(col):
        return pl.BlockSpec((1, HALO, GROUP), lambda bi, i: (bi, jnp.maximum(i * r - 1, 0), col))

    def vec(n):
        return pl.BlockSpec((1, 1, n), lambda bi, i: (layer, 0, 0))

    out_spec = pl.BlockSpec((1, tm, GROUP), lambda bi, i: (bi, i, 0))
    shp = jax.ShapeDtypeStruct((b, t, GROUP), BF16)
    return pl.pallas_call(
        functools.partial(_poolconv_kernel, tm=tm),
        grid=(b, t // tm),
        in_specs=[
            cur(0), halo(0), cur(a_blk), halo(a_blk), cur(g_blk), halo(g_blk),
            pl.BlockSpec((1, len(POOL_WINDOWS), LANES, LANES), lambda bi, i: (layer, 0, 0, 0)),
            vec(GROUP),
            pl.BlockSpec((1, CONV_WIDTH, GROUP), lambda bi, i: (layer, 0, 0)),
            vec(GROUP), vec(GROUP), vec(GROUP),
            pl.BlockSpec((1, GROUP, GROUP), lambda bi, i: (layer, 0, 0)),
            vec(GROUP),
        ],
        out_specs=[out_spec, out_spec],
        out_shape=[shp, shp],
        scratch_shapes=[pltpu.VMEM((tm + HALO, GROUP), F32), pltpu.VMEM((tm + HALO, GROUP), F32),
                        pltpu.VMEM((SUBLANES - 1, tm + HALO - SUBLANES, GROUP), F32)],
        compiler_params=_params("arbitrary", "arbitrary"),
        name="pool_conv",
    )(z, z, z, z, z, z, pool_w_bf16, pool_scale, dw_w, dw_b, ln_g, ln_b, pw_w_bf16, pw_b)


def _diff_kernel(q_ref, k_ref, v_ref, lq1_ref, lk1_ref, lq2_ref, lk2_ref, g_ref, o_ref, *, tq, lambda_init):
    qi = pl.program_id(2)
    q = q_ref[0]
    lane = lax.broadcasted_iota(I32, (tq, HEAD), 1)
    zero = jnp.zeros_like(q)
    q2 = jnp.concatenate([jnp.where(lane < DIFF_QK, q, zero), jnp.where(lane >= DIFF_QK, q, zero)], axis=0)
    rows = 2 * tq

    def lane_tiles(s):
        return [s[:, c * LANES:(c + 1) * LANES] for c in range(tq // LANES)]

    def fold(j, carry, masked):
        m, l_run, acc = carry
        kb = k_ref[0, pl.ds(pl.multiple_of(j * tq, tq), tq), :]
        vb = v_ref[0, pl.ds(pl.multiple_of(j * tq, tq), tq), :]
        s = lax.dot_general(q2, kb, (((1,), (1,)), ((), ())), preferred_element_type=F32)
        if masked:
            row = lax.broadcasted_iota(I32, (rows, tq), 0) % tq
            col = lax.broadcasted_iota(I32, (rows, tq), 1)
            s = jnp.where(col <= row, s, NEG_INF)
        tiles = lane_tiles(s)
        m_blk = tiles[0]
        for part in tiles[1:]:
            m_blk = jnp.maximum(m_blk, part)
        m_new = jnp.maximum(m, jnp.max(m_blk, axis=-1, keepdims=True))
        alpha = jnp.exp2(m - m_new)
        p = jnp.exp2(s - m_new)
        l_run = alpha * l_run
        for part in lane_tiles(p):
            l_run = l_run + part
        acc = alpha * acc + jnp.dot(p.astype(BF16), vb, preferred_element_type=F32)
        return m_new, l_run, acc

    init = (jnp.full((rows, 1), NEG_INF, F32), jnp.zeros((rows, LANES), F32), jnp.zeros((rows, HEAD), F32))
    carry = lax.fori_loop(0, qi, lambda j, c: fold(j, c, False), init)
    _, l_run, acc = fold(qi, carry, True)
    l = jnp.sum(l_run, axis=-1, keepdims=True)

    lam =(jnp.exp(jnp.sum(lq1_ref[0] * lk1_ref[0], axis=-1, keepdims=True))
           - jnp.exp(jnp.sum(lq2_ref[0] * lk2_ref[0], axis=-1, keepdims=True)) + lambda_init)
    o = acc / l
    od = o[:tq] - lam * o[tq:]
    y = od * lax.rsqrt(jnp.mean(od * od, axis=-1, keepdims=True) + RMS_EPS) * g_ref[0]
    o_ref[0] = (y * (1.0 - lambda_init)).astype(o_ref.dtype)


def _diff_attention(z, lq1, lk1, lq2, lk2, subln_g, layer, lambda_init):
    b, t, _ = z.shape
    heads = GROUP // HEAD
    tq = min(t, 512)
    q0, k0, v0 = 1 * heads, 2 * heads, 3 * heads

    def vec(n):
        return pl.BlockSpec((1, 1, n), lambda bi, h, i: (layer, 0, 0))

    return pl.pallas_call(
        functools.partial(_diff_kernel, tq=tq, lambda_init=lambda_init),
        grid=(b, heads, t // tq),
        in_specs=[
            pl.BlockSpec((1, tq, HEAD), lambda bi, h, i: (bi, i, q0 + h)),
            pl.BlockSpec((1, t, HEAD), lambda bi, h, i: (bi, 0, k0 + h)),
            pl.BlockSpec((1, t, HEAD), lambda bi, h, i: (bi, 0, v0 + h)),
            vec(DIFF_QK), vec(DIFF_QK), vec(DIFF_QK), vec(DIFF_QK), vec(HEAD),
        ],
        out_specs=pl.BlockSpec((1, tq, HEAD), lambda bi, h, i: (bi, i, h)),
        out_shape=jax.ShapeDtypeStruct((b, t, GROUP), BF16),
        compiler_params=_params("arbitrary", "arbitrary", "arbitrary"),
        name="diff_attention",
    )(z, z, z, lq1, lk1, lq2, lk2, subln_g)


def _dilated_kernel(q_ref, k_ref, v_ref, o_ref, qf, kf, vf, o0, o1, o2, e0, e1, e2, *, t, group):
    qf[...] = q_ref[0].astype(F32)
    kf[...] = k_ref[0].astype(F32)
    vf[...] = v_ref[0].astype(F32)
    band = DIL_BAND
    a_idx = lax.broadcasted_iota(I32, (band, 2 * band), 0)
    b_idx = lax.broadcasted_iota(I32, (band, 2 * band), 1)
    in_band = (b_idx >= a_idx) & (b_idx <= a_idx + band)
    outs = ((o0, e0), (o1, e1), (o2, e2))

    for (window, dil), (o_scr, e_scr) in zip(DILATED_PAIRS, outs):
        nb = t // (dil * band)

        def one_block(it, dil=dil, nb=nb, o_scr=o_scr, e_scr=e_scr):
            r = it // nb
            n = it % nb
            start = r + n * (band * dil)
            prev = r + jnp.maximum(n - 1, 0) * (band * dil)

            def rows(ref, s0):
                return ref[pl.ds(s0, band, stride=dil), :] if dil > 1 else ref[pl.ds(s0, band), :]

            qb = rows(qf, start).astype(BF16)
            kk = jnp.concatenate([rows(kf, prev), rows(kf, start)], axis=0).astype(BF16)
            vv = jnp.concatenate([rows(vf, prev), rows(vf, start)], axis=0).astype(BF16)
            s = lax.dot_general(qb, kk, (((1,), (1,)), ((), ())), preferred_element_type=F32)
            valid = in_band & (b_idx >= jnp.where(n > 0, 0, band))
            s = jnp.where(valid, s, NEG_INF)
            m = jnp.max(s, axis=-1, keepdims=True)
            p = jnp.exp2(s - m)
            l = jnp.sum(p, axis=-1, keepdims=True)
            o = jnp.dot(p.astype(BF16), vv, preferred_element_type=F32) / l
            idx = pl.ds(start, band, stride=dil) if dil > 1 else pl.ds(start, band)
            o_scr[idx, :] = o
            e_scr[idx, :] = jnp.broadcast_to(m + jnp.log2(l), (band, HEAD))

        def blocks(g, carry, one_block=one_block):
            for u in range(group):
                one_block(g * group + u)
            return carry

        lax.fori_loop(0, dil * nb // group, blocks, 0)

    top = jnp.maximum(jnp.maximum(e0[...], e1[...]), e2[...])
    w0 = jnp.exp2(e0[...] - top)
    w1 = jnp.exp2(e1[...] - top)
    w2 = jnp.exp2(e2[...] - top)
    mix = (w0 * o0[...] + w1 * o1[...] + w2 * o2[...]) / (w0 + w1 + w2)
    o_ref[0] = mix.astype(o_ref.dtype)


def _dilated_attention(z):
    b, t, _ = z.shape
    heads = GROUP // HEAD
    q0, k0, v0 = 4 * heads, 5 * heads, 6 * heads
    group = 16
    assert all(w // dl == DIL_BAND and t % w == 0 for w, dl in DILATED_PAIRS)
    assert (t // DIL_BAND) % group == 0

    def col(c0):
        return pl.BlockSpec((1, t, HEAD), lambda bi, h: (bi, 0, c0 + h))

    return pl.pallas_call(
        functools.partial(_dilated_kernel, t=t, group=group),
        grid=(b, heads),
        in_specs=[col(q0), col(k0), col(v0)],
        out_specs=pl.BlockSpec((1, t, HEAD), lambda bi, h: (bi, 0, h)),
        out_shape=jax.ShapeDtypeStruct((b, t, GROUP), BF16),
        scratch_shapes=[pltpu.VMEM((t, HEAD), F32)] * 9,
        compiler_params=_params("arbitrary", "arbitrary"),
        name="dilated_attention",
    )(z, z, z)


def _pack_rows(h, out_ref, row0=0):
    m, half = h.shape[0], h.shape[1] // 2
    bits = pltpu.bitcast(h.astype(BF16).astype(F32), U32)
    word = (bits[:, half:] & jnp.uint32(0xFFFF0000)) | (bits[:, :half] >> 16)
    n_slab = half // LANES
    for s in range(n_slab):
        out_ref[pl.ds(row0 * n_slab + s, m, stride=n_slab), :] = word[:, s * LANES:(s + 1) * LANES]


def _unpack_rows(word):
    return pltpu.bitcast(word << 16, F32), pltpu.bitcast(word & jnp.uint32(0xFFFF0000), F32)


def _outproj_router_kernel(yp_ref, yd_ref, yc_ref, yv_ref, wo_ref, x_ref, g1_ref, n2_ref, sc_ref, sh_ref,
                           rw_ref, rb_ref,
                           x1_ref, h2_ref, idx_ref, rank_ref, wts_ref, cnt_ref, carry_scr, *, tm):
    first = (pl.program_id(0) == 0) & (pl.program_id(1) == 0)

    @pl.when(first)
    def _():
        carry_scr[...] = jnp.zeros_like(carry_scr)

    mix = jnp.dot(yp_ref[0], wo_ref[0 * GROUP:1 * GROUP, :], preferred_element_type=F32)
    mix = mix + jnp.dot(yd_ref[0], wo_ref[1 * GROUP:2 * GROUP, :], preferred_element_type=F32)
    mix = mix + jnp.dot(yc_ref[0], wo_ref[2 * GROUP:3 * GROUP, :], preferred_element_type=F32)
    mix = mix + jnp.dot(yv_ref[0], wo_ref[3 * GROUP:4 * GROUP, :], preferred_element_type=F32)
    x1 = x_ref[0] + g1_ref[0] * mix
    x1_ref[0] = x1

    y = x1 * lax.rsqrt(jnp.mean(x1 * x1, axis=-1, keepdims=True) + RMS_EPS) * n2_ref[0]
    h2 = y * (1.0 + sc_ref[0]) + sh_ref[0]
    _pack_rows(h2, h2_ref)

    logits = lax.dot_general(rw_ref[0], h2, (((1,), (1,)), ((), ())), precision=HIGHEST,
                             preferred_element_type=F32) + rb_ref[0]
    e_idx = lax.broadcasted_iota(I32, (N_EXPERTS, tm), 0)
    work = logits
    vals, sels, hots = [], [], []
    for _ in range(TOP_K):
        mx = jnp.max(work, axis=0, keepdims=True)
        sel = jnp.min(jnp.where(work == mx, e_idx, N_EXPERTS), axis=0, keepdims=True)
        hot = e_idx == sel
        vals.append(mx)
        sels.append(sel)
        hots.append(hot)
        work = jnp.where(hot, -jnp.inf, work)
    exps = [jnp.exp(v - vals[0]) for v in vals]
    denom = exps[0] + exps[1] + exps[2] + exps[3]

    chosen = jnp.zeros((N_EXPERTS, tm), F32)
    for hot in hots:
        chosen = chosen + hot.astype(F32)
    s_idx = lax.broadcasted_iota(I32, (tm, tm), 0)
    t_idx = lax.broadcasted_iota(I32, (tm, tm), 1)
    upper = (s_idx < t_idx).astype(BF16)
    before = jnp.dot(chosen.astype(BF16), upper, preferred_element_type=F32) + carry_scr[:, 0:1]
    for k in range(TOP_K):
        idx_ref[k:k + 1, :] = sels[k]
        rank_ref[k:k + 1, :] = jnp.sum(jnp.where(hots[k], before, 0.0), axis=0, keepdims=True).astype(I32)
        wts_ref[k:k + 1, :] = exps[k] / denom
    carry_scr[...] = carry_scr[...] + jnp.sum(chosen, axis=1, keepdims=True)
    cnt_ref[...] = carry_scr[...]


def _outproj_router(ys, w_out_bf16, x, gate1, norm2_g, scale2, shift2, router_wt, router_b, layer):
    b, t, d = x.shape
    tm = min(t, 512)
    n = b * t
    nt = t // tm
    slab = d // 2 // LANES

    def ytile():
        return pl.BlockSpec((1, tm, GROUP), lambda bi, i: (bi, i, 0))

    def bvec():
        return pl.BlockSpec((1, 1, d), lambda bi, i: (bi, 0, 0))

    tok = pl.BlockSpec((TOP_K, tm), lambda bi, i: (0, bi * nt + i))
    outs = pl.pallas_call(
        functools.partial(_outproj_router_kernel, tm=tm),
        grid=(b, nt),
        in_specs=[
            ytile(), ytile(), ytile(), ytile(),
            pl.BlockSpec((None, 4 * GROUP, d), lambda bi, i: (layer, 0, 0)),
            pl.BlockSpec((1, tm, d), lambda bi, i: (bi, i, 0)),
            bvec(),
            pl.BlockSpec((1, 1, d), lambda bi, i: (layer, 0, 0)),
            bvec(), bvec(),
            pl.BlockSpec((1, N_EXPERTS, d), lambda bi, i: (layer, 0, 0)),
            pl.BlockSpec((1, N_EXPERTS, 1), lambda bi, i: (layer, 0, 0)),
        ],
        out_specs=[
            pl.BlockSpec((1, tm, d), lambda bi, i: (bi, i, 0)),
            pl.BlockSpec((tm * slab, LANES), lambda bi, i: (bi * nt + i, 0)),
            tok, tok, tok,
            pl.BlockSpec((N_EXPERTS, LANES), lambda bi, i: (0, 0)),
        ],
        out_shape=[
            jax.ShapeDtypeStruct((b, t, d), F32),
            jax.ShapeDtypeStruct((n * slab, LANES), U32),
            jax.ShapeDtypeStruct((TOP_K, n), I32),
            jax.ShapeDtypeStruct((TOP_K, n), I32),
            jax.ShapeDtypeStruct((TOP_K, n), F32),
            jax.ShapeDtypeStruct((N_EXPERTS, LANES), F32),
        ],
        scratch_shapes=[pltpu.VMEM((N_EXPERTS, LANES), F32)],
        compiler_params=_params("arbitrary", "arbitrary"),
        name="outproj_router",
    )(*ys, w_out_bf16, x, gate1, norm2_g, scale2, shift2, router_wt, router_b)
    return outs


def _scatter_kernel(zs_ref, pos_ref, h_ref, xs_hbm, zbuf, zsem, sem, *, tm, tm_e):
    zrows = zbuf.shape[0]

    @pl.when(pl.program_id(0) == 0)
    def _():
        zbuf[...] = jnp.zeros_like(zbuf)

        def zero(e, c):
            @pl.when(zs_ref[e] >= 0)
            def _():
                for q in range(tm_e // zrows):
                    pltpu.make_async_copy(zbuf, xs_hbm.at[pl.ds(zs_ref[e] + q * zrows, zrows)], zsem).start()
            return c

        lax.fori_loop(0, N_EXPERTS, zero, 0)

        def zero_done(e, c):
            @pl.when(zs_ref[e] >= 0)
            def _():
                for q in range(tm_e // zrows):
                    pltpu.make_async_copy(zbuf, xs_hbm.at[pl.ds(0, zrows)], zsem).wait()
            return c

        lax.fori_loop(0, N_EXPERTS, zero_done, 0)

    def issue(g, c):
        for u in range(ISSUE_UNROLL):
            tt = g * ISSUE_UNROLL + u
            for k in range(TOP_K):
                pltpu.make_async_copy(h_ref.at[tt], xs_hbm.at[pos_ref[k * tm + tt]], sem).start(priority=k % 2)
        return c

    lax.fori_loop(0, tm // ISSUE_UNROLL, issue, 0)
    for k in range(TOP_K):
        pltpu.make_async_copy(h_ref, xs_hbm.at[pl.ds(0, tm)], sem).wait()


def _block_major_positions(pos, tm):
    n = pos.shape[1]
    return pos.reshape(TOP_K, n // tm, tm).transpose(1, 0, 2).reshape(-1)


def _scatter_rows(zero_start, pos_flat, h_rows, m_pad, tm_e):
    n, slab, _ = h_rows.shape
    tm = SCATTER_TOKENS
    zrows = min(tm_e, 256)
    grid_spec = pltpu.PrefetchScalarGridSpec(
        num_scalar_prefetch=1,
        grid=(n // tm,),
        in_specs=[
            pl.BlockSpec((TOP_K * tm,), lambda i, zs: (i,), memory_space=pltpu.SMEM),
            pl.BlockSpec((tm, slab, LANES), lambda i, zs: (i, 0, 0)),
        ],
        out_specs=pl.BlockSpec(memory_space=pl.ANY),
        scratch_shapes=[pltpu.VMEM((zrows, slab, LANES), U32), pltpu.SemaphoreType.DMA(()),
                        pltpu.SemaphoreType.DMA(())],
    )
    return pl.pallas_call(
        functools.partial(_scatter_kernel, tm=tm, tm_e=tm_e),
        grid_spec=grid_spec,
        out_shape=jax.ShapeDtypeStruct((m_pad, slab, LANES), U32),
        compiler_params=_params("arbitrary"),
        name="scatter_rows",
    )(zero_start, pos_flat, h_rows)


def _deinterleave(hh):
    m, width = hh.shape
    lane = lax.broadcasted_iota(I32, (m, LANES), 1)
    low = lane < LANES // 2
    evens_then_odds = jnp.where(low, 2 * lane, 2 * lane - (LANES - 1))
    parts = [jnp.take_along_axis(hh[:, b * LANES:(b + 1) * LANES], evens_then_odds, axis=1)
             for b in range(width // LANES)]
    gates, lins = [], []
    for b in range(0, len(parts), 2):
        first, second = parts[b], parts[b + 1]
        gates.append(jnp.where(low, first, pltpu.roll(second, LANES // 2, 1)))
        lins.append(jnp.where(low, pltpu.roll(first, LANES // 2, 1), second))
    return jnp.concatenate(gates, axis=1), jnp.concatenate(lins, axis=1)


def _expert_kernel(te_ref, tv_ref, nu_ref, xs_ref, w1_ref, b1_ref, w2_ref, b2_ref, ys_ref,
                   x_scr, acc_scr, *, tm, sub, nc):
    i = pl.program_id(0)
    c = pl.program_id(1)
    d = x_scr.shape[1]
    n_slab = d // 2 // LANES
    live = i < nu_ref[0]

    def unpack_x():
        for s in range(n_slab):
            lo, hi = _unpack_rows(xs_ref[pl.ds(s, tm, stride=n_slab), :])
            x_scr[:, s * LANES:(s + 1) * LANES] = lo.astype(BF16)
            x_scr[:, d // 2 + s * LANES:d // 2 + (s + 1) * LANES] = hi.astype(BF16)

    def ffn_rows(n_rows):
        hh = jnp.dot(x_scr[0:n_rows, :], w1_ref[...].astype(BF16), preferred_element_type=F32) + b1_ref[...]
        g, lin = _deinterleave(hh)
        g = jnp.minimum(g, SWIGLU_LIMIT)
        lin = jnp.clip(lin, -SWIGLU_LIMIT, SWIGLU_LIMIT)
        act = g * _sigmoid(SWIGLU_ALPHA * g) * (lin + 1.0)
        return jnp.dot(act.astype(BF16), w2_ref[...].astype(BF16), preferred_element_type=F32)

    def pack_y(read_rows):
        for sb in range(tm // sub):
            _pack_rows(read_rows(sb * sub, (sb + 1) * sub), ys_ref, row0=sb * sub)

    @pl.when(live)
    def _():
        n_sub = tm // sub
        filled = (tv_ref[i] + sub - 1) // sub
        full = filled == n_sub
        first = c == 0
        last = c == nc - 1

        @pl.when(full & first)
        def _():
            unpack_x()
            acc_scr[...] = b2_ref[...] + ffn_rows(tm)

        @pl.when(full & last)
        def _():
            y = acc_scr[...] + ffn_rows(tm)
            pack_y(lambda r0, r1: y[r0:r1])

        @pl.when(full & jnp.logical_not(first | last))
        def _():
            acc_scr[...] += ffn_rows(tm)

        @pl.when(jnp.logical_not(full) & first)
        def _():
            unpack_x()
            acc_scr[...] = jnp.zeros_like(acc_scr) + b2_ref[...]

        for nb in range(1, n_sub):
            @pl.when(filled == nb)
            def _(nb=nb):
                acc_scr[0:nb * sub, :] += ffn_rows(nb * sub)

        @pl.when(jnp.logical_not(full) & last)
        def _():
            pack_y(lambda r0, r1: acc_scr[r0:r1, :])

    @pl.when(jnp.logical_not(live) & (c == 0))
    def _():
        ys_ref[...] = jnp.zeros_like(ys_ref)


def _experts(tile_expert, tile_valid, n_used, xs2d, w1, b1, w2, b2, layer, tm, sub, m_pad):
    _, n_exp, d, h2 = w1.shape
    hid = h2 // 2
    tc = min(hid // 2, 512)
    nc = hid // tc
    assert nc >= 2
    n_tiles = m_pad // tm
    slab = d // 2 // LANES

    def live(i, nu):
        return jnp.minimum(i, nu[0] - 1)

    def chunk(i, c, nu):
        return jnp.where(i < nu[0], c, nc - 1)

    grid_spec = pltpu.PrefetchScalarGridSpec(
        num_scalar_prefetch=3,
        grid=(n_tiles, nc),
        in_specs=[
            pl.BlockSpec((tm * slab, LANES), lambda i, c, te, tv, nu: (live(i, nu), 0)),
            pl.BlockSpec((None, None, d, 2 * tc), lambda i, c, te, tv, nu: (layer, te[i], 0, chunk(i, c, nu))),
            pl.BlockSpec((None, None, 1, 2 * tc), lambda i, c, te, tv, nu: (layer, te[i], 0, chunk(i, c, nu))),
            pl.BlockSpec((None, None, tc, d), lambda i, c, te, tv, nu: (layer, te[i], chunk(i, c, nu), 0)),
            pl.BlockSpec((None, None, 1, d), lambda i, c, te, tv, nu: (layer, te[i], 0, 0)),
        ],
        out_specs=pl.BlockSpec((tm * slab, LANES), lambda i, c, te, tv, nu: (i, 0)),
        scratch_shapes=[pltpu.VMEM((tm, d), BF16), pltpu.VMEM((tm, d), F32)],
    )
    return pl.pallas_call(
        functools.partial(_expert_kernel, tm=tm, sub=sub, nc=nc),
        grid_spec=grid_spec,
        out_shape=jax.ShapeDtypeStruct((m_pad * slab, LANES), U32),
        compiler_params=_params("arbitrary", "arbitrary"),
        name="experts",
    )(tile_expert, tile_valid, n_used, xs2d, w1, b1, w2, b2)


def _combine_kernel(pos_ref, pos_next_ref, wts_ref, x_ref, g2_ref, fg_ref, ys_hbm, ys_flat_hbm, o_ref, buf, sem,
                    *, tm, final, n_steps):
    i = pl.program_id(0)
    n_slab = x_ref.shape[1] // 2 // LANES
    slot_rows = TOP_K * tm * n_slab
    slot = i % 2
    base = pl.multiple_of(slot * slot_rows, slot_rows)

    def issue_block(p_ref, to_slot):
        to_base = to_slot * slot_rows

        def issue(g, c):
            for u in range(ISSUE_UNROLL):
                tt = g * ISSUE_UNROLL + u
                for k in range(TOP_K):
                    row0 = pl.multiple_of(to_base + (k * tm + tt) * n_slab, n_slab)
                    pltpu.make_async_copy(ys_hbm.at[p_ref[k * tm + tt]], buf.at[pl.ds(row0, n_slab)],
                                          sem.at[to_slot]).start(priority=k % 2)
            return c

        lax.fori_loop(0, tm // ISSUE_UNROLL, issue, 0)

    @pl.when(i == 0)
    def _():
        issue_block(pos_ref, 0)

    @pl.when(i + 1 < n_steps)
    def _():
        issue_block(pos_next_ref, 1 - slot)

    pltpu.make_async_copy(ys_flat_hbm.at[pl.ds(0, slot_rows)], buf.at[pl.ds(base, slot_rows)], sem.at[slot]).wait()

    w_sq = jnp.concatenate([wts_ref[...], jnp.zeros((tm - TOP_K, tm), F32)], axis=0)
    w_t = w_sq.T
    w_k = [jnp.broadcast_to(w_t[:, k:k + 1], (tm, LANES)) for k in range(TOP_K)]
    lows, highs = [], []
    for s in range(n_slab):
        acc_lo = jnp.zeros((tm, LANES), F32)
        acc_hi = jnp.zeros((tm, LANES), F32)
        for k in range(TOP_K):
            lo, hi = _unpack_rows(buf[pl.ds(base + k * tm * n_slab + s, tm, stride=n_slab), :])
            acc_lo = acc_lo + w_k[k] * lo
            acc_hi = acc_hi + w_k[k] * hi
        lows.append(acc_lo)
        highs.append(acc_hi)
    moe = jnp.concatenate(lows + highs, axis=1)
    x2 = x_ref[...] + g2_ref[0] * moe
    if final:
        x2 = x2 * lax.rsqrt(jnp.mean(x2 * x2, axis=-1, keepdims=True) + RMS_EPS) * fg_ref[...]
    o_ref[...] = x2


def _combine(pos_flat, wts, x1, gate2, final_g, ys_rows, tokens_per_batch, final):
    n, d = x1.shape
    tm = GATHER_TOKENS
    steps_per_batch = tokens_per_batch // tm
    n_steps = n // tm
    m_pad, slab, _ = ys_rows.shape
    return pl.pallas_call(
        functools.partial(_combine_kernel, tm=tm, final=final, n_steps=n_steps),
        grid=(n_steps,),
        in_specs=[
            pl.BlockSpec((TOP_K * tm,), lambda i: (i,), memory_space=pltpu.SMEM),
            pl.BlockSpec((TOP_K * tm,), lambda i: (jnp.minimum(i + 1, n_steps - 1),), memory_space=pltpu.SMEM),
            pl.BlockSpec((TOP_K, tm), lambda i: (0, i)),
            pl.BlockSpec((tm, d), lambda i: (i, 0)),
            pl.BlockSpec((1, 1, d), lambda i: (i // steps_per_batch, 0, 0)),
            pl.BlockSpec((1, d), lambda i: (0, 0)),
            pl.BlockSpec(memory_space=pl.ANY),
            pl.BlockSpec(memory_space=pl.ANY),
        ],
        out_specs=pl.BlockSpec((tm, d), lambda i: (i, 0)),
        out_shape=jax.ShapeDtypeStruct((n, d), F32),
        scratch_shapes=[pltpu.VMEM((2 * TOP_K * tm * slab, LANES), U32), pltpu.SemaphoreType.DMA((2,))],
        compiler_params=_params("arbitrary"),
        name="combine",
    )(pos_flat, pos_flat, wts, x1, gate2, final_g, ys_rows, ys_rows.reshape(m_pad * slab, LANES))


def _routing_plan(idx, rank, counts, tm_e, n_tiles):
    experts = jnp.arange(N_EXPERTS, dtype=I32)
    cnt = counts[:, 0].astype(I32)
    tiles = (cnt + tm_e - 1) // tm_e
    tile_end = jnp.cumsum(tiles)
    tile_start = tile_end - tiles
    offsets = tile_start * tm_e
    pos = rank + jnp.sum(jnp.where(idx[..., None] == experts, offsets, 0), axis=-1)
    n_used = tile_end[-1]
    tile_ids = jnp.arange(n_tiles, dtype=I32)
    te = jnp.sum((tile_ids[:, None] >= tile_end[None, :]).astype(I32), axis=1)
    te_last = jnp.sum((n_used - 1 >= tile_end).astype(I32))
    te = jnp.where(tile_ids < n_used, te, te_last).astype(I32)
    mine = te[:, None] == experts[None, :]
    rows_left = jnp.sum(jnp.where(mine, cnt[None, :] - (tile_ids[:, None] - tile_start[None, :]) * tm_e, 0), axis=1)
    tile_valid = jnp.where(tile_ids < n_used, jnp.clip(rows_left, 0, tm_e), 0).astype(I32)
    zero_start = jnp.where(tiles > 0, (tile_end - 1) * tm_e, -1).astype(I32)
    return pos.astype(I32), te, tile_valid, n_used.reshape(1).astype(I32), zero_start


def kernel(x, c, positions, mod_w, mod_b, norm1_g, norm2_g, w_in, pool_w, pool_scale, diff_lq1, diff_lk1,
           diff_lq2, diff_lk2, diff_subln_g, conv_dw_w, conv_dw_b, conv_ln_g, conv_ln_b, conv_pw_w, conv_pw_b,
           w_out, router_w, router_b, exp_w1, exp_b1, exp_w2, exp_b2, final_g):
    b, t, d = x.shape
    depth = mod_w.shape[0]
    n = b * t
    tm_e = 1024
    sub_e = 256
    m_pad = n * TOP_K + N_EXPERTS * tm_e
    n_tiles = m_pad // tm_e
    slab = d // 2 // LANES

    def row3(a):
        return a.reshape(a.shape[0], 1, a.shape[1])

    mod = _modulation(c, mod_w, mod_b)
    tables = _rope_tables(positions)
    w_out_b = w_out.astype(BF16)
    pool_w_b = pool_w.astype(BF16)
    pw_w_b = conv_pw_w.astype(BF16)
    router_wt = jnp.swapaxes(router_w, 1, 2)
    router_b3 = router_b.reshape(depth, N_EXPERTS, 1)
    b1r = exp_b1.reshape(depth, N_EXPERTS, 1, exp_b1.shape[-1])
    b2r = exp_b2.reshape(depth, N_EXPERTS, 1, d)
    final_g2 = final_g.reshape(1, d)

    for l in range(depth):
        lambda_init = 0.8 - 0.6 * math.exp(-0.3 * l)
        sh1, sc1, g1, sh2, sc2, g2 = [m.reshape(b, 1, d) for m in jnp.split(mod[l], 6, axis=-1)]
        z = _inproj(x, row3(norm1_g), sc1, sh1, w_in, tables, l)
        y_pool, y_conv = _poolconv(z, pool_w_b, row3(pool_scale), conv_dw_w, row3(conv_dw_b), row3(conv_ln_g),
                                   row3(conv_ln_b), pw_w_b, row3(conv_pw_b), l)
        y_diff = _diff_attention(z, row3(diff_lq1), row3(diff_lk1), row3(diff_lq2), row3(diff_lk2),
                                 row3(diff_subln_g), l, lambda_init)
        y_dil = _dilated_attention(z)
        x1, h2, idx, rank, wts, counts = _outproj_router(
            (y_pool, y_diff, y_dil, y_conv), w_out_b, x, g1, row3(norm2_g), sc2, sh2, router_wt, router_b3, l)
        pos, tile_expert, tile_valid, n_used, zero_start = _routing_plan(idx, rank, counts, tm_e, n_tiles)
        xs = _scatter_rows(zero_start, _block_major_positions(pos, SCATTER_TOKENS), h2.reshape(n, slab, LANES),
                           m_pad, tm_e)
        ys = _experts(tile_expert, tile_valid, n_used, xs.reshape(m_pad * slab, LANES), exp_w1, b1r, exp_w2, b2r,
                      l, tm_e, sub_e, m_pad)
        x = _combine(_block_major_positions(pos, GATHER_TOKENS), wts, x1.reshape(n, d), g2, final_g2,
                     ys.reshape(m_pad, slab, LANES), t, final=(l == depth - 1)).reshape(b, t, d)
    return x
```
